```python
import jax
import jax.numpy as jnp
from jax import lax
import numpy as np

D_MODEL = 1024
BATCH = 16
SEQ = 256
DEPTH = 2
DEC_BATCH = 2
DEC_SEQ = 4096
PAST_LEN = 512

GRID_W = 64
Q_BLOCK = 128
ROPE_THETA = 10000.0
EPS = 1e-6

GQA_HEADS = 8
GQA_KV_HEADS = 2
GQA_REP = GQA_HEADS // GQA_KV_HEADS
GQA_HEAD_DIM = 64
GQA_WIDTH = GQA_HEADS * GQA_HEAD_DIM
GQA_KV_WIDTH = GQA_KV_HEADS * GQA_HEAD_DIM
GLA_HEADS = 4
GLA_DK = 64
GLA_DV = 128
GLA_WIDTH = GLA_HEADS * GLA_DV
GLA_K_WIDTH = GLA_HEADS * GLA_DK
GLA_RANK = 16
GLA_NORMALIZER = 16.0
GLA_CHUNK = 64
MLA_HEADS = 4
MLA_Q_LORA = 256
MLA_KV_LORA = 256
MLA_NOPE_DIM = 64
MLA_ROPE_DIM = 32
MLA_V_DIM = 128
MLA_WIDTH = MLA_HEADS * MLA_V_DIM

IN_SPLITS = (GQA_WIDTH, GQA_KV_WIDTH, GQA_KV_WIDTH, GQA_WIDTH,
             GLA_K_WIDTH, GLA_K_WIDTH, GLA_WIDTH, GLA_WIDTH, GLA_RANK, GLA_RANK,
             MLA_Q_LORA, MLA_KV_LORA, MLA_ROPE_DIM, MLA_WIDTH,
             D_MODEL, D_MODEL, D_MODEL)
N_IN = sum(IN_SPLITS)

kernel_name = 'hybrid_diffusion_gqa_gla_mla_step'


def rms_norm(x, g):
    xf = x.astype(jnp.float32)
    y = xf * lax.rsqrt(jnp.mean(xf * xf, axis=-1, keepdims=True) + EPS)
    return (y * g.astype(jnp.float32)).astype(x.dtype)


def rope_1d(x, pos):
    half = x.shape[-1] // 2
    freqs = ROPE_THETA ** (-jnp.arange(half, dtype=jnp.float32) / half)
    ang = pos[:, None] * freqs
    ang = ang.reshape((1, ang.shape[0]) + (1,) * (x.ndim - 3) + (half,))
    cos, sin = jnp.cos(ang), jnp.sin(ang)
    xf = x.astype(jnp.float32)
    x1, x2 = xf[..., :half], xf[..., half:]
    return jnp.concatenate([x1 * cos - x2 * sin, x1 * sin + x2 * cos], axis=-1).astype(x.dtype)


def axial_rope(x, row, col):
    h = x.shape[-1] // 2
    return jnp.concatenate([rope_1d(x[..., :h], row), rope_1d(x[..., h:], col)], axis=-1)


def block_attention(q, k, v):
    b, nq, g, r, dq = q.shape
    dv = v.shape[-1]
    scale = dq ** -0.5
    qb = q.reshape(b, nq // Q_BLOCK, Q_BLOCK, g, r, dq).swapaxes(0, 1)

    def one_block(qi):
        s = jnp.einsum('bqgrd,bkgd->bgrqk', qi, k).astype(jnp.float32) * scale
        p = jax.nn.softmax(s, axis=-1).astype(v.dtype)
        return jnp.einsum('bgrqk,bkgv->bqgrv', p, v)

    o = lax.map(one_block, qb)
    return o.swapaxes(0, 1).reshape(b, nq, g * r, dv)


def gla_chunked(q, k, v, log_a, s0):
    b, n, h, dk = q.shape
    dv = v.shape[-1]
    nc = n // GLA_CHUNK

    def chunks(t):
        return t.astype(jnp.float32).reshape((b, nc, GLA_CHUNK) + t.shape[2:]).swapaxes(0, 1)

    mask = jnp.tril(jnp.ones((GLA_CHUNK, GLA_CHUNK), dtype=bool))[None, :, :, None, None]

    def step(s, inp):
        qc, kc, vc, gc = inp
        cum = jnp.cumsum(gc, axis=1)
        last = cum[:, -1]
        o_inter = jnp.einsum('bchk,bhkv->bchv', qc * jnp.exp(cum), s)
        decay = jnp.exp(jnp.where(mask, cum[:, :, None] - cum[:, None, :], -jnp.inf))
        att = jnp.einsum('bihk,bjhk,bijhk->bhij', qc, kc, decay)
        o_intra = jnp.einsum('bhij,bjhv->bihv', att, vc)
        s_new = jnp.exp(last)[..., None] * s + jnp.einsum(
            'bjhk,bjhv->bhkv', kc * jnp.exp(last[:, None] - cum), vc)
        return s_new, o_inter + o_intra

    s_fin, o = lax.scan(step, s0.astype(jnp.float32), (chunks(q), chunks(k), chunks(v), chunks(log_a)))
    return o.swapaxes(0, 1).reshape(b, n, h, dv), s_fin


def gla_bidirectional(q, k, v, la_f, la_b, s0_f, s0_b):
    flip = lambda t: jnp.flip(t, axis=1)
    o_f, s_f = gla_chunked(q, k, v, la_f, s0_f)
    o_b, s_b = gla_chunked(flip(q), flip(k), flip(v), flip(la_b), s0_b)
    return o_f + flip(o_b), s_f, s_b


def modulation(cond, w, b):
    m = (jax.nn.silu(cond) @ w + b)[..., None, :]
    return jnp.split(m, 3, axis=-1)


def mixer(h, p, pos, cache):
    bsz, n, _ = h.shape
    z = h @ p['w_in']
    offs = [int(i) for i in np.cumsum(IN_SPLITS)[:-1]]
    (qa, ka, va, ga, qg, kg, vg, gg, rf, rb,
     qlat, kvlat, kr, gc, m1, m2, m3) = jnp.split(z, offs, axis=-1)
    is_ctx = cache is None

    qa = rms_norm(qa.reshape(bsz, n, GQA_KV_HEADS, GQA_REP, GQA_HEAD_DIM), p['g_q_norm'])
    ka = rms_norm(ka.reshape(bsz, n, GQA_KV_HEADS, GQA_HEAD_DIM), p['g_k_norm'])
    va = va.reshape(bsz, n, GQA_KV_HEADS, GQA_HEAD_DIM)
    if is_ctx:
        k_all, v_all = ka, va
    else:
        row, col = pos
        qa = axial_rope(qa, row, col)
        k_all = jnp.concatenate([cache['gqa_k'].astype(h.dtype), axial_rope(ka, row, col)], axis=1)
        v_all = jnp.concatenate([cache['gqa_v'].astype(h.dtype), va], axis=1)
    y_a = block_attention(qa, k_all, v_all).reshape(bsz, n, GQA_WIDTH)
    y_a = (y_a * jax.nn.silu(ga)) @ p['w_o_gqa']

    qg = qg.reshape(bsz, n, GLA_HEADS, GLA_DK) * (GLA_DK ** -0.5)
    kg = kg.reshape(bsz, n, GLA_HEADS, GLA_DK)
    vg = vg.reshape(bsz, n, GLA_HEADS, GLA_DV)
    la_f = jax.nn.log_sigmoid((rf @ p['w_gla_decay_fwd'] + p['b_gla_decay_fwd']).astype(jnp.float32))
    la_b = jax.nn.log_sigmoid((rb @ p['w_gla_decay_bwd'] + p['b_gla_decay_bwd']).astype(jnp.float32))
    la_f = la_f.reshape(bsz, n, GLA_HEADS, GLA_DK) / GLA_NORMALIZER
    la_b = la_b.reshape(bsz, n, GLA_HEADS, GLA_DK) / GLA_NORMALIZER
    if is_ctx:
        s0_f = jnp.zeros((bsz, GLA_HEADS, GLA_DK, GLA_DV), jnp.float32)
        s0_b = s0_f
    else:
        s0_f, s0_b = cache['gla_fwd'], cache['gla_bwd']
    o_g, s_f, s_b = gla_bidirectional(qg, kg, vg, la_f, la_b, s0_f, s0_b)
    o_g = rms_norm(o_g.astype(h.dtype), p['g_gla_out']).reshape(bsz, n, GLA_WIDTH)
    y_b = (o_g * jax.nn.silu(gg)) @ p['w_o_gla']

    cq = (rms_norm(qlat, p['g_mla_q']) @ p['w_mla_uq']).reshape(bsz, n, MLA_HEADS, MLA_NOPE_DIM + MLA_ROPE_DIM)
    q_nope, q_rope = cq[..., :MLA_NOPE_DIM], cq[..., MLA_NOPE_DIM:]
    ckv = rms_norm(kvlat, p['g_mla_kv'])
    if is_ctx:
        ckv_all, kr_all = ckv, kr
    else:
        q_rope = axial_rope(q_rope, row, col)
        kr_lat = axial_rope(kr[:, :, None, :], row, col)[:, :, 0]
        ckv_all = jnp.concatenate([cache['mla_ckv'].astype(h.dtype), ckv], axis=1)
        kr_all = jnp.concatenate([cache['mla_krope'].astype(h.dtype), kr_lat], axis=1)
    nk = ckv_all.shape[1]
    kv = (ckv_all @ p['w_mla_ukv']).reshape(bsz, nk, MLA_HEADS, MLA_NOPE_DIM + MLA_V_DIM)
    k_nope, v_c = kv[..., :MLA_NOPE_DIM], kv[..., MLA_NOPE_DIM:]
    k_c = jnp.concatenate(
        [k_nope, jnp.broadcast_to(kr_all[:, :, None, :], (bsz, nk, MLA_HEADS, MLA_ROPE_DIM))], axis=-1)
    q_c = jnp.concatenate([q_nope, q_rope], axis=-1)[:, :, :, None, :]
    y_c = block_attention(q_c, k_c, v_c).reshape(bsz, n, MLA_WIDTH)
    y_c = (y_c * jax.nn.silu(gc)) @ p['w_o_mla']

    merged = jax.nn.sigmoid(m1) * y_a + jax.nn.sigmoid(m2) * y_b + jax.nn.sigmoid(m3) * y_c
    out = merged @ p['w_out']
    ctx_tensors = (ka, va, ckv, kr, s_f, s_b) if is_ctx else None
    return out, ctx_tensors


def setup_inputs(seed: int = 0) -> dict:
    key = jax.random.key(seed)
    keys = iter(jax.random.split(key, 40))
    L, D = DEPTH, D_MODEL

    def nrm(shape, scale):
        return jax.random.normal(next(keys), shape, jnp.float32) * scale

    return {
        'x_prompt': nrm((BATCH, SEQ, D), 1.0),
        'x_sample': nrm((DEC_BATCH, DEC_SEQ, D), 1.0),
        'cache_gqa_k': nrm((DEC_BATCH, DEPTH, PAST_LEN, GQA_KV_HEADS, GQA_HEAD_DIM), 1.0),
        'cache_gqa_v': nrm((DEC_BATCH, DEPTH, PAST_LEN, GQA_KV_HEADS, GQA_HEAD_DIM), 1.0),
        'cache_mla_ckv': nrm((DEC_BATCH, DEPTH, PAST_LEN, MLA_KV_LORA), 1.0),
        'cache_mla_krope': nrm((DEC_BATCH, DEPTH, PAST_LEN, MLA_ROPE_DIM), 1.0),
        'state_gla_fwd': nrm((DEC_BATCH, DEPTH, GLA_HEADS, GLA_DK, GLA_DV), 0.5),
        'state_gla_bwd': nrm((DEC_BATCH, DEPTH, GLA_HEADS, GLA_DK, GLA_DV), 0.5),
        'c': nrm((DEC_BATCH, D), 1.0),
        'c_ctx': nrm((D,), 1.0),
        'w_mod': nrm((L, D, 3 * D), 0.5 * D ** -0.5),
        'b_mod': nrm((L, 3 * D), 0.02),
        'g_pre': 1.0 + nrm((L, D), 0.02),
        'g_post': 1.0 + nrm((L, D), 0.02),
        'w_in': nrm((L, D, N_IN), D ** -0.5),
        'g_q_norm': 1.0 + nrm((L, GQA_HEAD_DIM), 0.02),
        'g_k_norm': 1.0 + nrm((L, GQA_HEAD_DIM), 0.02),
        'w_gla_decay_fwd': nrm((L, GLA_RANK, GLA_K_WIDTH), GLA_RANK ** -0.5),
        'b_gla_decay_fwd': nrm((L, GLA_K_WIDTH), 0.1),
        'w_gla_decay_bwd': nrm((L, GLA_RANK, GLA_K_WIDTH), GLA_RANK ** -0.5),
        'b_gla_decay_bwd': nrm((L, GLA_K_WIDTH), 0.1),
        'g_gla_out': 1.0 + nrm((L, GLA_DV), 0.02),
        'g_mla_q': 1.0 + nrm((L, MLA_Q_LORA), 0.02),
        'g_mla_kv': 1.0 + nrm((L, MLA_KV_LORA), 0.02),
        'w_mla_uq': nrm((L, MLA_Q_LORA, MLA_HEADS * (MLA_NOPE_DIM + MLA_ROPE_DIM)), MLA_Q_LORA ** -0.5),
        'w_mla_ukv': nrm((L, MLA_KV_LORA, MLA_HEADS * (MLA_NOPE_DIM + MLA_V_DIM)), MLA_KV_LORA ** -0.5),
        'w_o_gqa': nrm((L, GQA_WIDTH, D), GQA_WIDTH ** -0.5),
        'w_o_gla': nrm((L, GLA_WIDTH, D), GLA_WIDTH ** -0.5),
        'w_o_mla': nrm((L, MLA_WIDTH, D), MLA_WIDTH ** -0.5),
        'w_out': nrm((L, D, D), D ** -0.5),
    }


def reference(x_prompt, x_sample, cache_gqa_k, cache_gqa_v, cache_mla_ckv, cache_mla_krope,
              state_gla_fwd, state_gla_bwd, c, c_ctx, w_mod, b_mod, g_pre, g_post, w_in,
              g_q_norm, g_k_norm, w_gla_decay_fwd, b_gla_decay_fwd, w_gla_decay_bwd,
              b_gla_decay_bwd, g_gla_out, g_mla_q, g_mla_kv, w_mla_uq, w_mla_ukv,
              w_o_gqa, w_o_gla, w_o_mla, w_out):
    rows = x_sample.shape[1] // GRID_W
    row = jnp.repeat(jnp.arange(rows), GRID_W).astype(jnp.float32)
    col = jnp.tile(jnp.arange(GRID_W), rows).astype(jnp.float32)

    def layer_params(l):
        return {
            'w_in': w_in[l], 'g_q_norm': g_q_norm[l], 'g_k_norm': g_k_norm[l],
            'w_gla_decay_fwd': w_gla_decay_fwd[l], 'b_gla_decay_fwd': b_gla_decay_fwd[l],
            'w_gla_decay_bwd': w_gla_decay_bwd[l], 'b_gla_decay_bwd': b_gla_decay_bwd[l],
            'g_gla_out': g_gla_out[l], 'g_mla_q': g_mla_q[l], 'g_mla_kv': g_mla_kv[l],
            'w_mla_uq': w_mla_uq[l], 'w_mla_ukv': w_mla_ukv[l],
            'w_o_gqa': w_o_gqa[l], 'w_o_gla': w_o_gla[l], 'w_o_mla': w_o_mla[l], 'w_out': w_out[l],
        }

    def sub_layer(x, cond, l, pos, cache):
        shift, scale, gate = modulation(cond, w_mod[l], b_mod[l])
        h = rms_norm(x, g_pre[l]) * (1 + scale) + shift
        out, ctx = mixer(h, layer_params(l), pos, cache)
        return x + gate * rms_norm(out, g_post[l]), ctx

    xp = x_prompt
    ks, vs, ckvs, krs, sfs, sbs = [], [], [], [], [], []
    for l in range(DEPTH):
        xp, ctx = sub_layer(xp, c_ctx, l, None, None)
        ks.append(ctx[0]); vs.append(ctx[1]); ckvs.append(ctx[2])
        krs.append(ctx[3]); sfs.append(ctx[4]); sbs.append(ctx[5])
    y_prompt = xp

    xs = x_sample
    for l in range(DEPTH):
        cache = {
            'gqa_k': cache_gqa_k[:, l], 'gqa_v': cache_gqa_v[:, l],
            'mla_ckv': cache_mla_ckv[:, l], 'mla_krope': cache_mla_krope[:, l],
            'gla_fwd': state_gla_fwd[:, l], 'gla_bwd': state_gla_bwd[:, l],
        }
        xs, _ = sub_layer(xs, c, l, (row, col), cache)
    y_sample = xs

    return (y_prompt, y_sample, jnp.stack(ks, axis=1), jnp.stack(vs, axis=1),
            jnp.stack(ckvs, axis=1), jnp.stack(krs, axis=1),
            jnp.stack(sfs, axis=1), jnp.stack(sbs, axis=1))
```

```python
import functools

import numpy as np
import jax
import jax.numpy as jnp
from jax import lax
from jax.experimental import pallas as pl
from jax.experimental.pallas import tpu as pltpu

F32 = jnp.float32
BF16 = jnp.bfloat16

EPS = 1e-6
ROPE_THETA = 10000.0
GRID_W = 64

GQA_HEADS, GQA_KV_HEADS, GQA_HEAD_DIM = 8, 2, 64
GQA_REP = GQA_HEADS // GQA_KV_HEADS
GQA_WIDTH = GQA_HEADS * GQA_HEAD_DIM
GQA_KV_WIDTH = GQA_KV_HEADS * GQA_HEAD_DIM
GLA_HEADS, GLA_DK, GLA_DV = 4, 64, 128
GLA_WIDTH = GLA_HEADS * GLA_DV
GLA_K_WIDTH = GLA_HEADS * GLA_DK
GLA_RANK = 16
GLA_NORMALIZER = 16.0
GLA_CHUNK = 64
MLA_HEADS, MLA_Q_LORA, MLA_KV_LORA = 4, 256, 256
MLA_NOPE_DIM, MLA_ROPE_DIM, MLA_V_DIM = 64, 32, 128
MLA_WIDTH = MLA_HEADS * MLA_V_DIM
MLA_QK_PAD = 128
MLA_ROPE_LANE = MLA_NOPE_DIM

LANES = 128
VMEM_LIMIT = 56 * 1024 * 1024


def _sigmoid(x):
    return 1.0 / (1.0 + jnp.exp(-x))


def _silu(x):
    return x * _sigmoid(x)


def _log_sigmoid(x):
    return jnp.minimum(x, 0.0) - jnp.log(1.0 + jnp.exp(-jnp.abs(x)))


def _dot(a, b):
    return jnp.dot(a, b, preferred_element_type=F32)


def _dot_nt(a, b):
    return lax.dot_general(a, b, (((1,), (1,)), ((), ())), preferred_element_type=F32)


def _dot_tn(a, b):
    return lax.dot_general(a, b, (((0,), (0,)), ((), ())), preferred_element_type=F32)


def _split3(x):
    hi = x.astype(BF16)
    r1 = x - hi.astype(F32)
    mid = r1.astype(BF16)
    lo = (r1 - mid.astype(F32)).astype(BF16)
    return hi, mid, lo


def _segment_mean_sq(x, ones_bf16, width):
    sq = x * x
    hi = sq.astype(BF16)
    lo = (sq - hi.astype(F32)).astype(BF16)
    return (_dot(hi, ones_bf16) + _dot(lo, ones_bf16)) * (1.0 / width)


def _swap_halves(x, half):
    n = x.shape[-1]
    lane = lax.broadcasted_iota(jnp.int32, x.shape, x.ndim - 1)
    first = (lane % (2 * half)) < half
    return jnp.where(first, pltpu.roll(x, n - half, x.ndim - 1), pltpu.roll(x, half, x.ndim - 1))


def _rope(x, cos, sin_signed, half):
    return x * cos + _swap_halves(x, half) * sin_signed


def _tile_lanes(t, reps):
    return t if reps == 1 else jnp.concatenate([t] * reps, axis=-1)


def _mod_kernel(c_ref, w_ref, b_ref, o_ref):
    c = c_ref[...]
    o_ref[0] = _dot(_silu(c).astype(BF16), w_ref[0].astype(BF16)) + b_ref[0]


def _modulation(conds, w_mod, b_mod):
    L, D, D3 = w_mod.shape
    nj = D3 // D
    return pl.pallas_call(
        _mod_kernel,
        grid=(L, nj),
        in_specs=[pl.BlockSpec((8, D), lambda l, j: (0, 0)),
                  pl.BlockSpec((1, D, D), lambda l, j: (l, 0, j)),
                  pl.BlockSpec((1, 1, D), lambda l, j: (l, 0, j))],
        out_specs=pl.BlockSpec((1, 8, D), lambda l, j: (l, 0, j)),
        out_shape=jax.ShapeDtypeStruct((L, 8, D3), F32),
        compiler_params=pltpu.CompilerParams(dimension_semantics=("parallel", "parallel")),
        name="modulation",
    )(conds, w_mod, b_mod.reshape(L, 1, D3))


_C_QA, _C_KV, _C_GA, _C_QKG, _C_VG, _C_GG, _C_R, _C_LAT, _C_KR, _C_GC, _C_M = (
    0, 512, 768, 1280, 1792, 2304, 2816, 2944, 3456, 3584, 4096)
_N_PACKED = 7168


def _in_proj_kernel(*refs, is_ctx, d_model):
    it = iter(refs)
    x_ref, mod_ref, gpre_ref, w_ref, gq_ref, gk_ref, ones_ref, wdec_ref, bdec_ref = (next(it) for _ in range(9))
    gmq_ref, gmkv_ref, wuq_ref, wk_ref, wv_ref = (next(it) for _ in range(5))
    if not is_ctx:
        ca_ref, sa_ref, cc_ref, sc_ref = (next(it) for _ in range(4))
    (qa_o, ka_o, va_o, ga_o, qg_o, kg_o, vg_o, gg_o, laf_o, lab_o,
     qc_o, kc_o, vc_o, gc_o, m1_o, m2_o, m3_o) = (next(it) for _ in range(17))
    if is_ctx:
        ka32_o, va32_o, ckv32_o, kr32_o = (next(it) for _ in range(4))

    D = d_model
    x = x_ref[0]
    mod = mod_ref[0]
    shift, scale = mod[:, :D], mod[:, D:2 * D]
    xn = x * lax.rsqrt(jnp.mean(x * x, axis=-1, keepdims=True) + EPS) * gpre_ref[...]
    hb = (xn * (1.0 + scale) + shift).astype(BF16)

    def proj(c0, width):
        return _dot(hb, w_ref[:, c0:c0 + width])

    ones = ones_ref[...]

    qa = proj(_C_QA, GQA_WIDTH)
    qa = qa * lax.rsqrt(_segment_mean_sq(qa, ones, GQA_HEAD_DIM) + EPS) * gq_ref[...]
    kv = proj(_C_KV, 2 * GQA_KV_WIDTH)
    ka, va = kv[:, :GQA_KV_WIDTH], kv[:, GQA_KV_WIDTH:]
    ka = ka * lax.rsqrt(_segment_mean_sq(ka, ones[:GQA_KV_WIDTH, :GQA_KV_WIDTH], GQA_HEAD_DIM) + EPS) * gk_ref[...]
    if is_ctx:
        ka32_o[0] = ka
        va32_o[0] = va
    else:
        ca, sa = ca_ref[...], sa_ref[...]
        qa = _rope(qa, _tile_lanes(ca, GQA_WIDTH // LANES), _tile_lanes(sa, GQA_WIDTH // LANES), GQA_HEAD_DIM // 4)
        ka = _rope(ka, ca, sa, GQA_HEAD_DIM // 4)
    qa = (qa * GQA_HEAD_DIM ** -0.5).astype(BF16)
    for g in range(GQA_KV_HEADS):
        for r in range(GQA_REP):
            hd = (g * GQA_REP + r) * GQA_HEAD_DIM
            qa_o[0, g, r] = qa[:, hd:hd + GQA_HEAD_DIM]
    ka_o[0] = ka.astype(BF16)
    va_o[0] = va.astype(BF16)
    ga_o[0] = _silu(proj(_C_GA, GQA_WIDTH))

    qkg = proj(_C_QKG, 2 * GLA_K_WIDTH)
    qg = qkg[:, :GLA_K_WIDTH] * GLA_DK ** -0.5
    kg = qkg[:, GLA_K_WIDTH:]
    vg = proj(_C_VG, GLA_WIDTH)
    r = proj(_C_R, LANES).astype(BF16)
    la = _log_sigmoid(_dot(r, wdec_ref[...]) + bdec_ref[...]) * (1.0 / GLA_NORMALIZER)
    for h in range(GLA_HEADS):
        kq = slice(h * GLA_DK, (h + 1) * GLA_DK)
        qg_o[0, h] = qg[:, kq]
        kg_o[0, h] = kg[:, kq]
        laf_o[0, h] = la[:, h * GLA_DK:(h + 1) * GLA_DK]
        lab_o[0, h] = la[:, GLA_K_WIDTH + h * GLA_DK:GLA_K_WIDTH + (h + 1) * GLA_DK]
        vg_o[0, h] = vg[:, h * GLA_DV:(h + 1) * GLA_DV]
    gg_o[0] = _silu(proj(_C_GG, GLA_WIDTH))

    lat = proj(_C_LAT, MLA_Q_LORA + MLA_KV_LORA)
    ql, kvl = lat[:, :MLA_Q_LORA], lat[:, MLA_Q_LORA:]
    ql = ql * lax.rsqrt(jnp.mean(ql * ql, axis=-1, keepdims=True) + EPS) * gmq_ref[...]
    ckv = kvl * lax.rsqrt(jnp.mean(kvl * kvl, axis=-1, keepdims=True) + EPS) * gmkv_ref[...]
    qc = _dot(ql.astype(BF16), wuq_ref[...])
    krb = proj(_C_KR, LANES)
    if is_ctx:
        ckv32_o[0] = ckv
        kr32_o[0] = krb[:, :MLA_ROPE_DIM]
    else:
        cc, sc = cc_ref[...], sc_ref[...]
        qc = _rope(qc, _tile_lanes(cc, MLA_HEADS), _tile_lanes(sc, MLA_HEADS), MLA_ROPE_DIM // 4)
        krb = _rope(krb, cc, sc, MLA_ROPE_DIM // 4)
    lane = lax.broadcasted_iota(jnp.int32, krb.shape, 1)
    kr_part = jnp.where((lane >= MLA_ROPE_LANE) & (lane < MLA_ROPE_LANE + MLA_ROPE_DIM), krb, 0.0)
    qc = (qc * (MLA_NOPE_DIM + MLA_ROPE_DIM) ** -0.5).astype(BF16)
    ckvb = ckv.astype(BF16)
    kc = _dot(ckvb, wk_ref[...]) + _tile_lanes(kr_part, MLA_HEADS)
    vc = _dot(ckvb, wv_ref[...])
    for h in range(MLA_HEADS):
        qc_o[0, h] = qc[:, h * MLA_QK_PAD:(h + 1) * MLA_QK_PAD]
    kc_o[0] = kc.astype(BF16)
    vc_o[0] = vc.astype(BF16)
    gc_o[0] = _silu(proj(_C_GC, MLA_WIDTH))

    m1_o[0] = _sigmoid(proj(_C_M, D))
    m2_o[0] = _sigmoid(proj(_C_M + D, D))
    m3_o[0] = _sigmoid(proj(_C_M + 2 * D, D))


def _const_spec(shape):
    nd = len(shape)
    return pl.BlockSpec(shape, lambda b, i, _nd=nd: (0,) * _nd, pipeline_mode=pl.Buffered(1))


def _in_proj(x, mod, lw, rope, *, is_ctx, tm):
    B, N, D = x.shape
    per_batch_mod = mod.shape[0] > 1
    row3 = lambda w: pl.BlockSpec((1, tm, w), lambda b, i: (b, i, 0))
    head4 = lambda h, w: pl.BlockSpec((1, h, tm, w), lambda b, i: (b, 0, i, 0))
    consts = [lw["g_pre"], lw["w_in"], lw["g_q"], lw["g_k"], lw["ones"], lw["w_dec"], lw["b_dec"],
              lw["g_mla_q"], lw["g_mla_kv"], lw["w_uq"], lw["w_uk"], lw["w_uv"]]
    in_specs = [row3(D),
                pl.BlockSpec((1, 1, 3 * D), (lambda b, i: (b, 0, 0)) if per_batch_mod else (lambda b, i: (0, 0, 0)))]
    in_specs += [_const_spec(c.shape) for c in consts]
    args = [x, mod] + consts
    if not is_ctx:
        in_specs += [pl.BlockSpec((tm, LANES), lambda b, i: (i, 0))] * 4
        args += list(rope)
    sds = jax.ShapeDtypeStruct
    out_shape = [
        sds((B, GQA_KV_HEADS, GQA_REP, N, GQA_HEAD_DIM), BF16), sds((B, N, GQA_KV_WIDTH), BF16),
        sds((B, N, GQA_KV_WIDTH), BF16), sds((B, N, GQA_WIDTH), F32),
        sds((B, GLA_HEADS, N, GLA_DK), F32), sds((B, GLA_HEADS, N, GLA_DK), F32),
        sds((B, GLA_HEADS, N, GLA_DV), F32), sds((B, N, GLA_WIDTH), F32),
        sds((B, GLA_HEADS, N, GLA_DK), F32), sds((B, GLA_HEADS, N, GLA_DK), F32),
        sds((B, MLA_HEADS, N, MLA_QK_PAD), BF16), sds((B, N, MLA_HEADS * MLA_QK_PAD), BF16),
        sds((B, N, MLA_WIDTH), BF16), sds((B, N, MLA_WIDTH), F32),
        sds((B, N, D), F32), sds((B, N, D), F32), sds((B, N, D), F32)]
    out_specs = [
        pl.BlockSpec((1, GQA_KV_HEADS, GQA_REP, tm, GQA_HEAD_DIM), lambda b, i: (b, 0, 0, i, 0)),
        row3(GQA_KV_WIDTH), row3(GQA_KV_WIDTH), row3(GQA_WIDTH),
        head4(GLA_HEADS, GLA_DK), head4(GLA_HEADS, GLA_DK), head4(GLA_HEADS, GLA_DV), row3(GLA_WIDTH),
        head4(GLA_HEADS, GLA_DK), head4(GLA_HEADS, GLA_DK),
        head4(MLA_HEADS, MLA_QK_PAD), row3(MLA_HEADS * MLA_QK_PAD), row3(MLA_WIDTH), row3(MLA_WIDTH),
        row3(D), row3(D), row3(D)]
    if is_ctx:
        out_shape += [sds((B, N, GQA_KV_WIDTH), F32), sds((B, N, GQA_KV_WIDTH), F32),
                      sds((B, N, MLA_KV_LORA), F32), sds((B, N, MLA_ROPE_DIM), F32)]
        out_specs += [row3(GQA_KV_WIDTH), row3(GQA_KV_WIDTH), row3(MLA_KV_LORA), row3(MLA_ROPE_DIM)]
    return pl.pallas_call(
        functools.partial(_in_proj_kernel, is_ctx=is_ctx, d_model=D),
        grid=(B, N // tm),
        in_specs=in_specs, out_specs=out_specs, out_shape=out_shape,
        compiler_params=pltpu.CompilerParams(dimension_semantics=("parallel", "parallel"),
                                             vmem_limit_bytes=VMEM_LIMIT),
        name="in_proj_ctx" if is_ctx else "in_proj_lat",
    )(*args)


def _kv_up_kernel(ckv_ref, kr_ref, wk_ref, wv_ref, kc_o, vc_o):
    ckvb = ckv_ref[0].astype(BF16)
    kc_o[0] = (_dot(ckvb, wk_ref[...]) + _tile_lanes(kr_ref[0], MLA_HEADS)).astype(BF16)
    vc_o[0] = _dot(ckvb, wv_ref[...]).astype(BF16)


def _kv_up(ckv, kr_padded, w_uk, w_uv):
    B, P, R = ckv.shape
    W = MLA_HEADS * MLA_QK_PAD
    return pl.pallas_call(
        _kv_up_kernel,
        grid=(B,),
        in_specs=[pl.BlockSpec((1, P, R), lambda b: (b, 0, 0)),
                  pl.BlockSpec((1, P, LANES), lambda b: (b, 0, 0)),
                  pl.BlockSpec(w_uk.shape, lambda b: (0, 0)),
                  pl.BlockSpec(w_uv.shape, lambda b: (0, 0))],
        out_specs=[pl.BlockSpec((1, P, W), lambda b: (b, 0, 0)),
                   pl.BlockSpec((1, P, MLA_WIDTH), lambda b: (b, 0, 0))],
        out_shape=[jax.ShapeDtypeStruct((B, P, W), BF16), jax.ShapeDtypeStruct((B, P, MLA_WIDTH), BF16)],
        compiler_params=pltpu.CompilerParams(dimension_semantics=("parallel",)),
        name="kv_up_cache",
    )(ckv, kr_padded, w_uk, w_uv)


def _attention_kernel(q_ref, kt_ref, v_ref, o_ref, s_scr, *, tk):
    R, tq, dq = q_ref.shape[2:]
    nk = kt_ref.shape[3]
    rows = R * tq
    q = q_ref[0, 0].reshape(rows, dq)
    m_part = jnp.full((rows, LANES), -jnp.inf, F32)
    for c in range(nk // tk):
        s = _dot(q, kt_ref[0, 0, :, c * tk:(c + 1) * tk])
        s_scr[:, c * tk:(c + 1) * tk] = s
        for j in range(tk // LANES):
            m_part = jnp.maximum(m_part, s[:, j * LANES:(j + 1) * LANES])
    m = jnp.max(m_part, axis=1, keepdims=True)
    l_part = jnp.zeros((rows, LANES), F32)
    acc = jnp.zeros((rows, v_ref.shape[3]), F32)
    for c in range(nk // tk):
        p = jnp.exp(s_scr[:, c * tk:(c + 1) * tk] - m)
        for j in range(tk // LANES):
            l_part = l_part + p[:, j * LANES:(j + 1) * LANES]
        acc = acc + _dot(p.astype(BF16), v_ref[0, 0, c * tk:(c + 1) * tk, :])
    l = jnp.sum(l_part, axis=1, keepdims=True)
    o_ref[0, 0] = (acc / l).reshape(R, tq, acc.shape[-1])


def _attention(q, kt, v, *, tq, tk):
    B, G, R, Nq, dq = q.shape
    Nk, dv = v.shape[2], v.shape[3]
    tk = min(tk, Nk)
    return pl.pallas_call(
        functools.partial(_attention_kernel, tk=tk),
        grid=(B, G, Nq // tq),
        in_specs=[pl.BlockSpec((1, 1, R, tq, dq), lambda b, g, i: (b, g, 0, i, 0)),
                  pl.BlockSpec((1, 1, dq, Nk), lambda b, g, i: (b, g, 0, 0)),
                  pl.BlockSpec((1, 1, Nk, dv), lambda b, g, i: (b, g, 0, 0))],
        out_specs=pl.BlockSpec((1, 1, R, tq, dv), lambda b, g, i: (b, g, 0, i, 0)),
        out_shape=jax.ShapeDtypeStruct((B, G, R, Nq, dv), F32),
        scratch_shapes=[pltpu.VMEM((R * tq, Nk), F32)],
        compiler_params=pltpu.CompilerParams(dimension_semantics=("parallel", "parallel", "parallel"),
                                             vmem_limit_bytes=VMEM_LIMIT),
        name="attention",
    )(q, kt, v)


def _gla_kernel(qf_ref, kf_ref, vf_ref, gf_ref, qb_ref, kb_ref, vb_ref, gb_ref, s0f_ref, s0b_ref,
                of_ref, ob_ref, sf_ref, sb_ref, st_scr, *, nchunk):
    C = GLA_CHUNK
    i = pl.program_id(1)

    @pl.when(i == 0)
    def _():
        st_scr[0] = s0f_ref[0]
        st_scr[1] = s0b_ref[0]

    row = lax.broadcasted_iota(jnp.int32, (C, C), 0)
    col = lax.broadcasted_iota(jnp.int32, (C, C), 1)
    keep = (row >= col, row <= col)
    tri = tuple(jnp.where(kp, 1.0, 0.0).astype(BF16) for kp in keep)
    dirs = ((qf_ref, kf_ref, vf_ref, gf_ref, of_ref), (qb_ref, kb_ref, vb_ref, gb_ref, ob_ref))

    def body(c, carry):
        for d, (q_ref, k_ref, v_ref, g_ref, o_ref) in enumerate(dirs):
            cc = c if d == 0 else nchunk - 1 - c
            rows = pl.ds(pl.multiple_of(cc * C, C), C)
            for h in range(GLA_HEADS):
                q, k, v, g = q_ref[0, h, rows, :], k_ref[0, h, rows, :], v_ref[0, h, rows, :], g_ref[0, h, rows, :]
                cum = sum(_dot(tri[d], part) for part in _split3(g))
                last = cum[C - 1:C, :] if d == 0 else cum[0:1, :]
                qd = q * jnp.exp(cum)
                kd = (k * jnp.exp(last - cum)).astype(BF16)
                a = _dot_nt((qd * jnp.exp(-last)).astype(BF16), kd)
                a = jnp.where(keep[d], a, 0.0).astype(BF16)
                vb = v.astype(BF16)
                st = st_scr[d, h]
                o_ref[0, h, rows, :] = _dot_nt(qd.astype(BF16), st.astype(BF16)) + _dot(a, vb)
                st_scr[d, h] = st * jnp.exp(last) + _dot_tn(vb, kd)
        return carry

    lax.fori_loop(0, nchunk, body, 0)

    @pl.when(i == pl.num_programs(1) - 1)
    def _():
        sf_ref[0] = st_scr[0]
        sb_ref[0] = st_scr[1]


def _gla(qg, kg, vg, laf, lab, s0f_t, s0b_t, *, tb):
    B, H, N, dk = qg.shape
    dv = vg.shape[3]
    nblk = N // tb
    fwd = lambda w: pl.BlockSpec((1, H, tb, w), lambda b, i: (b, 0, i, 0))
    bwd = lambda w: pl.BlockSpec((1, H, tb, w), lambda b, i: (b, 0, nblk - 1 - i, 0))
    st = pl.BlockSpec((1, H, dv, dk), lambda b, i: (b, 0, 0, 0))
    sds = jax.ShapeDtypeStruct
    return pl.pallas_call(
        functools.partial(_gla_kernel, nchunk=tb // GLA_CHUNK),
        grid=(B, nblk),
        in_specs=[fwd(dk), fwd(dk), fwd(dv), fwd(dk), bwd(dk), bwd(dk), bwd(dv), bwd(dk), st, st],
        out_specs=[fwd(dv), bwd(dv), st, st],
        out_shape=[sds((B, H, N, dv), F32), sds((B, H, N, dv), F32),
                   sds((B, H, dv, dk), F32), sds((B, H, dv, dk), F32)],
        scratch_shapes=[pltpu.VMEM((2, H, dv, dk), F32)],
        compiler_params=pltpu.CompilerParams(dimension_semantics=("parallel", "arbitrary"),
                                             vmem_limit_bytes=VMEM_LIMIT),
        name="gla",
    )(qg, kg, vg, laf, qg, kg, vg, lab, s0f_t, s0b_t)


def _out_proj_kernel(x_ref, mod_ref, ya_ref, ga_ref, of_ref, ob_ref, gg_ref, yc_ref, gc_ref,
                     m1_ref, m2_ref, m3_ref, woa_ref, wog_ref, woc_ref, wout_ref, ggla_ref, gpost_ref,
                     o_ref, *, d_model):
    D = d_model
    ya = _dot((ya_ref[0] * ga_ref[0]).astype(BF16), woa_ref[...])
    gg = gg_ref[0]
    heads = []
    for h in range(GLA_HEADS):
        og = of_ref[0, h] + ob_ref[0, h]
        og = og * lax.rsqrt(jnp.mean(og * og, axis=-1, keepdims=True) + EPS) * ggla_ref[...]
        heads.append((og * gg[:, h * GLA_DV:(h + 1) * GLA_DV]).astype(BF16))
    yb = _dot(jnp.concatenate(heads, axis=-1), wog_ref[...])
    gc = gc_ref[0]
    yc_in = jnp.concatenate(
        [(yc_ref[0, h, 0] * gc[:, h * MLA_V_DIM:(h + 1) * MLA_V_DIM]).astype(BF16) for h in range(MLA_HEADS)], axis=-1)
    yc = _dot(yc_in, woc_ref[...])
    merged = m1_ref[0] * ya + m2_ref[0] * yb + m3_ref[0] * yc
    out = _dot(merged.astype(BF16), wout_ref[...])
    out = out * lax.rsqrt(jnp.mean(out * out, axis=-1, keepdims=True) + EPS) * gpost_ref[...]
    o_ref[0] = x_ref[0] + mod_ref[0][:, 2 * D:] * out


def _out_proj(x, mod, ya, ga, o_f, o_b, gg, yc, gc, m1, m2, m3, lw, *, tm):
    B, N, D = x.shape
    per_batch_mod = mod.shape[0] > 1
    row3 = lambda w: pl.BlockSpec((1, tm, w), lambda b, i: (b, i, 0))
    consts = [lw["w_o_gqa"], lw["w_o_gla"], lw["w_o_mla"], lw["w_out"], lw["g_gla_out"], lw["g_post"]]
    in_specs = [row3(D),
                pl.BlockSpec((1, 1, 3 * D), (lambda b, i: (b, 0, 0)) if per_batch_mod else (lambda b, i: (0, 0, 0))),
                row3(GQA_WIDTH), row3(GQA_WIDTH),
                pl.BlockSpec((1, GLA_HEADS, tm, GLA_DV), lambda b, i: (b, 0, i, 0)),
                pl.BlockSpec((1, GLA_HEADS, tm, GLA_DV), lambda b, i: (b, 0, i, 0)),
                row3(GLA_WIDTH),
                pl.BlockSpec((1, MLA_HEADS, 1, tm, MLA_V_DIM), lambda b, i: (b, 0, 0, i, 0)),
                row3(MLA_WIDTH), row3(D), row3(D), row3(D)]
    in_specs += [_const_spec(c.shape) for c in consts]
    return pl.pallas_call(
        functools.partial(_out_proj_kernel, d_model=D),
        grid=(B, N // tm),
        in_specs=in_specs,
        out_specs=row3(D),
        out_shape=jax.ShapeDtypeStruct((B, N, D), F32),
        compiler_params=pltpu.CompilerParams(dimension_semantics=("parallel", "parallel"),
                                             vmem_limit_bytes=VMEM_LIMIT),
        name="out_proj",
    )(x, mod, ya, ga, o_f, o_b, gg, yc, gc, m1, m2, m3, *consts)


def _rope_tables(n_tokens):
    t = np.arange(n_tokens)
    pos = np.stack([t // GRID_W, t % GRID_W], axis=0).astype(np.float64)

    def tables(lane_part, lane_in_part, half, active):
        freqs = ROPE_THETA ** (-(lane_in_part % half).astype(np.float64) / half)
        ang = pos[lane_part].T * freqs[None, :]
        cos = np.where(active[None, :], np.cos(ang), 1.0)
        sin = np.where(active[None, :], np.where(lane_in_part < half, -1.0, 1.0)[None, :] * np.sin(ang), 0.0)
        return jnp.asarray(cos, F32), jnp.asarray(sin, F32)

    lane = np.arange(LANES)
    ja = lane % GQA_HEAD_DIM
    ca, sa = tables(ja // (GQA_HEAD_DIM // 2), ja % (GQA_HEAD_DIM // 2), GQA_HEAD_DIM // 4, np.ones(LANES, bool))
    active = (lane >= MLA_ROPE_LANE) & (lane < MLA_ROPE_LANE + MLA_ROPE_DIM)
    jc = np.where(active, lane - MLA_ROPE_LANE, 0)
    cc, sc = tables(jc // (MLA_ROPE_DIM // 2), jc % (MLA_ROPE_DIM // 2), MLA_ROPE_DIM // 4, active)
    return ca, sa, cc, sc


def _pack_layer(l, p):
    D = p["w_in"].shape[1]
    w = p["w_in"][l]
    o_r = 2816
    o_lat = o_r + 2 * GLA_RANK
    o_kr = o_lat + MLA_Q_LORA + MLA_KV_LORA
    o_gc = o_kr + MLA_ROPE_DIM
    z = lambda n: jnp.zeros((D, n), F32)
    kr = w[:, o_kr:o_gc]
    w_packed = jnp.concatenate(
        [w[:, :o_lat], z(LANES - 2 * GLA_RANK), w[:, o_lat:o_kr],
         kr, z(MLA_ROPE_LANE - MLA_ROPE_DIM), kr, z(LANES - MLA_ROPE_LANE - MLA_ROPE_DIM), w[:, o_gc:]],
        axis=1).astype(BF16)
    assert w_packed.shape[1] == _N_PACKED
    w_dec = jnp.zeros((LANES, 2 * GLA_K_WIDTH), F32)
    w_dec = w_dec.at[:GLA_RANK, :GLA_K_WIDTH].set(p["w_gla_decay_fwd"][l])
    w_dec = w_dec.at[GLA_RANK:2 * GLA_RANK, GLA_K_WIDTH:].set(p["w_gla_decay_bwd"][l])
    b_dec = jnp.concatenate([p["b_gla_decay_fwd"][l], p["b_gla_decay_bwd"][l]])[None, :]
    pad_q = MLA_QK_PAD - MLA_NOPE_DIM - MLA_ROPE_DIM
    w_uq = p["w_mla_uq"][l].reshape(MLA_Q_LORA, MLA_HEADS, MLA_NOPE_DIM + MLA_ROPE_DIM)
    w_uq = jnp.pad(w_uq, ((0, 0), (0, 0), (0, pad_q))).reshape(MLA_Q_LORA, MLA_HEADS * MLA_QK_PAD)
    w_ukv = p["w_mla_ukv"][l].reshape(MLA_KV_LORA, MLA_HEADS, MLA_NOPE_DIM + MLA_V_DIM)
    w_uk = jnp.pad(w_ukv[:, :, :MLA_NOPE_DIM], ((0, 0), (0, 0), (0, MLA_QK_PAD - MLA_NOPE_DIM)))
    w_uk = w_uk.reshape(MLA_KV_LORA, MLA_HEADS * MLA_QK_PAD)
    w_uv = w_ukv[:, :, MLA_NOPE_DIM:].reshape(MLA_KV_LORA, MLA_WIDTH)
    seg = np.arange(GQA_WIDTH) // GQA_HEAD_DIM
    return {
        "g_pre": p["g_pre"][l][None, :], "g_post": p["g_post"][l][None, :],
        "w_in": w_packed,
        "g_q": jnp.tile(p["g_q_norm"][l], GQA_HEADS)[None, :],
        "g_k": jnp.tile(p["g_k_norm"][l], GQA_KV_HEADS)[None, :],
        "ones": jnp.asarray(seg[:, None] == seg[None, :], BF16),
        "w_dec": w_dec.astype(BF16), "b_dec": b_dec,
        "g_mla_q": p["g_mla_q"][l][None, :], "g_mla_kv": p["g_mla_kv"][l][None, :],
        "w_uq": w_uq.astype(BF16), "w_uk": w_uk.astype(BF16), "w_uv": w_uv.astype(BF16),
        "w_o_gqa": p["w_o_gqa"][l].astype(BF16), "w_o_gla": p["w_o_gla"][l].astype(BF16),
        "w_o_mla": p["w_o_mla"][l].astype(BF16), "w_out": p["w_out"][l].astype(BF16),
        "g_gla_out": p["g_gla_out"][l][None, :],
    }


def _heads_t(k, n_heads):
    B, N, W = k.shape
    return k.reshape(B, N, n_heads, W // n_heads).transpose(0, 2, 3, 1)


def _heads(v, n_heads):
    B, N, W = v.shape
    return v.reshape(B, N, n_heads, W // n_heads).transpose(0, 2, 1, 3)


def _sub_layer(x, mod, lw, rope, cache, *, tm, tq_a, tq_c, tk, tb):
    is_ctx = cache is None
    B, N, D = x.shape
    outs = _in_proj(x, mod, lw, rope, is_ctx=is_ctx, tm=tm)
    (qa, ka, va, ga, qg, kg, vg, gg, laf, lab, qc, kc, vc, gc, m1, m2, m3) = outs[:17]
    if is_ctx:
        s0f = s0b = jnp.zeros((B, GLA_HEADS, GLA_DV, GLA_DK), F32)
    else:
        kc_p, vc_p = _kv_up(cache["mla_ckv"], cache["mla_kr_padded"], lw["w_uk"], lw["w_uv"])
        ka = jnp.concatenate([cache["gqa_k"], ka], axis=1)
        va = jnp.concatenate([cache["gqa_v"], va], axis=1)
        kc = jnp.concatenate([kc_p, kc], axis=1)
        vc = jnp.concatenate([vc_p, vc], axis=1)
        s0f, s0b = cache["gla_fwd_t"], cache["gla_bwd_t"]
    ya = _attention(qa, _heads_t(ka, GQA_KV_HEADS), _heads(va, GQA_KV_HEADS), tq=tq_a, tk=tk)
    ya = ya.reshape(B, GQA_HEADS, N, GQA_HEAD_DIM).transpose(0, 2, 1, 3).reshape(B, N, GQA_WIDTH)
    yc = _attention(qc[:, :, None], _heads_t(kc, MLA_HEADS), _heads(vc, MLA_HEADS), tq=tq_c, tk=tk)
    o_f, o_b, s_f, s_b = _gla(qg, kg, vg, laf, lab, s0f, s0b, tb=tb)
    y = _out_proj(x, mod, ya, ga, o_f, o_b, gg, yc, gc, m1, m2, m3, lw, tm=tm)
    ctx = (outs[17], outs[18], outs[19], outs[20], s_f, s_b) if is_ctx else None
    return y, ctx


def kernel(x_prompt, x_sample, cache_gqa_k, cache_gqa_v, cache_mla_ckv, cache_mla_krope, state_gla_fwd, state_gla_bwd, c, c_ctx, w_mod, b_mod, g_pre, g_post, w_in, g_q_norm, g_k_norm, w_gla_decay_fwd, b_gla_decay_fwd, w_gla_decay_bwd, b_gla_decay_bwd, g_gla_out, g_mla_q, g_mla_kv, w_mla_uq, w_mla_ukv, w_o_gqa, w_o_gla, w_o_mla, w_out):
    params = dict(g_pre=g_pre, g_post=g_post, w_in=w_in, g_q_norm=g_q_norm, g_k_norm=g_k_norm,
                  w_gla_decay_fwd=w_gla_decay_fwd, b_gla_decay_fwd=b_gla_decay_fwd,
                  w_gla_decay_bwd=w_gla_decay_bwd, b_gla_decay_bwd=b_gla_decay_bwd, g_gla_out=g_gla_out,
                  g_mla_q=g_mla_q, g_mla_kv=g_mla_kv, w_mla_uq=w_mla_uq, w_mla_ukv=w_mla_ukv,
                  w_o_gqa=w_o_gqa, w_o_gla=w_o_gla, w_o_mla=w_o_mla, w_out=w_out)
    depth, D = w_in.shape[0], w_in.shape[1]
    B, S = x_prompt.shape[:2]
    Bd, Nd = x_sample.shape[:2]
    P = cache_gqa_k.shape[2]

    conds = jnp.concatenate([c_ctx[None, :], c, jnp.zeros((8 - 1 - Bd, D), F32)], axis=0)
    mods = _modulation(conds, w_mod, b_mod)
    rope = _rope_tables(Nd)
    layers = [_pack_layer(l, params) for l in range(depth)]

    tm_c = min(S, 256)
    tm_d = 256
    xp = x_prompt
    ctx_out = []
    for l in range(depth):
        xp, ctx = _sub_layer(xp, mods[l, 0:1][:, None, :], layers[l], None, None,
                             tm=tm_c, tq_a=min(S, 128), tq_c=min(S, 512), tk=512, tb=min(S, 512))
        ctx_out.append(ctx)

    xs = x_sample
    kr_pad = jnp.pad(cache_mla_krope, ((0, 0), (0, 0), (0, 0), (MLA_ROPE_LANE, LANES - MLA_ROPE_LANE - MLA_ROPE_DIM)))
    for l in range(depth):
        cache = {
            "gqa_k": cache_gqa_k[:, l].reshape(Bd, P, GQA_KV_WIDTH).astype(BF16),
            "gqa_v": cache_gqa_v[:, l].reshape(Bd, P, GQA_KV_WIDTH).astype(BF16),
            "mla_ckv": cache_mla_ckv[:, l], "mla_kr_padded": kr_pad[:, l],
            "gla_fwd_t": state_gla_fwd[:, l].swapaxes(-1, -2), "gla_bwd_t": state_gla_bwd[:, l].swapaxes(-1, -2),
        }
        xs, _ = _sub_layer(xs, mods[l, 1:1 + Bd][:, None, :], layers[l], rope, cache,
                           tm=tm_d, tq_a=128, tq_c=512, tk=512, tb=512)

    stack = lambda j: jnp.stack([ctx_out[l][j] for l in range(depth)], axis=1)
    new_k = stack(0).reshape(B, depth, S, GQA_KV_HEADS, GQA_HEAD_DIM)
    new_v = stack(1).reshape(B, depth, S, GQA_KV_HEADS, GQA_HEAD_DIM)
    new_sf = stack(4).swapaxes(-1, -2)
    new_sb = stack(5).swapaxes(-1, -2)
    return (xp, xs, new_k, new_v, stack(2), stack(3), new_sf, new_sb)
```

```python
import functools

import numpy as np
import jax
import jax.numpy as jnp
from jax import lax
from jax.experimental import pallas as pl
from jax.experimental.pallas import tpu as pltpu

F32 = jnp.float32
BF16 = jnp.bfloat16

EPS = 1e-6
LOG2_E = 1.4426950408889634
ROPE_THETA = 10000.0
GRID_W = 64

GQA_HEADS, GQA_KV_HEADS, GQA_HEAD_DIM = 8, 2, 64
GQA_REP = GQA_HEADS // GQA_KV_HEADS
GQA_WIDTH = GQA_HEADS * GQA_HEAD_DIM
GQA_KV_WIDTH = GQA_KV_HEADS * GQA_HEAD_DIM
GLA_HEADS, GLA_DK, GLA_DV = 4, 64, 128
GLA_WIDTH = GLA_HEADS * GLA_DV
GLA_K_WIDTH = GLA_HEADS * GLA_DK
GLA_RANK = 16
GLA_NORMALIZER = 16.0
GLA_CHUNK = 64
MLA_HEADS, MLA_Q_LORA, MLA_KV_LORA = 4, 256, 256
MLA_NOPE_DIM, MLA_ROPE_DIM, MLA_V_DIM = 64, 32, 128
MLA_WIDTH = MLA_HEADS * MLA_V_DIM
MLA_QK_PAD = 128
MLA_ROPE_LANE = MLA_NOPE_DIM

LANES = 128
VMEM_LIMIT = 56 * 1024 * 1024


def _sigmoid(x):
    return 1.0 / (1.0 + jnp.exp(-x))


def _silu(x):
    return x * _sigmoid(x)


def _log_sigmoid(x):
    return jnp.minimum(x, 0.0) - jnp.log(1.0 + jnp.exp(-jnp.abs(x)))


def _dot(a, b):
    return jnp.dot(a, b, preferred_element_type=F32)


def _dot_nt(a, b):
    return lax.dot_general(a, b, (((1,), (1,)), ((), ())), preferred_element_type=F32)


def _dot_tn(a, b):
    return lax.dot_general(a, b, (((0,), (0,)), ((), ())), preferred_element_type=F32)


def _split3(x):
    hi = x.astype(BF16)
    r1 = x - hi.astype(F32)
    mid = r1.astype(BF16)
    lo = (r1 - mid.astype(F32)).astype(BF16)
    return hi, mid, lo


def _segment_mean_sq(x, ones_bf16, width):
    sq = x * x
    hi = sq.astype(BF16)
    lo = (sq - hi.astype(F32)).astype(BF16)
    return (_dot(hi, ones_bf16) + _dot(lo, ones_bf16)) * (1.0 / width)


def _swap_halves(x, half):
    n = x.shape[-1]
    lane = lax.broadcasted_iota(jnp.int32, x.shape, x.ndim - 1)
    first = (lane % (2 * half)) < half
    return jnp.where(first, pltpu.roll(x, n - half, x.ndim - 1), pltpu.roll(x, half, x.ndim - 1))


def _rope(x, cos, sin_signed, half):
    return x * cos + _swap_halves(x, half) * sin_signed


def _tile_lanes(t, reps):
    return t if reps == 1 else jnp.concatenate([t] * reps, axis=-1)


def _mod_kernel(c_ref, w_ref, b_ref, o_ref):
    c = c_ref[...]
    o_ref[0] = _dot(_silu(c).astype(BF16), w_ref[0].astype(BF16)) + b_ref[0]


def _modulation(conds, w_mod, b_mod):
    L, D, D3 = w_mod.shape
    nj = D3 // D
    return pl.pallas_call(
        _mod_kernel,
        grid=(L, nj),
        in_specs=[pl.BlockSpec((8, D), lambda l, j: (0, 0)),
                  pl.BlockSpec((1, D, D), lambda l, j: (l, 0, j)),
                  pl.BlockSpec((1, 1, D), lambda l, j: (l, 0, j))],
        out_specs=pl.BlockSpec((1, 8, D), lambda l, j: (l, 0, j)),
        out_shape=jax.ShapeDtypeStruct((L, 8, D3), F32),
        compiler_params=pltpu.CompilerParams(dimension_semantics=("parallel", "parallel")),
        name="modulation",
    )(conds, w_mod, b_mod.reshape(L, 1, D3))


_C_QA, _C_KV, _C_GA, _C_QKG, _C_VG, _C_GG, _C_R, _C_LAT, _C_KR, _C_GC, _C_M = (
    0, 512, 768, 1280, 1792, 2304, 2816, 2944, 3456, 3584, 4096)
_N_PACKED = 7168


def _in_proj_kernel(*refs, is_ctx, d_model):
    it = iter(refs)
    x_ref, mod_ref, gpre_ref, w_ref, gq_ref, gk_ref, ones_ref, wdec_ref, bdec_ref = (next(it) for _ in range(9))
    gmq_ref, gmkv_ref, wuq_ref, wk_ref, wv_ref = (next(it) for _ in range(5))
    if not is_ctx:
        ca_ref, sa_ref, cc_ref, sc_ref = (next(it) for _ in range(4))
    (qa_o, ka_o, va_o, ga_o, qg_o, kg_o, vg_o, gg_o, laf_o, lab_o,
     qc_o, kc_o, vc_o, gc_o, m1_o, m2_o, m3_o) = (next(it) for _ in range(17))
    if is_ctx:
        ka32_o, va32_o, ckv32_o, kr32_o = (next(it) for _ in range(4))

    D = d_model
    x = x_ref[0]
    mod = mod_ref[0]
    shift, scale = mod[:, :D], mod[:, D:2 * D]
    xn = x * lax.rsqrt(jnp.mean(x * x, axis=-1, keepdims=True) + EPS) * gpre_ref[...]
    hb = (xn * (1.0 + scale) + shift).astype(BF16)

    def proj(c0, width):
        return _dot(hb, w_ref[:, c0:c0 + width])

    ones = ones_ref[...]

    qa = proj(_C_QA, GQA_WIDTH)
    qa = qa * lax.rsqrt(_segment_mean_sq(qa, ones, GQA_HEAD_DIM) + EPS) * gq_ref[...]
    kv = proj(_C_KV, 2 * GQA_KV_WIDTH)
    ka, va = kv[:, :GQA_KV_WIDTH], kv[:, GQA_KV_WIDTH:]
    ka = ka * lax.rsqrt(_segment_mean_sq(ka, ones[:GQA_KV_WIDTH, :GQA_KV_WIDTH], GQA_HEAD_DIM) + EPS) * gk_ref[...]
    if is_ctx:
        ka32_o[0] = ka
        va32_o[0] = va
    else:
        ca, sa = ca_ref[...], sa_ref[...]
        qa = _rope(qa, _tile_lanes(ca, GQA_WIDTH // LANES), _tile_lanes(sa, GQA_WIDTH // LANES), GQA_HEAD_DIM // 4)
        ka = _rope(ka, ca, sa, GQA_HEAD_DIM // 4)
    qa_o[0] = (qa * (GQA_HEAD_DIM ** -0.5 * LOG2_E)).astype(BF16)
    ka_o[0] = ka.astype(BF16)
    va_o[0] = va.astype(BF16)
    ga_o[0] = _silu(proj(_C_GA, GQA_WIDTH))

    qkg = proj(_C_QKG, 2 * GLA_K_WIDTH)
    qg = qkg[:, :GLA_K_WIDTH] * GLA_DK ** -0.5
    kg = qkg[:, GLA_K_WIDTH:]
    vg = proj(_C_VG, GLA_WIDTH)
    r = proj(_C_R, LANES).astype(BF16)
    la = _log_sigmoid(_dot(r, wdec_ref[...]) + bdec_ref[...]) * (1.0 / GLA_NORMALIZER)
    for h in range(GLA_HEADS):
        kq = slice(h * GLA_DK, (h + 1) * GLA_DK)
        qg_o[0, h] = qg[:, kq]
        kg_o[0, h] = kg[:, kq]
        laf_o[0, h] = la[:, h * GLA_DK:(h + 1) * GLA_DK]
        lab_o[0, h] = la[:, GLA_K_WIDTH + h * GLA_DK:GLA_K_WIDTH + (h + 1) * GLA_DK]
        vg_o[0, h] = vg[:, h * GLA_DV:(h + 1) * GLA_DV]
    gg_o[0] = _silu(proj(_C_GG, GLA_WIDTH))

    lat = proj(_C_LAT, MLA_Q_LORA + MLA_KV_LORA)
    ql, kvl = lat[:, :MLA_Q_LORA], lat[:, MLA_Q_LORA:]
    ql = ql * lax.rsqrt(jnp.mean(ql * ql, axis=-1, keepdims=True) + EPS) * gmq_ref[...]
    ckv = kvl * lax.rsqrt(jnp.mean(kvl * kvl, axis=-1, keepdims=True) + EPS) * gmkv_ref[...]
    qc = _dot(ql.astype(BF16), wuq_ref[...])
    krb = proj(_C_KR, LANES)
    if is_ctx:
        ckv32_o[0] = ckv
        kr32_o[0] = krb[:, :MLA_ROPE_DIM]
    else:
        cc, sc = cc_ref[...], sc_ref[...]
        qc = _rope(qc, _tile_lanes(cc, MLA_HEADS), _tile_lanes(sc, MLA_HEADS), MLA_ROPE_DIM // 4)
        krb = _rope(krb, cc, sc, MLA_ROPE_DIM // 4)
    lane = lax.broadcasted_iota(jnp.int32, krb.shape, 1)
    kr_part = jnp.where((lane >= MLA_ROPE_LANE) & (lane < MLA_ROPE_LANE + MLA_ROPE_DIM), krb, 0.0)
    qc_o[0] = (qc * ((MLA_NOPE_DIM + MLA_ROPE_DIM) ** -0.5 * LOG2_E)).astype(BF16)
    ckvb = ckv.astype(BF16)
    kc = _dot(ckvb, wk_ref[...]) + _tile_lanes(kr_part, MLA_HEADS)
    vc = _dot(ckvb, wv_ref[...])
    kc_o[0] = kc.astype(BF16)
    vc_o[0] = vc.astype(BF16)
    gc_o[0] = _silu(proj(_C_GC, MLA_WIDTH))

    m1_o[0] = _sigmoid(proj(_C_M, D))
    m2_o[0] = _sigmoid(proj(_C_M + D, D))
    m3_o[0] = _sigmoid(proj(_C_M + 2 * D, D))


def _const_spec(shape):
    nd = len(shape)
    return pl.BlockSpec(shape, lambda b, i, _nd=nd: (0,) * _nd, pipeline_mode=pl.Buffered(1))


def _in_proj(x, mod, lw, rope, *, is_ctx, tm):
    B, N, D = x.shape
    per_batch_mod = mod.shape[0] > 1
    row3 = lambda w: pl.BlockSpec((1, tm, w), lambda b, i: (b, i, 0))
    head4 = lambda h, w: pl.BlockSpec((1, h, tm, w), lambda b, i: (b, 0, i, 0))
    consts = [lw["g_pre"], lw["w_in"], lw["g_q"], lw["g_k"], lw["ones"], lw["w_dec"], lw["b_dec"],
              lw["g_mla_q"], lw["g_mla_kv"], lw["w_uq"], lw["w_uk"], lw["w_uv"]]
    in_specs = [row3(D),
                pl.BlockSpec((1, 1, 3 * D), (lambda b, i: (b, 0, 0)) if per_batch_mod else (lambda b, i: (0, 0, 0)))]
    in_specs += [_const_spec(c.shape) for c in consts]
    args = [x, mod] + consts
    if not is_ctx:
        in_specs += [pl.BlockSpec((tm, LANES), lambda b, i: (i, 0))] * 4
        args += list(rope)
    sds = jax.ShapeDtypeStruct
    out_shape = [
        sds((B, N, GQA_WIDTH), BF16), sds((B, N, GQA_KV_WIDTH), BF16),
        sds((B, N, GQA_KV_WIDTH), BF16), sds((B, N, GQA_WIDTH), F32),
        sds((B, GLA_HEADS, N, GLA_DK), F32), sds((B, GLA_HEADS, N, GLA_DK), F32),
        sds((B, GLA_HEADS, N, GLA_DV), F32), sds((B, N, GLA_WIDTH), F32),
        sds((B, GLA_HEADS, N, GLA_DK), F32), sds((B, GLA_HEADS, N, GLA_DK), F32),
        sds((B, N, MLA_HEADS * MLA_QK_PAD), BF16), sds((B, N, MLA_HEADS * MLA_QK_PAD), BF16),
        sds((B, N, MLA_WIDTH), BF16), sds((B, N, MLA_WIDTH), F32),
        sds((B, N, D), F32), sds((B, N, D), F32), sds((B, N, D), F32)]
    out_specs = [
        row3(GQA_WIDTH), row3(GQA_KV_WIDTH), row3(GQA_KV_WIDTH), row3(GQA_WIDTH),
        head4(GLA_HEADS, GLA_DK), head4(GLA_HEADS, GLA_DK), head4(GLA_HEADS, GLA_DV), row3(GLA_WIDTH),
        head4(GLA_HEADS, GLA_DK), head4(GLA_HEADS, GLA_DK),
        row3(MLA_HEADS * MLA_QK_PAD), row3(MLA_HEADS * MLA_QK_PAD), row3(MLA_WIDTH), row3(MLA_WIDTH),
        row3(D), row3(D), row3(D)]
    if is_ctx:
        out_shape += [sds((B, N, GQA_KV_WIDTH), F32), sds((B, N, GQA_KV_WIDTH), F32),
                      sds((B, N, MLA_KV_LORA), F32), sds((B, N, MLA_ROPE_DIM), F32)]
        out_specs += [row3(GQA_KV_WIDTH), row3(GQA_KV_WIDTH), row3(MLA_KV_LORA), row3(MLA_ROPE_DIM)]
    return pl.pallas_call(
        functools.partial(_in_proj_kernel, is_ctx=is_ctx, d_model=D),
        grid=(B, N // tm),
        in_specs=in_specs, out_specs=out_specs, out_shape=out_shape,
        compiler_params=pltpu.CompilerParams(dimension_semantics=("parallel", "parallel"),
                                             vmem_limit_bytes=VMEM_LIMIT),
        name="in_proj_ctx" if is_ctx else "in_proj_lat",
    )(*args)


def _kv_up_kernel(ckv_ref, kr_ref, wk_ref, wv_ref, kc_o, vc_o):
    ckvb = ckv_ref[0].astype(BF16)
    kc_o[0] = (_dot(ckvb, wk_ref[...]) + _tile_lanes(kr_ref[0], MLA_HEADS)).astype(BF16)
    vc_o[0] = _dot(ckvb, wv_ref[...]).astype(BF16)


def _kv_up(ckv, kr_padded, w_uk, w_uv):
    B, P, R = ckv.shape
    W = MLA_HEADS * MLA_QK_PAD
    return pl.pallas_call(
        _kv_up_kernel,
        grid=(B,),
        in_specs=[pl.BlockSpec((1, P, R), lambda b: (b, 0, 0)),
                  pl.BlockSpec((1, P, LANES), lambda b: (b, 0, 0)),
                  pl.BlockSpec(w_uk.shape, lambda b: (0, 0)),
                  pl.BlockSpec(w_uv.shape, lambda b: (0, 0))],
        out_specs=[pl.BlockSpec((1, P, W), lambda b: (b, 0, 0)),
                   pl.BlockSpec((1, P, MLA_WIDTH), lambda b: (b, 0, 0))],
        out_shape=[jax.ShapeDtypeStruct((B, P, W), BF16), jax.ShapeDtypeStruct((B, P, MLA_WIDTH), BF16)],
        compiler_params=pltpu.CompilerParams(dimension_semantics=("parallel",)),
        name="kv_up_cache",
    )(ckv, kr_padded, w_uk, w_uv)


def _attention_kernel(qt_ref, k_ref, vt_ref, o_ref, s_scr, *, tk):
    W = qt_ref.shape[4]
    nk = k_ref.shape[2]
    dv = vt_ref.shape[2]
    qt = qt_ref[0, 0, 0]
    m_acc = jnp.full((8, W), -jnp.inf, F32)
    for c in range(nk // tk):
        s = _dot(k_ref[0, 0, c * tk:(c + 1) * tk, :], qt)
        s_scr[c * tk:(c + 1) * tk, :] = s
        m_acc = jnp.maximum(m_acc, jnp.max(s.reshape(tk // 8, 8, W), axis=0))
    m = jnp.max(m_acc, axis=0, keepdims=True)
    l_acc = jnp.zeros((8, W), F32)
    acc = jnp.zeros((dv, W), F32)
    for c in range(nk // tk):
        p = jnp.exp2(s_scr[c * tk:(c + 1) * tk, :] - m)
        l_acc = l_acc + jnp.sum(p.reshape(tk // 8, 8, W), axis=0)
        acc = acc + _dot(vt_ref[0, 0, :, c * tk:(c + 1) * tk], p.astype(BF16))
    l = jnp.sum(l_acc, axis=0, keepdims=True)
    o_ref[0, 0, 0] = acc / l


def _attention(q, k, v, *, n_groups, tq, tk):
    B, N, _ = q.shape
    Nk = k.shape[1]
    G = n_groups
    dq, dv = k.shape[2] // G, v.shape[2] // G
    R = q.shape[2] // (G * dq)
    nt, W = N // tq, R * tq
    tk = min(tk, Nk)
    qt = q.reshape(B, nt, tq, G, R, dq).transpose(0, 3, 1, 5, 4, 2).reshape(B, G, nt, dq, W)
    kh = k.reshape(B, Nk, G, dq).transpose(0, 2, 1, 3)
    vt = v.reshape(B, Nk, G, dv).transpose(0, 2, 3, 1)
    ot = pl.pallas_call(
        functools.partial(_attention_kernel, tk=tk),
        grid=(B, G, nt),
        in_specs=[pl.BlockSpec((1, 1, 1, dq, W), lambda b, g, i: (b, g, i, 0, 0)),
                  pl.BlockSpec((1, 1, Nk, dq), lambda b, g, i: (b, g, 0, 0)),
                  pl.BlockSpec((1, 1, dv, Nk), lambda b, g, i: (b, g, 0, 0))],
        out_specs=pl.BlockSpec((1, 1, 1, dv, W), lambda b, g, i: (b, g, i, 0, 0)),
        out_shape=jax.ShapeDtypeStruct((B, G, nt, dv, W), F32),
        scratch_shapes=[pltpu.VMEM((Nk, W), F32)],
        compiler_params=pltpu.CompilerParams(dimension_semantics=("parallel", "parallel", "parallel"),
                                             vmem_limit_bytes=VMEM_LIMIT),
        name="attention",
    )(qt, kh, vt)
    return ot.reshape(B, G, nt, dv, R, tq).transpose(0, 2, 5, 1, 4, 3).reshape(B, N, G * R * dv)


def _gla_kernel(qf_ref, kf_ref, vf_ref, gf_ref, qb_ref, kb_ref, vb_ref, gb_ref, s0f_ref, s0b_ref,
                of_ref, ob_ref, sf_ref, sb_ref, st_scr, *, nchunk):
    C = GLA_CHUNK
    i = pl.program_id(1)

    @pl.when(i == 0)
    def _():
        st_scr[0] = s0f_ref[0]
        st_scr[1] = s0b_ref[0]

    row = lax.broadcasted_iota(jnp.int32, (C, C), 0)
    col = lax.broadcasted_iota(jnp.int32, (C, C), 1)
    keep = (row >= col, row <= col)
    tri = tuple(jnp.where(kp, 1.0, 0.0).astype(BF16) for kp in keep)
    dirs = ((qf_ref, kf_ref, vf_ref, gf_ref, of_ref), (qb_ref, kb_ref, vb_ref, gb_ref, ob_ref))

    def body(c, carry):
        for d, (q_ref, k_ref, v_ref, g_ref, o_ref) in enumerate(dirs):
            cc = c if d == 0 else nchunk - 1 - c
            rows = pl.ds(pl.multiple_of(cc * C, C), C)
            for h in range(GLA_HEADS):
                q, k, v, g = q_ref[0, h, rows, :], k_ref[0, h, rows, :], v_ref[0, h, rows, :], g_ref[0, h, rows, :]
                cum = sum(_dot(tri[d], part) for part in _split3(g))
                last = cum[C - 1:C, :] if d == 0 else cum[0:1, :]
                qd = q * jnp.exp(cum)
                kd = (k * jnp.exp(last - cum)).astype(BF16)
                a = _dot_nt((qd * jnp.exp(-last)).astype(BF16), kd)
                a = jnp.where(keep[d], a, 0.0).astype(BF16)
                vb = v.astype(BF16)
                st = st_scr[d, h]
                o_ref[0, h, rows, :] = _dot_nt(qd.astype(BF16), st.astype(BF16)) + _dot(a, vb)
                st_scr[d, h] = st * jnp.exp(last) + _dot_tn(vb, kd)
        return carry

    lax.fori_loop(0, nchunk, body, 0)

    @pl.when(i == pl.num_programs(1) - 1)
    def _():
        sf_ref[0] = st_scr[0]
        sb_ref[0] = st_scr[1]


def _gla(qg, kg, vg, laf, lab, s0f_t, s0b_t, *, tb):
    B, H, N, dk = qg.shape
    dv = vg.shape[3]
    nblk = N // tb
    fwd = lambda w: pl.BlockSpec((1, H, tb, w), lambda b, i: (b, 0, i, 0))
    bwd = lambda w: pl.BlockSpec((1, H, tb, w), lambda b, i: (b, 0, nblk - 1 - i, 0))
    st = pl.BlockSpec((1, H, dv, dk), lambda b, i: (b, 0, 0, 0))
    sds = jax.ShapeDtypeStruct
    return pl.pallas_call(
        functools.partial(_gla_kernel, nchunk=tb // GLA_CHUNK),
        grid=(B, nblk),
        in_specs=[fwd(dk), fwd(dk), fwd(dv), fwd(dk), bwd(dk), bwd(dk), bwd(dv), bwd(dk), st, st],
        out_specs=[fwd(dv), bwd(dv), st, st],
        out_shape=[sds((B, H, N, dv), F32), sds((B, H, N, dv), F32),
                   sds((B, H, dv, dk), F32), sds((B, H, dv, dk), F32)],
        scratch_shapes=[pltpu.VMEM((2, H, dv, dk), F32)],
        compiler_params=pltpu.CompilerParams(dimension_semantics=("parallel", "arbitrary"),
                                             vmem_limit_bytes=VMEM_LIMIT),
        name="gla",
    )(qg, kg, vg, laf, qg, kg, vg, lab, s0f_t, s0b_t)


def _out_proj_kernel(x_ref, mod_ref, ya_ref, ga_ref, of_ref, ob_ref, gg_ref, yc_ref, gc_ref,
                     m1_ref, m2_ref, m3_ref, woa_ref, wog_ref, woc_ref, wout_ref, ggla_ref, gpost_ref,
                     o_ref, *, d_model):
    D = d_model
    ya = _dot((ya_ref[0] * ga_ref[0]).astype(BF16), woa_ref[...])
    gg = gg_ref[0]
    heads = []
    for h in range(GLA_HEADS):
        og = of_ref[0, h] + ob_ref[0, h]
        og = og * lax.rsqrt(jnp.mean(og * og, axis=-1, keepdims=True) + EPS) * ggla_ref[...]
        heads.append((og * gg[:, h * GLA_DV:(h + 1) * GLA_DV]).astype(BF16))
    yb = _dot(jnp.concatenate(heads, axis=-1), wog_ref[...])
    yc = _dot((yc_ref[0] * gc_ref[0]).astype(BF16), woc_ref[...])
    merged = m1_ref[0] * ya + m2_ref[0] * yb + m3_ref[0] * yc
    out = _dot(merged.astype(BF16), wout_ref[...])
    out = out * lax.rsqrt(jnp.mean(out * out, axis=-1, keepdims=True) + EPS) * gpost_ref[...]
    o_ref[0] = x_ref[0] + mod_ref[0][:, 2 * D:] * out


def _out_proj(x, mod, ya, ga, o_f, o_b, gg, yc, gc, m1, m2, m3, lw, *, tm):
    B, N, D = x.shape
    per_batch_mod = mod.shape[0] > 1
    row3 = lambda w: pl.BlockSpec((1, tm, w), lambda b, i: (b, i, 0))
    consts = [lw["w_o_gqa"], lw["w_o_gla"], lw["w_o_mla"], lw["w_out"], lw["g_gla_out"], lw["g_post"]]
    in_specs = [row3(D),
                pl.BlockSpec((1, 1, 3 * D), (lambda b, i: (b, 0, 0)) if per_batch_mod else (lambda b, i: (0, 0, 0))),
                row3(GQA_WIDTH), row3(GQA_WIDTH),
                pl.BlockSpec((1, GLA_HEADS, tm, GLA_DV), lambda b, i: (b, 0, i, 0)),
                pl.BlockSpec((1, GLA_HEADS, tm, GLA_DV), lambda b, i: (b, 0, i, 0)),
                row3(GLA_WIDTH), row3(MLA_WIDTH), row3(MLA_WIDTH), row3(D), row3(D), row3(D)]
    in_specs += [_const_spec(c.shape) for c in consts]
    return pl.pallas_call(
        functools.partial(_out_proj_kernel, d_model=D),
        grid=(B, N // tm),
        in_specs=in_specs,
        out_specs=row3(D),
        out_shape=jax.ShapeDtypeStruct((B, N, D), F32),
        compiler_params=pltpu.CompilerParams(dimension_semantics=("parallel", "parallel"),
                                             vmem_limit_bytes=VMEM_LIMIT),
        name="out_proj",
    )(x, mod, ya, ga, o_f, o_b, gg, yc, gc, m1, m2, m3, *consts)


def _rope_tables(n_tokens):
    t = np.arange(n_tokens)
    pos = np.stack([t // GRID_W, t % GRID_W], axis=0).astype(np.float64)

    def tables(lane_part, lane_in_part, half, active):
        freqs = ROPE_THETA ** (-(lane_in_part % half).astype(np.float64) / half)
        ang = pos[lane_part].T * freqs[None, :]
        cos = np.where(active[None, :], np.cos(ang), 1.0)
        sin = np.where(active[None, :], np.where(lane_in_part < half, -1.0, 1.0)[None, :] * np.sin(ang), 0.0)
        return jnp.asarray(cos, F32), jnp.asarray(sin, F32)

    lane = np.arange(LANES)
    ja = lane % GQA_HEAD_DIM
    ca, sa = tables(ja // (GQA_HEAD_DIM // 2), ja % (GQA_HEAD_DIM // 2), GQA_HEAD_DIM // 4, np.ones(LANES, bool))
    active = (lane >= MLA_ROPE_LANE) & (lane < MLA_ROPE_LANE + MLA_ROPE_DIM)
    jc = np.where(active, lane - MLA_ROPE_LANE, 0)
    cc, sc = tables(jc // (MLA_ROPE_DIM // 2), jc % (MLA_ROPE_DIM // 2), MLA_ROPE_DIM // 4, active)
    return ca, sa, cc, sc


def _pack_layer(l, p):
    D = p["w_in"].shape[1]
    w = p["w_in"][l]
    o_r = 2816
    o_lat = o_r + 2 * GLA_RANK
    o_kr = o_lat + MLA_Q_LORA + MLA_KV_LORA
    o_gc = o_kr + MLA_ROPE_DIM
    z = lambda n: jnp.zeros((D, n), F32)
    kr = w[:, o_kr:o_gc]
    w_packed = jnp.concatenate(
        [w[:, :o_lat], z(LANES - 2 * GLA_RANK), w[:, o_lat:o_kr],
         kr, z(MLA_ROPE_LANE - MLA_ROPE_DIM), kr, z(LANES - MLA_ROPE_LANE - MLA_ROPE_DIM), w[:, o_gc:]],
        axis=1).astype(BF16)
    assert w_packed.shape[1] == _N_PACKED
    w_dec = jnp.zeros((LANES, 2 * GLA_K_WIDTH), F32)
    w_dec = w_dec.at[:GLA_RANK, :GLA_K_WIDTH].set(p["w_gla_decay_fwd"][l])
    w_dec = w_dec.at[GLA_RANK:2 * GLA_RANK, GLA_K_WIDTH:].set(p["w_gla_decay_bwd"][l])
    b_dec = jnp.concatenate([p["b_gla_decay_fwd"][l], p["b_gla_decay_bwd"][l]])[None, :]
    pad_q = MLA_QK_PAD - MLA_NOPE_DIM - MLA_ROPE_DIM
    w_uq = p["w_mla_uq"][l].reshape(MLA_Q_LORA, MLA_HEADS, MLA_NOPE_DIM + MLA_ROPE_DIM)
    w_uq = jnp.pad(w_uq, ((0, 0), (0, 0), (0, pad_q))).reshape(MLA_Q_LORA, MLA_HEADS * MLA_QK_PAD)
    w_ukv = p["w_mla_ukv"][l].reshape(MLA_KV_LORA, MLA_HEADS, MLA_NOPE_DIM + MLA_V_DIM)
    w_uk = jnp.pad(w_ukv[:, :, :MLA_NOPE_DIM], ((0, 0), (0, 0), (0, MLA_QK_PAD - MLA_NOPE_DIM)))
    w_uk = w_uk.reshape(MLA_KV_LORA, MLA_HEADS * MLA_QK_PAD)
    w_uv = w_ukv[:, :, MLA_NOPE_DIM:].reshape(MLA_KV_LORA, MLA_WIDTH)
    seg = np.arange(GQA_WIDTH) // GQA_HEAD_DIM
    return {
        "g_pre": p["g_pre"][l][None, :], "g_post": p["g_post"][l][None, :],
        "w_in": w_packed,
        "g_q": jnp.tile(p["g_q_norm"][l], GQA_HEADS)[None, :],
        "g_k": jnp.tile(p["g_k_norm"][l], GQA_KV_HEADS)[None, :],
        "ones": jnp.asarray(seg[:, None] == seg[None, :], BF16),
        "w_dec": w_dec.astype(BF16), "b_dec": b_dec,
        "g_mla_q": p["g_mla_q"][l][None, :], "g_mla_kv": p["g_mla_kv"][l][None, :],
        "w_uq": w_uq.astype(BF16), "w_uk": w_uk.astype(BF16), "w_uv": w_uv.astype(BF16),
        "w_o_gqa": p["w_o_gqa"][l].astype(BF16), "w_o_gla": p["w_o_gla"][l].astype(BF16),
        "w_o_mla": p["w_o_mla"][l].astype(BF16), "w_out": p["w_out"][l].astype(BF16),
        "g_gla_out": p["g_gla_out"][l][None, :],
    }


def _sub_layer(x, mod, lw, rope, cache, *, tm, tq_a, tq_c, tk, tb):
    is_ctx = cache is None
    B, N, D = x.shape
    outs = _in_proj(x, mod, lw, rope, is_ctx=is_ctx, tm=tm)
    (qa, ka, va, ga, qg, kg, vg, gg, laf, lab, qc, kc, vc, gc, m1, m2, m3) = outs[:17]
    if is_ctx:
        s0f = s0b = jnp.zeros((B, GLA_HEADS, GLA_DV, GLA_DK), F32)
    else:
        kc_p, vc_p = _kv_up(cache["mla_ckv"], cache["mla_kr_padded"], lw["w_uk"], lw["w_uv"])
        ka = jnp.concatenate([cache["gqa_k"], ka], axis=1)
        va = jnp.concatenate([cache["gqa_v"], va], axis=1)
        kc = jnp.concatenate([kc_p, kc], axis=1)
        vc = jnp.concatenate([vc_p, vc], axis=1)
        s0f, s0b = cache["gla_fwd_t"], cache["gla_bwd_t"]
    ya = _attention(qa, ka, va, n_groups=GQA_KV_HEADS, tq=tq_a, tk=tk)
    yc = _attention(qc, kc, vc, n_groups=MLA_HEADS, tq=tq_c, tk=tk)
    o_f, o_b, s_f, s_b = _gla(qg, kg, vg, laf, lab, s0f, s0b, tb=tb)
    y = _out_proj(x, mod, ya, ga, o_f, o_b, gg, yc, gc, m1, m2, m3, lw, tm=tm)
    ctx = (outs[17], outs[18], outs[19], outs[20], s_f, s_b) if is_ctx else None
    return y, ctx


def kernel(x_prompt, x_sample, cache_gqa_k, cache_gqa_v, cache_mla_ckv, cache_mla_krope, state_gla_fwd, state_gla_bwd, c, c_ctx, w_mod, b_mod, g_pre, g_post, w_in, g_q_norm, g_k_norm, w_gla_decay_fwd, b_gla_decay_fwd, w_gla_decay_bwd, b_gla_decay_bwd, g_gla_out, g_mla_q, g_mla_kv, w_mla_uq, w_mla_ukv, w_o_gqa, w_o_gla, w_o_mla, w_out):
    params = dict(g_pre=g_pre, g_post=g_post, w_in=w_in, g_q_norm=g_q_norm, g_k_norm=g_k_norm,
                  w_gla_decay_fwd=w_gla_decay_fwd, b_gla_decay_fwd=b_gla_decay_fwd,
                  w_gla_decay_bwd=w_gla_decay_bwd, b_gla_decay_bwd=b_gla_decay_bwd, g_gla_out=g_gla_out,
                  g_mla_q=g_mla_q, g_mla_kv=g_mla_kv, w_mla_uq=w_mla_uq, w_mla_ukv=w_mla_ukv,
                  w_o_gqa=w_o_gqa, w_o_gla=w_o_gla, w_o_mla=w_o_mla, w_out=w_out)
    depth, D = w_in.shape[0], w_in.shape[1]
    B, S = x_prompt.shape[:2]
    Bd, Nd = x_sample.shape[:2]
    P = cache_gqa_k.shape[2]

    conds = jnp.concatenate([c_ctx[None, :], c, jnp.zeros((8 - 1 - Bd, D), F32)], axis=0)
    mods = _modulation(conds, w_mod, b_mod)
    rope = _rope_tables(Nd)
    layers = [_pack_layer(l, params) for l in range(depth)]

    tm_c = min(S, 256)
    tm_d = 256
    xp = x_prompt
    ctx_out = []
    for l in range(depth):
        xp, ctx = _sub_layer(xp, mods[l, 0:1][:, None, :], layers[l], None, None,
                             tm=tm_c, tq_a=min(S, 128), tq_c=min(S, 512), tk=512, tb=min(S, 512))
        ctx_out.append(ctx)

    xs = x_sample
    kr_pad = jnp.pad(cache_mla_krope, ((0, 0), (0, 0), (0, 0), (MLA_ROPE_LANE, LANES - MLA_ROPE_LANE - MLA_ROPE_DIM)))
    for l in range(depth):
        cache = {
            "gqa_k": cache_gqa_k[:, l].reshape(Bd, P, GQA_KV_WIDTH).astype(BF16),
            "gqa_v": cache_gqa_v[:, l].reshape(Bd, P, GQA_KV_WIDTH).astype(BF16),
            "mla_ckv": cache_mla_ckv[:, l], "mla_kr_padded": kr_pad[:, l],
            "gla_fwd_t": state_gla_fwd[:, l].swapaxes(-1, -2), "gla_bwd_t": state_gla_bwd[:, l].swapaxes(-1, -2),
        }
        xs, _ = _sub_layer(xs, mods[l, 1:1 + Bd][:, None, :], layers[l], rope, cache,
                           tm=tm_d, tq_a=128, tq_c=512, tk=512, tb=512)

    stack = lambda j: jnp.stack([ctx_out[l][j] for l in range(depth)], axis=1)
    new_k = stack(0).reshape(B, depth, S, GQA_KV_HEADS, GQA_HEAD_DIM)
    new_v = stack(1).reshape(B, depth, S, GQA_KV_HEADS, GQA_HEAD_DIM)
    new_sf = stack(4).swapaxes(-1, -2)
    new_sb = stack(5).swapaxes(-1, -2)
    return (xp, xs, new_k, new_v, stack(2), stack(3), new_sf, new_sb)
```

```python
import functools

import numpy as np
import jax
import jax.numpy as jnp
from jax import lax
from jax.experimental import pallas as pl
from jax.experimental.pallas import tpu as pltpu

F32 = jnp.float32
BF16 = jnp.bfloat16

EPS = 1e-6
LOG2_E = 1.4426950408889634
ROPE_THETA = 10000.0
GRID_W = 64

GQA_HEADS, GQA_KV_HEADS, GQA_HEAD_DIM = 8, 2, 64
GQA_REP = GQA_HEADS // GQA_KV_HEADS
GQA_WIDTH = GQA_HEADS * GQA_HEAD_DIM
GQA_KV_WIDTH = GQA_KV_HEADS * GQA_HEAD_DIM
GLA_HEADS, GLA_DK, GLA_DV = 4, 64, 128
GLA_WIDTH = GLA_HEADS * GLA_DV
GLA_K_WIDTH = GLA_HEADS * GLA_DK
GLA_RANK = 16
GLA_NORMALIZER = 16.0
GLA_CHUNK = 64
MLA_HEADS, MLA_Q_LORA, MLA_KV_LORA = 4, 256, 256
MLA_NOPE_DIM, MLA_ROPE_DIM, MLA_V_DIM = 64, 32, 128
MLA_WIDTH = MLA_HEADS * MLA_V_DIM
MLA_QK_PAD = 128
MLA_QK_WIDTH = MLA_HEADS * MLA_QK_PAD
MLA_ROPE_LANE = MLA_NOPE_DIM

LANES = 128
SUBLANES = 8
VMEM_LIMIT = 56 * 1024 * 1024

assert GQA_KV_WIDTH == LANES and 2 * GQA_HEAD_DIM == LANES and MLA_V_DIM == LANES


def _sigmoid(x):
    return 1.0 / (1.0 + jnp.exp(-x))


def _silu(x):
    return x * _sigmoid(x)


def _log_sigmoid(x):
    return jnp.minimum(x, 0.0) - jnp.log(1.0 + jnp.exp(-jnp.abs(x)))


def _dot(a, b):
    return jnp.dot(a, b, preferred_element_type=F32)


def _dot_nt(a, b):
    return lax.dot_general(a, b, (((1,), (1,)), ((), ())), preferred_element_type=F32)


def _dot_tn(a, b):
    return lax.dot_general(a, b, (((0,), (0,)), ((), ())), preferred_element_type=F32)


def _split3(x):
    hi = x.astype(BF16)
    r1 = x - hi.astype(F32)
    mid = r1.astype(BF16)
    lo = (r1 - mid.astype(F32)).astype(BF16)
    return hi, mid, lo


def _segment_mean_sq(x, ones_bf16, width):
    sq = x * x
    hi = sq.astype(BF16)
    lo = (sq - hi.astype(F32)).astype(BF16)
    return (_dot(hi, ones_bf16) + _dot(lo, ones_bf16)) * (1.0 / width)


def _swap_halves(x, half):
    n = x.shape[-1]
    lane = lax.broadcasted_iota(jnp.int32, x.shape, x.ndim - 1)
    first = (lane % (2 * half)) < half
    return jnp.where(first, pltpu.roll(x, n - half, x.ndim - 1), pltpu.roll(x, half, x.ndim - 1))


def _rope(x, cos, sin_signed, half):
    return x * cos + _swap_halves(x, half) * sin_signed


def _tile_lanes(t, reps):
    return t if reps == 1 else jnp.concatenate([t] * reps, axis=-1)


def _rms(x, gain):
    return x * lax.rsqrt(jnp.mean(x * x, axis=-1, keepdims=True) + EPS) * gain


def _mod_kernel(c_ref, w_ref, b_ref, o_ref):
    c = c_ref[...]
    o_ref[0] = _dot(_silu(c).astype(BF16), w_ref[0].astype(BF16)) + b_ref[0]


def _modulation(conds, w_mod, b_mod):
    L, D, D3 = w_mod.shape
    nj = D3 // D
    return pl.pallas_call(
        _mod_kernel,
        grid=(L, nj),
        in_specs=[pl.BlockSpec((SUBLANES, D), lambda l, j: (0, 0)),
                  pl.BlockSpec((1, D, D), lambda l, j: (l, 0, j)),
                  pl.BlockSpec((1, 1, D), lambda l, j: (l, 0, j))],
        out_specs=pl.BlockSpec((1, SUBLANES, D), lambda l, j: (l, 0, j)),
        out_shape=jax.ShapeDtypeStruct((L, SUBLANES, D3), F32),
        compiler_params=pltpu.CompilerParams(dimension_semantics=("parallel", "parallel")),
        name="modulation",
    )(conds, w_mod, b_mod.reshape(L, 1, D3))


_N_HEAD_COLS = 2816
_T_R, _T_LAT, _T_KR, _T_GC, _T_M = 0, 128, 640, 768, 1280
_N_TAIL_COLS = 4352
_C_QA, _C_KV, _C_GA, _C_QKG, _C_VG, _C_GG = 0, 512, 768, 1280, 1792, 2304


def _in_proj_kernel(*refs, is_ctx, d_model):
    it = iter(refs)
    x_ref, mod_ref, gpre_ref, wh_ref, wt_ref, gq_ref, gk_ref, ones_ref, wdec_ref, bdec_ref = (next(it) for _ in range(10))
    gmq_ref, gmkv_ref, wuq_ref, wk_ref, wv_ref = (next(it) for _ in range(5))
    if not is_ctx:
        ca_ref, sa_ref, cc_ref, sc_ref = (next(it) for _ in range(4))
    (qa_o, ka_o, vat_o, ga_o, qkg_o, vg_o, gg_o, la_o,
     qc_o, kc_o, vct_o, gc_o, m1_o, m2_o, m3_o) = (next(it) for _ in range(15))
    if is_ctx:
        ka32_o, va32_o, ckv32_o, kr32_o = (next(it) for _ in range(4))

    D = d_model
    mod = mod_ref[0]
    shift, scale = mod[:, :D], mod[:, D:2 * D]
    hb = (_rms(x_ref[0], gpre_ref[...]) * (1.0 + scale) + shift).astype(BF16)

    def head(c0, width):
        return _dot(hb, wh_ref[0, :, c0:c0 + width])

    def tail(c0, width):
        return _dot(hb, wt_ref[:, c0:c0 + width])

    ones = ones_ref[...]

    qa = head(_C_QA, GQA_WIDTH)
    qa = qa * lax.rsqrt(_segment_mean_sq(qa, ones, GQA_HEAD_DIM) + EPS) * gq_ref[...]
    kv = head(_C_KV, 2 * GQA_KV_WIDTH)
    ka, va = kv[:, :GQA_KV_WIDTH], kv[:, GQA_KV_WIDTH:]
    ka = ka * lax.rsqrt(_segment_mean_sq(ka, ones[:GQA_KV_WIDTH, :GQA_KV_WIDTH], GQA_HEAD_DIM) + EPS) * gk_ref[...]
    if is_ctx:
        ka32_o[0] = ka
        va32_o[0] = va
    else:
        ca, sa = ca_ref[...], sa_ref[...]
        qa = _rope(qa, _tile_lanes(ca, GQA_WIDTH // LANES), _tile_lanes(sa, GQA_WIDTH // LANES), GQA_HEAD_DIM // 4)
        ka = _rope(ka, ca, sa, GQA_HEAD_DIM // 4)
    qa_o[0] = (qa * (GQA_HEAD_DIM ** -0.5 * LOG2_E)).astype(BF16)
    ka_o[0] = ka.astype(BF16)
    vat_o[0] = va.T.astype(BF16)
    ga_o[0] = _silu(head(_C_GA, GQA_WIDTH)).astype(BF16)

    qkg = head(_C_QKG, 2 * GLA_K_WIDTH)
    lane = lax.broadcasted_iota(jnp.int32, qkg.shape, 1)
    qkg_o[0] = jnp.where(lane < GLA_K_WIDTH, qkg * GLA_DK ** -0.5, qkg)
    vg_o[0] = head(_C_VG, GLA_WIDTH)
    r = tail(_T_R, LANES).astype(BF16)
    la_o[0] = _log_sigmoid(_dot(r, wdec_ref[...]) + bdec_ref[...]) * (1.0 / GLA_NORMALIZER)
    gg_o[0] = _silu(head(_C_GG, GLA_WIDTH)).astype(BF16)

    lat = tail(_T_LAT, MLA_Q_LORA + MLA_KV_LORA)
    ql = _rms(lat[:, :MLA_Q_LORA], gmq_ref[...])
    ckv = _rms(lat[:, MLA_Q_LORA:], gmkv_ref[...])
    qc = _dot(ql.astype(BF16), wuq_ref[...])
    krb = tail(_T_KR, LANES)
    if is_ctx:
        ckv32_o[0] = ckv
        kr32_o[0] = krb[:, :MLA_ROPE_DIM]
    else:
        cc, sc = cc_ref[...], sc_ref[...]
        qc = _rope(qc, _tile_lanes(cc, MLA_HEADS), _tile_lanes(sc, MLA_HEADS), MLA_ROPE_DIM // 4)
        krb = _rope(krb, cc, sc, MLA_ROPE_DIM // 4)
    lane = lax.broadcasted_iota(jnp.int32, krb.shape, 1)
    kr_part = jnp.where((lane >= MLA_ROPE_LANE) & (lane < MLA_ROPE_LANE + MLA_ROPE_DIM), krb, 0.0)
    qc_o[0] = (qc * ((MLA_NOPE_DIM + MLA_ROPE_DIM) ** -0.5 * LOG2_E)).astype(BF16)
    ckvb = ckv.astype(BF16)
    kc_o[0] = (_dot(ckvb, wk_ref[...]) + _tile_lanes(kr_part, MLA_HEADS)).astype(BF16)
    vct_o[0] = _dot(ckvb, wv_ref[...]).T.astype(BF16)
    gc_o[0] = _silu(tail(_T_GC, MLA_WIDTH)).astype(BF16)

    m1_o[0] = _sigmoid(tail(_T_M, D)).astype(BF16)
    m2_o[0] = _sigmoid(tail(_T_M + D, D)).astype(BF16)
    m3_o[0] = _sigmoid(tail(_T_M + 2 * D, D)).astype(BF16)


def _const_spec(shape):
    nd = len(shape)
    return pl.BlockSpec(shape, lambda b, i, _nd=nd: (0,) * _nd, pipeline_mode=pl.Buffered(1))


def _in_proj(x, mod, w_in_bf16, layer, lw, rope, *, is_ctx, tm):
    B, N, D = x.shape
    per_batch_mod = mod.shape[0] > 1
    row3 = lambda w: pl.BlockSpec((1, tm, w), lambda b, i: (b, i, 0))
    col3 = lambda h: pl.BlockSpec((1, h, tm), lambda b, i: (b, 0, i))
    consts = [lw["g_q"], lw["g_k"], lw["ones"], lw["w_dec"], lw["b_dec"],
              lw["g_mla_q"], lw["g_mla_kv"], lw["w_uq"], lw["w_uk"], lw["w_uv"]]
    in_specs = [row3(D),
                pl.BlockSpec((1, 1, 3 * D), (lambda b, i: (b, 0, 0)) if per_batch_mod else (lambda b, i: (0, 0, 0))),
                _const_spec(lw["g_pre"].shape),
                pl.BlockSpec((1, D, _N_HEAD_COLS), lambda b, i: (layer, 0, 0), pipeline_mode=pl.Buffered(1)),
                _const_spec(lw["w_tail"].shape)]
    in_specs += [_const_spec(c.shape) for c in consts]
    args = [x, mod, lw["g_pre"], w_in_bf16, lw["w_tail"]] + consts
    if not is_ctx:
        in_specs += [pl.BlockSpec((tm, LANES), lambda b, i: (i, 0))] * 4
        args += list(rope)
    sds = jax.ShapeDtypeStruct
    out_shape = [
        sds((B, N, GQA_WIDTH), BF16), sds((B, N, GQA_KV_WIDTH), BF16), sds((B, GQA_KV_WIDTH, N), BF16),
        sds((B, N, GQA_WIDTH), BF16),
        sds((B, N, 2 * GLA_K_WIDTH), F32), sds((B, N, GLA_WIDTH), F32), sds((B, N, GLA_WIDTH), BF16),
        sds((B, N, 2 * GLA_K_WIDTH), F32),
        sds((B, N, MLA_QK_WIDTH), BF16), sds((B, N, MLA_QK_WIDTH), BF16), sds((B, MLA_WIDTH, N), BF16),
        sds((B, N, MLA_WIDTH), BF16),
        sds((B, N, D), BF16), sds((B, N, D), BF16), sds((B, N, D), BF16)]
    out_specs = [
        row3(GQA_WIDTH), row3(GQA_KV_WIDTH), col3(GQA_KV_WIDTH), row3(GQA_WIDTH),
        row3(2 * GLA_K_WIDTH), row3(GLA_WIDTH), row3(GLA_WIDTH), row3(2 * GLA_K_WIDTH),
        row3(MLA_QK_WIDTH), row3(MLA_QK_WIDTH), col3(MLA_WIDTH), row3(MLA_WIDTH),
        row3(D), row3(D), row3(D)]
    if is_ctx:
        out_shape += [sds((B, N, GQA_KV_WIDTH), F32), sds((B, N, GQA_KV_WIDTH), F32),
                      sds((B, N, MLA_KV_LORA), F32), sds((B, N, MLA_ROPE_DIM), F32)]
        out_specs += [row3(GQA_KV_WIDTH), row3(GQA_KV_WIDTH), row3(MLA_KV_LORA), row3(MLA_ROPE_DIM)]
    return pl.pallas_call(
        functools.partial(_in_proj_kernel, is_ctx=is_ctx, d_model=D),
        grid=(B, N // tm),
        in_specs=in_specs, out_specs=out_specs, out_shape=out_shape,
        compiler_params=pltpu.CompilerParams(dimension_semantics=("parallel", "parallel"),
                                             vmem_limit_bytes=VMEM_LIMIT),
        name="in_proj_ctx" if is_ctx else "in_proj_lat",
    )(*args)


def _cache_prep_kernel(gk_ref, gv_ref, ckv_ref, kr_ref, wk_ref, wv_ref, place_ref, ka_o, vat_o, kc_o, vct_o):
    ka_o[0] = gk_ref[0, 0].astype(BF16)
    vat_o[0] = gv_ref[0, 0].T.astype(BF16)
    ckvb = ckv_ref[0, 0].astype(BF16)
    kc_o[0] = (_dot(ckvb, wk_ref[...]) + _dot(kr_ref[0, 0].astype(BF16), place_ref[...])).astype(BF16)
    vct_o[0] = _dot(ckvb, wv_ref[...]).T.astype(BF16)


def _cache_prep(cache_gqa_k, cache_gqa_v, cache_mla_ckv, cache_mla_krope, layer, lw):
    B, L, P = cache_gqa_k.shape[:3]
    gk = cache_gqa_k.reshape(B, L, P, GQA_KV_WIDTH)
    gv = cache_gqa_v.reshape(B, L, P, GQA_KV_WIDTH)
    lsel = lambda w: pl.BlockSpec((1, 1, P, w), lambda b: (b, layer, 0, 0))
    full = lambda a: pl.BlockSpec(a.shape, lambda b: (0,) * a.ndim)
    sds = jax.ShapeDtypeStruct
    return pl.pallas_call(
        _cache_prep_kernel,
        grid=(B,),
        in_specs=[lsel(GQA_KV_WIDTH), lsel(GQA_KV_WIDTH), lsel(MLA_KV_LORA), lsel(MLA_ROPE_DIM),
                  full(lw["w_uk"]), full(lw["w_uv"]), full(lw["kr_place"])],
        out_specs=[pl.BlockSpec((1, P, GQA_KV_WIDTH), lambda b: (b, 0, 0)),
                   pl.BlockSpec((1, GQA_KV_WIDTH, P), lambda b: (b, 0, 0)),
                   pl.BlockSpec((1, P, MLA_QK_WIDTH), lambda b: (b, 0, 0)),
                   pl.BlockSpec((1, MLA_WIDTH, P), lambda b: (b, 0, 0))],
        out_shape=[sds((B, P, GQA_KV_WIDTH), BF16), sds((B, GQA_KV_WIDTH, P), BF16),
                   sds((B, P, MLA_QK_WIDTH), BF16), sds((B, MLA_WIDTH, P), BF16)],
        compiler_params=pltpu.CompilerParams(dimension_semantics=("parallel",)),
        name="cache_prep",
    )(gk, gv, cache_mla_ckv, cache_mla_krope, lw["w_uk"], lw["w_uv"], lw["kr_place"])


def _attention_kernel(*refs, n_src, tk, rep, shared_kv):
    q_ref, o_ref, s_scr = refs[0], refs[1 + 2 * n_src], refs[2 + 2 * n_src]
    srcs = [(refs[1 + 2 * j], refs[2 + 2 * j]) for j in range(n_src)]
    tq = q_ref.shape[1]
    g = pl.program_id(1)
    if shared_kv:
        hd = LANES // 2
        x = q_ref[0].astype(F32)
        lane = lax.broadcasted_iota(jnp.int32, (tq, LANES), 1)
        pieces = []
        for r in range(rep):
            win = x[:, (r // 2) * LANES:(r // 2 + 1) * LANES]
            win = jnp.where(g == r % 2, win, pltpu.roll(win, hd, 1))
            pieces.append(jnp.where(lane // hd == g, win, 0.0).astype(BF16))
        qcat = jnp.concatenate(pieces, axis=0)
    else:
        qcat = q_ref[0]
    W = qcat.shape[0]

    chunks = []
    off = 0
    for k_ref, vt_ref in srcs:
        n = k_ref.shape[1]
        step = min(tk, n)
        for c in range(n // step):
            chunks.append((k_ref, vt_ref, c * step, step, off))
            off += step

    m_acc = jnp.full((SUBLANES, W), -jnp.inf, F32)
    for k_ref, _, c0, step, off in chunks:
        s = _dot_nt(k_ref[0, c0:c0 + step, :], qcat)
        s_scr[off:off + step, :] = s
        m_acc = jnp.maximum(m_acc, jnp.max(s.reshape(step // SUBLANES, SUBLANES, W), axis=0))
    m = jnp.max(m_acc, axis=0, keepdims=True)
    l_acc = jnp.zeros((SUBLANES, W), F32)
    acc = jnp.zeros((LANES, W), F32)
    for _, vt_ref, c0, step, off in chunks:
        p = jnp.exp2(s_scr[off:off + step, :] - m)
        l_acc = l_acc + jnp.sum(p.reshape(step // SUBLANES, SUBLANES, W), axis=0)
        acc = acc + _dot(vt_ref[0, :, c0:c0 + step], p.astype(BF16))
    ot = acc / jnp.sum(l_acc, axis=0, keepdims=True)
    if shared_kv:
        ot = jnp.where(g == 0, ot[:LANES // 2], ot[LANES // 2:])
        for pair in range(rep // 2):
            two = jnp.concatenate([ot[:, (2 * pair) * tq:(2 * pair + 1) * tq],
                                   ot[:, (2 * pair + 1) * tq:(2 * pair + 2) * tq]], axis=0)
            o_ref[0, :, pair * LANES:(pair + 1) * LANES] = two.T
    else:
        o_ref[0] = ot.T


def _attention(q, sources, *, n_groups, shared_kv, tq, tk):
    B, N, Wq = q.shape
    G = n_groups
    qw = Wq // G
    rep = qw // (LANES // 2) if shared_kv else 1
    kv_blk = (lambda b, g, i: (b, 0, 0)) if shared_kv else (lambda b, g, i: (b, 0, g))
    vt_blk = (lambda b, g, i: (b, 0, 0)) if shared_kv else (lambda b, g, i: (b, g, 0))
    in_specs = [pl.BlockSpec((1, tq, qw), lambda b, g, i: (b, i, g))]
    args = [q]
    nk = 0
    for k, vt in sources:
        n = k.shape[1]
        nk += n
        in_specs += [pl.BlockSpec((1, n, LANES), kv_blk), pl.BlockSpec((1, LANES, n), vt_blk)]
        args += [k, vt]
    ow = qw if shared_kv else LANES
    return pl.pallas_call(
        functools.partial(_attention_kernel, n_src=len(sources), tk=tk, rep=rep, shared_kv=shared_kv),
        grid=(B, G, N // tq),
        in_specs=in_specs,
        out_specs=pl.BlockSpec((1, tq, ow), lambda b, g, i: (b, i, g)),
        out_shape=jax.ShapeDtypeStruct((B, N, G * ow), F32),
        scratch_shapes=[pltpu.VMEM((nk, rep * tq), F32)],
        compiler_params=pltpu.CompilerParams(dimension_semantics=("parallel", "parallel", "parallel"),
                                             vmem_limit_bytes=VMEM_LIMIT),
        name="attention_gqa" if shared_kv else "attention_mla",
    )(*args)


def _gla_kernel(*refs, nchunk, nb, has_state):
    it = iter(refs)
    qkf_ref, vf_ref, gf_ref, qkb_ref, vb_ref, gb_ref = (next(it) for _ in range(6))
    if has_state:
        s0f_ref, s0b_ref = next(it), next(it)
    of_ref, ob_ref, sf_ref, sb_ref, st_scr, bd_scr = (next(it) for _ in range(6))
    C, H, DK, DV, KW, VW = GLA_CHUNK, GLA_HEADS, GLA_DK, GLA_DV, GLA_K_WIDTH, GLA_WIDTH
    i = pl.program_id(1)

    @pl.when(i == 0)
    def _():
        bd_scr[...] = jnp.zeros(bd_scr.shape, BF16)
        for d, s0_ref in enumerate((s0f_ref, s0b_ref) if has_state else (None, None)):
            for b in range(nb):
                for h in range(H):
                    s0 = s0_ref[b, 0, h] if has_state else jnp.zeros((DK, DV), F32)
                    st_scr[d, b, h] = s0
                    bd_scr[d, b, h * DK:(h + 1) * DK, h * DV:(h + 1) * DV] = s0.astype(BF16)

    row = lax.broadcasted_iota(jnp.int32, (C, H * C), 0)
    col = lax.broadcasted_iota(jnp.int32, (C, H * C), 1) % C
    keep = (row >= col, row <= col)
    tok = lax.broadcasted_iota(jnp.int32, (C, KW), 0)
    k_head = lax.broadcasted_iota(jnp.int32, (C, KW), 1) // DK
    v_head = lax.broadcasted_iota(jnp.int32, (C, VW), 1) // DV
    dirs = ((qkf_ref, vf_ref, gf_ref, of_ref, 0), (qkb_ref, vb_ref, gb_ref, ob_ref, KW))

    def chain(d, q, k, v, g, st, bd):
        cum = g
        shift = 1
        while shift < C:
            if d == 0:
                cum = cum + jnp.where(tok >= shift, pltpu.roll(cum, shift, 0), 0.0)
            else:
                cum = cum + jnp.where(tok < C - shift, pltpu.roll(cum, C - shift, 0), 0.0)
            shift *= 2
        last = cum[C - 1:C, :] if d == 0 else cum[0:1, :]
        qd = q * jnp.exp(cum)
        kd = k * jnp.exp(last - cum)
        kd_stack = jnp.concatenate([jnp.where(k_head == h, kd, 0.0).astype(BF16) for h in range(H)], axis=0)
        a = _dot_nt((qd * jnp.exp(-last)).astype(BF16), kd_stack)
        a = jnp.where(keep[d], a, 0.0).astype(BF16)
        v_stack = jnp.concatenate([jnp.where(v_head == h, v, 0.0).astype(BF16) for h in range(H)], axis=0)
        o = _dot(qd.astype(BF16), bd) + _dot(a, v_stack)
        kdt = kd.T.astype(BF16)
        decay = jnp.exp(jnp.broadcast_to(last, (LANES, KW)).T)
        vb = v.astype(BF16)
        s_new = [st[h] * decay[h * DK:(h + 1) * DK, :] + _dot(kdt[h * DK:(h + 1) * DK, :], vb[:, h * DV:(h + 1) * DV])
                 for h in range(H)]
        return o, s_new

    def body(c, carry):
        work = []
        for d, (qk_ref, v_ref, g_ref, o_ref, g0) in enumerate(dirs):
            cc = c if d == 0 else nchunk - 1 - c
            rows = pl.ds(pl.multiple_of(cc * C, C), C)
            for b in range(nb):
                work.append((d, b, o_ref, rows, qk_ref[b, rows, :KW], qk_ref[b, rows, KW:], v_ref[b, rows, :],
                             g_ref[b, rows, g0:g0 + KW], st_scr[d, b], bd_scr[d, b]))
        done = [(d, b, o_ref, rows) + chain(d, q, k, v, g, st, bd) for d, b, o_ref, rows, q, k, v, g, st, bd in work]
        for d, b, o_ref, rows, o, s_new in done:
            o_ref[b, rows, :] = o
            for h in range(H):
                st_scr[d, b, h] = s_new[h]
                bd_scr[d, b, h * DK:(h + 1) * DK, h * DV:(h + 1) * DV] = s_new[h].astype(BF16)
        return carry

    lax.fori_loop(0, nchunk, body, 0)

    @pl.when(i == pl.num_programs(1) - 1)
    def _():
        for d, s_ref in enumerate((sf_ref, sb_ref)):
            for b in range(nb):
                s_ref[b] = st_scr[d, b]


def _gla(qkg, vg, la, state_fwd, state_bwd, layer, *, tb, nb):
    B, N, _ = qkg.shape
    H, DK, DV = GLA_HEADS, GLA_DK, GLA_DV
    nblk = N // tb
    fwd = lambda w: pl.BlockSpec((nb, tb, w), lambda b, i: (b, i, 0))
    bwd = lambda w: pl.BlockSpec((nb, tb, w), lambda b, i: (b, nblk - 1 - i, 0))
    in_specs = [fwd(2 * GLA_K_WIDTH), fwd(GLA_WIDTH), fwd(2 * GLA_K_WIDTH),
                bwd(2 * GLA_K_WIDTH), bwd(GLA_WIDTH), bwd(2 * GLA_K_WIDTH)]
    args = [qkg, vg, la, qkg, vg, la]
    has_state = state_fwd is not None
    if has_state:
        s0 = pl.BlockSpec((nb, 1, H, DK, DV), lambda b, i: (b, layer, 0, 0, 0))
        in_specs += [s0, s0]
        args += [state_fwd, state_bwd]
    st = pl.BlockSpec((nb, H, DK, DV), lambda b, i: (b, 0, 0, 0))
    sds = jax.ShapeDtypeStruct
    return pl.pallas_call(
        functools.partial(_gla_kernel, nchunk=tb // GLA_CHUNK, nb=nb, has_state=has_state),
        grid=(B // nb, nblk),
        in_specs=in_specs,
        out_specs=[fwd(GLA_WIDTH), bwd(GLA_WIDTH), st, st],
        out_shape=[sds((B, N, GLA_WIDTH), F32), sds((B, N, GLA_WIDTH), F32),
                   sds((B, H, DK, DV), F32), sds((B, H, DK, DV), F32)],
        scratch_shapes=[pltpu.VMEM((2, nb, H, DK, DV), F32), pltpu.VMEM((2, nb, GLA_K_WIDTH, GLA_WIDTH), BF16)],
        compiler_params=pltpu.CompilerParams(dimension_semantics=("parallel", "arbitrary"),
                                             vmem_limit_bytes=VMEM_LIMIT),
        name="gla",
    )(*args)


def _out_proj_kernel(x_ref, mod_ref, ya_ref, ga_ref, of_ref, ob_ref, gg_ref, yc_ref, gc_ref,
                     m1_ref, m2_ref, m3_ref, woa_ref, wog_ref, woc_ref, wout_ref, ggla_ref, gpost_ref,
                     o_ref, *, d_model):
    D = d_model
    ya = _dot((ya_ref[0] * ga_ref[0]).astype(BF16), woa_ref[...])
    og = of_ref[0] + ob_ref[0]
    gg = gg_ref[0]
    heads = []
    for h in range(GLA_HEADS):
        sl = slice(h * GLA_DV, (h + 1) * GLA_DV)
        heads.append((_rms(og[:, sl], ggla_ref[...]) * gg[:, sl]).astype(BF16))
    yb = _dot(jnp.concatenate(heads, axis=-1), wog_ref[...])
    yc = _dot((yc_ref[0] * gc_ref[0]).astype(BF16), woc_ref[...])
    merged = m1_ref[0] * ya + m2_ref[0] * yb + m3_ref[0] * yc
    out = _rms(_dot(merged.astype(BF16), wout_ref[...]), gpost_ref[...])
    o_ref[0] = x_ref[0] + mod_ref[0][:, 2 * D:] * out


def _out_proj(x, mod, ya, ga, o_f, o_b, gg, yc, gc, m1, m2, m3, lw, *, tm):
    B, N, D = x.shape
    per_batch_mod = mod.shape[0] > 1
    row3 = lambda w: pl.BlockSpec((1, tm, w), lambda b, i: (b, i, 0))
    consts = [lw["w_o_gqa"], lw["w_o_gla"], lw["w_o_mla"], lw["w_out"], lw["g_gla_out"], lw["g_post"]]
    in_specs = [row3(D),
                pl.BlockSpec((1, 1, 3 * D), (lambda b, i: (b, 0, 0)) if per_batch_mod else (lambda b, i: (0, 0, 0))),
                row3(GQA_WIDTH), row3(GQA_WIDTH), row3(GLA_WIDTH), row3(GLA_WIDTH), row3(GLA_WIDTH),
                row3(MLA_WIDTH), row3(MLA_WIDTH), row3(D), row3(D), row3(D)]
    in_specs += [_const_spec(c.shape) for c in consts]
    return pl.pallas_call(
        functools.partial(_out_proj_kernel, d_model=D),
        grid=(B, N // tm),
        in_specs=in_specs,
        out_specs=row3(D),
        out_shape=jax.ShapeDtypeStruct((B, N, D), F32),
        compiler_params=pltpu.CompilerParams(dimension_semantics=("parallel", "parallel"),
                                             vmem_limit_bytes=VMEM_LIMIT),
        name="out_proj",
    )(x, mod, ya, ga, o_f, o_b, gg, yc, gc, m1, m2, m3, *consts)


def _rope_tables(n_tokens):
    t = np.arange(n_tokens)
    pos = np.stack([t // GRID_W, t % GRID_W], axis=0).astype(np.float64)

    def tables(lane_part, lane_in_part, half, active):
        freqs = ROPE_THETA ** (-(lane_in_part % half).astype(np.float64) / half)
        ang = pos[lane_part].T * freqs[None, :]
        cos = np.where(active[None, :], np.cos(ang), 1.0)
        sin = np.where(active[None, :], np.where(lane_in_part < half, -1.0, 1.0)[None, :] * np.sin(ang), 0.0)
        return jnp.asarray(cos, F32), jnp.asarray(sin, F32)

    lane = np.arange(LANES)
    ja = lane % GQA_HEAD_DIM
    ca, sa = tables(ja // (GQA_HEAD_DIM // 2), ja % (GQA_HEAD_DIM // 2), GQA_HEAD_DIM // 4, np.ones(LANES, bool))
    active = (lane >= MLA_ROPE_LANE) & (lane < MLA_ROPE_LANE + MLA_ROPE_DIM)
    jc = np.where(active, lane - MLA_ROPE_LANE, 0)
    cc, sc = tables(jc // (MLA_ROPE_DIM // 2), jc % (MLA_ROPE_DIM // 2), MLA_ROPE_DIM // 4, active)
    return ca, sa, cc, sc


def _pack_layer(l, p, w_in_bf16):
    D = w_in_bf16.shape[1]
    w = w_in_bf16[l]
    o_r = _N_HEAD_COLS
    o_lat = o_r + 2 * GLA_RANK
    o_kr = o_lat + MLA_Q_LORA + MLA_KV_LORA
    o_gc = o_kr + MLA_ROPE_DIM
    z = lambda n: jnp.zeros((D, n), BF16)
    kr = w[:, o_kr:o_gc]
    w_tail = jnp.concatenate(
        [w[:, o_r:o_lat], z(LANES - 2 * GLA_RANK), w[:, o_lat:o_kr],
         kr, z(MLA_ROPE_LANE - MLA_ROPE_DIM), kr, z(LANES - MLA_ROPE_LANE - MLA_ROPE_DIM), w[:, o_gc:]], axis=1)
    assert w_tail.shape[1] == _N_TAIL_COLS
    w_dec = jnp.zeros((LANES, 2 * GLA_K_WIDTH), F32)
    w_dec = w_dec.at[:GLA_RANK, :GLA_K_WIDTH].set(p["w_gla_decay_fwd"][l])
    w_dec = w_dec.at[GLA_RANK:2 * GLA_RANK, GLA_K_WIDTH:].set(p["w_gla_decay_bwd"][l])
    b_dec = jnp.concatenate([p["b_gla_decay_fwd"][l], p["b_gla_decay_bwd"][l]])[None, :]
    pad_q = MLA_QK_PAD - MLA_NOPE_DIM - MLA_ROPE_DIM
    w_uq = p["w_mla_uq"][l].reshape(MLA_Q_LORA, MLA_HEADS, MLA_NOPE_DIM + MLA_ROPE_DIM)
    w_uq = jnp.pad(w_uq, ((0, 0), (0, 0), (0, pad_q))).reshape(MLA_Q_LORA, MLA_QK_WIDTH)
    w_ukv = p["w_mla_ukv"][l].reshape(MLA_KV_LORA, MLA_HEADS, MLA_NOPE_DIM + MLA_V_DIM)
    w_uk = jnp.pad(w_ukv[:, :, :MLA_NOPE_DIM], ((0, 0), (0, 0), (0, MLA_QK_PAD - MLA_NOPE_DIM)))
    w_uk = w_uk.reshape(MLA_KV_LORA, MLA_QK_WIDTH)
    w_uv = w_ukv[:, :, MLA_NOPE_DIM:].reshape(MLA_KV_LORA, MLA_WIDTH)
    seg = np.arange(GQA_WIDTH) // GQA_HEAD_DIM
    place = np.zeros((MLA_ROPE_DIM, MLA_QK_WIDTH), np.float32)
    for h in range(MLA_HEADS):
        place[np.arange(MLA_ROPE_DIM), h * MLA_QK_PAD + MLA_ROPE_LANE + np.arange(MLA_ROPE_DIM)] = 1.0
    return {
        "g_pre": p["g_pre"][l][None, :], "g_post": p["g_post"][l][None, :],
        "w_tail": w_tail,
        "g_q": jnp.tile(p["g_q_norm"][l], GQA_HEADS)[None, :],
        "g_k": jnp.tile(p["g_k_norm"][l], GQA_KV_HEADS)[None, :],
        "ones": jnp.asarray(seg[:, None] == seg[None, :], BF16),
        "w_dec": w_dec.astype(BF16), "b_dec": b_dec,
        "g_mla_q": p["g_mla_q"][l][None, :], "g_mla_kv": p["g_mla_kv"][l][None, :],
        "w_uq": w_uq.astype(BF16), "w_uk": w_uk.astype(BF16), "w_uv": w_uv.astype(BF16),
        "kr_place": jnp.asarray(place, BF16),
        "w_o_gqa": p["w_o_gqa"][l].astype(BF16), "w_o_gla": p["w_o_gla"][l].astype(BF16),
        "w_o_mla": p["w_o_mla"][l].astype(BF16), "w_out": p["w_out"][l].astype(BF16),
        "g_gla_out": p["g_gla_out"][l][None, :],
    }


def _sub_layer(x, mod, w_in_bf16, layer, lw, rope, cache, *, tm, tq_a, tq_c, tk, tb, nb):
    is_ctx = cache is None
    outs = _in_proj(x, mod, w_in_bf16, layer, lw, rope, is_ctx=is_ctx, tm=tm)
    (qa, ka, vat, ga, qkg, vg, gg, la, qc, kc, vct, gc, m1, m2, m3) = outs[:15]
    src_a, src_c = [(ka, vat)], [(kc, vct)]
    s0f = s0b = None
    if not is_ctx:
        ka_p, vat_p, kc_p, vct_p = _cache_prep(cache["gqa_k"], cache["gqa_v"], cache["mla_ckv"],
                                               cache["mla_krope"], layer, lw)
        src_a, src_c = [(ka_p, vat_p)] + src_a, [(kc_p, vct_p)] + src_c
        s0f, s0b = cache["gla_fwd"], cache["gla_bwd"]
    ya = _attention(qa, src_a, n_groups=GQA_KV_HEADS, shared_kv=True, tq=tq_a, tk=tk)
    yc = _attention(qc, src_c, n_groups=MLA_HEADS, shared_kv=False, tq=tq_c, tk=tk)
    o_f, o_b, s_f, s_b = _gla(qkg, vg, la, s0f, s0b, layer, tb=tb, nb=nb)
    y = _out_proj(x, mod, ya, ga, o_f, o_b, gg, yc, gc, m1, m2, m3, lw, tm=tm)
    ctx = (outs[15], outs[16], outs[17], outs[18], s_f, s_b) if is_ctx else None
    return y, ctx


def kernel(x_prompt, x_sample, cache_gqa_k, cache_gqa_v, cache_mla_ckv, cache_mla_krope, state_gla_fwd, state_gla_bwd, c, c_ctx, w_mod, b_mod, g_pre, g_post, w_in, g_q_norm, g_k_norm, w_gla_decay_fwd, b_gla_decay_fwd, w_gla_decay_bwd, b_gla_decay_bwd, g_gla_out, g_mla_q, g_mla_kv, w_mla_uq, w_mla_ukv, w_o_gqa, w_o_gla, w_o_mla, w_out):
    params = dict(g_pre=g_pre, g_post=g_post, g_q_norm=g_q_norm, g_k_norm=g_k_norm,
                  w_gla_decay_fwd=w_gla_decay_fwd, b_gla_decay_fwd=b_gla_decay_fwd,
                  w_gla_decay_bwd=w_gla_decay_bwd, b_gla_decay_bwd=b_gla_decay_bwd, g_gla_out=g_gla_out,
                  g_mla_q=g_mla_q, g_mla_kv=g_mla_kv, w_mla_uq=w_mla_uq, w_mla_ukv=w_mla_ukv,
                  w_o_gqa=w_o_gqa, w_o_gla=w_o_gla, w_o_mla=w_o_mla, w_out=w_out)
    depth, D = w_in.shape[0], w_in.shape[1]
    B, S = x_prompt.shape[:2]
    Bd, Nd = x_sample.shape[:2]

    conds = jnp.concatenate([c_ctx[None, :], c, jnp.zeros((SUBLANES - 1 - Bd, D), F32)], axis=0)
    mods = _modulation(conds, w_mod, b_mod)
    rope = _rope_tables(Nd)
    w_in_bf16 = w_in.astype(BF16)
    layers = [_pack_layer(l, params, w_in_bf16) for l in range(depth)]

    xp = x_prompt
    ctx_out = []
    for l in range(depth):
        xp, ctx = _sub_layer(xp, mods[l, 0:1][:, None, :], w_in_bf16, l, layers[l], None, None,
                             tm=min(S, 256), tq_a=min(S, 128), tq_c=min(S, 512), tk=512,
                             tb=min(S, 512), nb=2 if B % 2 == 0 else 1)
        ctx_out.append(ctx)

    xs = x_sample
    cache = {"gqa_k": cache_gqa_k, "gqa_v": cache_gqa_v, "mla_ckv": cache_mla_ckv, "mla_krope": cache_mla_krope,
             "gla_fwd": state_gla_fwd, "gla_bwd": state_gla_bwd}
    for l in range(depth):
        xs, _ = _sub_layer(xs, mods[l, 1:1 + Bd][:, None, :], w_in_bf16, l, layers[l], rope, cache,
                           tm=256, tq_a=128, tq_c=512, tk=512, tb=512, nb=2 if Bd % 2 == 0 else 1)

    stack = lambda j: jnp.stack([ctx_out[l][j] for l in range(depth)], axis=1)
    new_k = stack(0).reshape(B, depth, S, GQA_KV_HEADS, GQA_HEAD_DIM)
    new_v = stack(1).reshape(B, depth, S, GQA_KV_HEADS, GQA_HEAD_DIM)
    return (xp, xs, new_k, new_v, stack(2), stack(3), stack(4), stack(5))
```

```python
import functools

import numpy as np
import jax
import jax.numpy as jnp
from jax import lax
from jax.experimental import pallas as pl
from jax.experimental.pallas import tpu as pltpu

F32 = jnp.float32
BF16 = jnp.bfloat16

EPS = 1e-6
LOG2_E = 1.4426950408889634
ROPE_THETA = 10000.0
GRID_W = 64

GQA_HEADS, GQA_KV_HEADS, GQA_HEAD_DIM = 8, 2, 64
GQA_REP = GQA_HEADS // GQA_KV_HEADS
GQA_WIDTH = GQA_HEADS * GQA_HEAD_DIM
GQA_KV_WIDTH = GQA_KV_HEADS * GQA_HEAD_DIM
GLA_HEADS, GLA_DK, GLA_DV = 4, 64, 128
GLA_WIDTH = GLA_HEADS * GLA_DV
GLA_K_WIDTH = GLA_HEADS * GLA_DK
GLA_RANK = 16
GLA_NORMALIZER = 16.0
GLA_CHUNK = 64
MLA_HEADS, MLA_Q_LORA, MLA_KV_LORA = 4, 256, 256
MLA_NOPE_DIM, MLA_ROPE_DIM, MLA_V_DIM = 64, 32, 128
MLA_WIDTH = MLA_HEADS * MLA_V_DIM
MLA_QK_PAD = 128
MLA_QK_WIDTH = MLA_HEADS * MLA_QK_PAD
MLA_ROPE_LANE = MLA_NOPE_DIM

LANES = 128
SUBLANES = 8
VMEM_LIMIT = 56 * 1024 * 1024

assert GQA_KV_WIDTH == LANES and 2 * GQA_HEAD_DIM == LANES and MLA_V_DIM == LANES


def _sigmoid(x):
    return 1.0 / (1.0 + jnp.exp(-x))


def _silu(x):
    return x * _sigmoid(x)


def _log_sigmoid(x):
    return jnp.minimum(x, 0.0) - jnp.log(1.0 + jnp.exp(-jnp.abs(x)))


def _dot(a, b):
    return jnp.dot(a, b, preferred_element_type=F32)


def _dot_nt(a, b):
    return lax.dot_general(a, b, (((1,), (1,)), ((), ())), preferred_element_type=F32)


def _dot_tn(a, b):
    return lax.dot_general(a, b, (((0,), (0,)), ((), ())), preferred_element_type=F32)


def _split3(x):
    hi = x.astype(BF16)
    r1 = x - hi.astype(F32)
    mid = r1.astype(BF16)
    lo = (r1 - mid.astype(F32)).astype(BF16)
    return hi, mid, lo


def _segment_mean_sq(x, ones_bf16, width):
    sq = x * x
    hi = sq.astype(BF16)
    lo = (sq - hi.astype(F32)).astype(BF16)
    return (_dot(hi, ones_bf16) + _dot(lo, ones_bf16)) * (1.0 / width)


def _swap_halves(x, half):
    n = x.shape[-1]
    lane = lax.broadcasted_iota(jnp.int32, x.shape, x.ndim - 1)
    first = (lane % (2 * half)) < half
    return jnp.where(first, pltpu.roll(x, n - half, x.ndim - 1), pltpu.roll(x, half, x.ndim - 1))


def _rope(x, cos, sin_signed, half):
    return x * cos + _swap_halves(x, half) * sin_signed


def _tile_lanes(t, reps):
    return t if reps == 1 else jnp.concatenate([t] * reps, axis=-1)


def _rms(x, gain):
    return x * lax.rsqrt(jnp.mean(x * x, axis=-1, keepdims=True) + EPS) * gain


def _mod_kernel(c_ref, w_ref, b_ref, o_ref):
    c = c_ref[...]
    o_ref[0] = _dot(_silu(c).astype(BF16), w_ref[0].astype(BF16)) + b_ref[0]


def _modulation(conds, w_mod, b_mod):
    L, D, D3 = w_mod.shape
    nj = D3 // D
    return pl.pallas_call(
        _mod_kernel,
        grid=(L, nj),
        in_specs=[pl.BlockSpec((SUBLANES, D), lambda l, j: (0, 0)),
                  pl.BlockSpec((1, D, D), lambda l, j: (l, 0, j)),
                  pl.BlockSpec((1, 1, D), lambda l, j: (l, 0, j))],
        out_specs=pl.BlockSpec((1, SUBLANES, D), lambda l, j: (l, 0, j)),
        out_shape=jax.ShapeDtypeStruct((L, SUBLANES, D3), F32),
        compiler_params=pltpu.CompilerParams(dimension_semantics=("parallel", "parallel")),
        name="modulation",
    )(conds, w_mod, b_mod.reshape(L, 1, D3))


_N_HEAD_COLS = 2816
_T_R, _T_LAT, _T_KR, _T_GC, _T_M = 0, 128, 640, 768, 1280
_N_TAIL_COLS = 4352
_C_QA, _C_KV, _C_GA, _C_QKG, _C_VG, _C_GG = 0, 512, 768, 1280, 1792, 2304


def _in_proj_kernel(*refs, is_ctx, d_model):
    it = iter(refs)
    x_ref, mod_ref, gpre_ref, wh_ref, wt_ref, gq_ref, gk_ref, ones_ref, wdec_ref, bdec_ref = (next(it) for _ in range(10))
    gmq_ref, gmkv_ref, wuq_ref, wk_ref, wv_ref = (next(it) for _ in range(5))
    if not is_ctx:
        ca_ref, sa_ref, cc_ref, sc_ref = (next(it) for _ in range(4))
    (qa_o, ka_o, vat_o, ga_o, qkg_o, vg_o, gg_o, la_o,
     qc_o, kc_o, vct_o, gc_o, m1_o, m2_o, m3_o) = (next(it) for _ in range(15))
    if is_ctx:
        ka32_o, va32_o, ckv32_o, kr32_o = (next(it) for _ in range(4))

    D = d_model
    mod = mod_ref[0]
    shift, scale = mod[:, :D], mod[:, D:2 * D]
    hb = (_rms(x_ref[0], gpre_ref[...]) * (1.0 + scale) + shift).astype(BF16)

    def head(c0, width):
        return _dot(hb, wh_ref[0, :, c0:c0 + width])

    def tail(c0, width):
        return _dot(hb, wt_ref[:, c0:c0 + width])

    ones = ones_ref[...]

    qa = head(_C_QA, GQA_WIDTH)
    qa = qa * lax.rsqrt(_segment_mean_sq(qa, ones, GQA_HEAD_DIM) + EPS) * gq_ref[...]
    kv = head(_C_KV, 2 * GQA_KV_WIDTH)
    ka, va = kv[:, :GQA_KV_WIDTH], kv[:, GQA_KV_WIDTH:]
    ka = ka * lax.rsqrt(_segment_mean_sq(ka, ones[:GQA_KV_WIDTH, :GQA_KV_WIDTH], GQA_HEAD_DIM) + EPS) * gk_ref[...]
    if is_ctx:
        ka32_o[0] = ka
        va32_o[0] = va
    else:
        ca, sa = ca_ref[...], sa_ref[...]
        qa = _rope(qa, _tile_lanes(ca, GQA_WIDTH // LANES), _tile_lanes(sa, GQA_WIDTH // LANES), GQA_HEAD_DIM // 4)
        ka = _rope(ka, ca, sa, GQA_HEAD_DIM // 4)
    qa_o[0] = (qa * (GQA_HEAD_DIM ** -0.5 * LOG2_E)).astype(BF16)
    ka_o[0] = ka.astype(BF16)
    vat_o[0] = va.T.astype(BF16)
    ga_o[0] = _silu(head(_C_GA, GQA_WIDTH)).astype(BF16)

    qkg = head(_C_QKG, 2 * GLA_K_WIDTH)
    lane = lax.broadcasted_iota(jnp.int32, qkg.shape, 1)
    qkg_o[0] = jnp.where(lane < GLA_K_WIDTH, qkg * GLA_DK ** -0.5, qkg)
    vg_o[0] = head(_C_VG, GLA_WIDTH)
    r = tail(_T_R, LANES).astype(BF16)
    la_o[0] = _log_sigmoid(_dot(r, wdec_ref[...]) + bdec_ref[...]) * (1.0 / GLA_NORMALIZER)
    gg_o[0] = _silu(head(_C_GG, GLA_WIDTH)).astype(BF16)

    lat = tail(_T_LAT, MLA_Q_LORA + MLA_KV_LORA)
    ql = _rms(lat[:, :MLA_Q_LORA], gmq_ref[...])
    ckv = _rms(lat[:, MLA_Q_LORA:], gmkv_ref[...])
    qc = _dot(ql.astype(BF16), wuq_ref[...])
    krb = tail(_T_KR, LANES)
    if is_ctx:
        ckv32_o[0] = ckv
        kr32_o[0] = krb[:, :MLA_ROPE_DIM]
    else:
        cc, sc = cc_ref[...], sc_ref[...]
        qc = _rope(qc, _tile_lanes(cc, MLA_HEADS), _tile_lanes(sc, MLA_HEADS), MLA_ROPE_DIM // 4)
        krb = _rope(krb, cc, sc, MLA_ROPE_DIM // 4)
    lane = lax.broadcasted_iota(jnp.int32, krb.shape, 1)
    kr_part = jnp.where((lane >= MLA_ROPE_LANE) & (lane < MLA_ROPE_LANE + MLA_ROPE_DIM), krb, 0.0)
    qc_o[0] = (qc * ((MLA_NOPE_DIM + MLA_ROPE_DIM) ** -0.5 * LOG2_E)).astype(BF16)
    ckvb = ckv.astype(BF16)
    kc_o[0] = (_dot(ckvb, wk_ref[...]) + _tile_lanes(kr_part, MLA_HEADS)).astype(BF16)
    vct_o[0] = _dot(ckvb, wv_ref[...]).T.astype(BF16)
    gc_o[0] = _silu(tail(_T_GC, MLA_WIDTH)).astype(BF16)

    m1_o[0] = _sigmoid(tail(_T_M, D)).astype(BF16)
    m2_o[0] = _sigmoid(tail(_T_M + D, D)).astype(BF16)
    m3_o[0] = _sigmoid(tail(_T_M + 2 * D, D)).astype(BF16)


def _const_spec(shape):
    nd = len(shape)
    return pl.BlockSpec(shape, lambda b, i, _nd=nd: (0,) * _nd, pipeline_mode=pl.Buffered(1))


def _in_proj(x, mod, w_in_bf16, layer, lw, rope, *, is_ctx, tm):
    B, N, D = x.shape
    per_batch_mod = mod.shape[0] > 1
    row3 = lambda w: pl.BlockSpec((1, tm, w), lambda b, i: (b, i, 0))
    col3 = lambda h: pl.BlockSpec((1, h, tm), lambda b, i: (b, 0, i))
    consts = [lw["g_q"], lw["g_k"], lw["ones"], lw["w_dec"], lw["b_dec"],
              lw["g_mla_q"], lw["g_mla_kv"], lw["w_uq"], lw["w_uk"], lw["w_uv"]]
    in_specs = [row3(D),
                pl.BlockSpec((1, 1, 3 * D), (lambda b, i: (b, 0, 0)) if per_batch_mod else (lambda b, i: (0, 0, 0))),
                _const_spec(lw["g_pre"].shape),
                pl.BlockSpec((1, D, _N_HEAD_COLS), lambda b, i: (layer, 0, 0), pipeline_mode=pl.Buffered(1)),
                _const_spec(lw["w_tail"].shape)]
    in_specs += [_const_spec(c.shape) for c in consts]
    args = [x, mod, lw["g_pre"], w_in_bf16, lw["w_tail"]] + consts
    if not is_ctx:
        in_specs += [pl.BlockSpec((tm, LANES), lambda b, i: (i, 0))] * 4
        args += list(rope)
    sds = jax.ShapeDtypeStruct
    out_shape = [
        sds((B, N, GQA_WIDTH), BF16), sds((B, N, GQA_KV_WIDTH), BF16), sds((B, GQA_KV_WIDTH, N), BF16),
        sds((B, N, GQA_WIDTH), BF16),
        sds((B, N, 2 * GLA_K_WIDTH), F32), sds((B, N, GLA_WIDTH), F32), sds((B, N, GLA_WIDTH), BF16),
        sds((B, N, 2 * GLA_K_WIDTH), F32),
        sds((B, N, MLA_QK_WIDTH), BF16), sds((B, N, MLA_QK_WIDTH), BF16), sds((B, MLA_WIDTH, N), BF16),
        sds((B, N, MLA_WIDTH), BF16),
        sds((B, N, D), BF16), sds((B, N, D), BF16), sds((B, N, D), BF16)]
    out_specs = [
        row3(GQA_WIDTH), row3(GQA_KV_WIDTH), col3(GQA_KV_WIDTH), row3(GQA_WIDTH),
        row3(2 * GLA_K_WIDTH), row3(GLA_WIDTH), row3(GLA_WIDTH), row3(2 * GLA_K_WIDTH),
        row3(MLA_QK_WIDTH), row3(MLA_QK_WIDTH), col3(MLA_WIDTH), row3(MLA_WIDTH),
        row3(D), row3(D), row3(D)]
    if is_ctx:
        out_shape += [sds((B, N, GQA_KV_WIDTH), F32), sds((B, N, GQA_KV_WIDTH), F32),
                      sds((B, N, MLA_KV_LORA), F32), sds((B, N, MLA_ROPE_DIM), F32)]
        out_specs += [row3(GQA_KV_WIDTH), row3(GQA_KV_WIDTH), row3(MLA_KV_LORA), row3(MLA_ROPE_DIM)]
    return pl.pallas_call(
        functools.partial(_in_proj_kernel, is_ctx=is_ctx, d_model=D),
        grid=(B, N // tm),
        in_specs=in_specs, out_specs=out_specs, out_shape=out_shape,
        compiler_params=pltpu.CompilerParams(dimension_semantics=("parallel", "parallel"),
                                             vmem_limit_bytes=VMEM_LIMIT),
        name="in_proj_ctx" if is_ctx else "in_proj_lat",
    )(*args)


def _cache_prep_kernel(gk_ref, gv_ref, ckv_ref, kr_ref, wk_ref, wv_ref, place_ref, ka_o, vat_o, kc_o, vct_o):
    ka_o[0] = gk_ref[0, 0].astype(BF16)
    vat_o[0] = gv_ref[0, 0].T.astype(BF16)
    ckvb = ckv_ref[0, 0].astype(BF16)
    kc_o[0] = (_dot(ckvb, wk_ref[...]) + _dot(kr_ref[0, 0].astype(BF16), place_ref[...])).astype(BF16)
    vct_o[0] = _dot(ckvb, wv_ref[...]).T.astype(BF16)


def _cache_prep(cache_gqa_k, cache_gqa_v, cache_mla_ckv, cache_mla_krope, layer, lw):
    B, L, P = cache_gqa_k.shape[:3]
    gk = cache_gqa_k.reshape(B, L, P, GQA_KV_WIDTH)
    gv = cache_gqa_v.reshape(B, L, P, GQA_KV_WIDTH)
    lsel = lambda w: pl.BlockSpec((1, 1, P, w), lambda b: (b, layer, 0, 0))
    full = lambda a: pl.BlockSpec(a.shape, lambda b: (0,) * a.ndim)
    sds = jax.ShapeDtypeStruct
    return pl.pallas_call(
        _cache_prep_kernel,
        grid=(B,),
        in_specs=[lsel(GQA_KV_WIDTH), lsel(GQA_KV_WIDTH), lsel(MLA_KV_LORA), lsel(MLA_ROPE_DIM),
                  full(lw["w_uk"]), full(lw["w_uv"]), full(lw["kr_place"])],
        out_specs=[pl.BlockSpec((1, P, GQA_KV_WIDTH), lambda b: (b, 0, 0)),
                   pl.BlockSpec((1, GQA_KV_WIDTH, P), lambda b: (b, 0, 0)),
                   pl.BlockSpec((1, P, MLA_QK_WIDTH), lambda b: (b, 0, 0)),
                   pl.BlockSpec((1, MLA_WIDTH, P), lambda b: (b, 0, 0))],
        out_shape=[sds((B, P, GQA_KV_WIDTH), BF16), sds((B, GQA_KV_WIDTH, P), BF16),
                   sds((B, P, MLA_QK_WIDTH), BF16), sds((B, MLA_WIDTH, P), BF16)],
        compiler_params=pltpu.CompilerParams(dimension_semantics=("parallel",)),
        name="cache_prep",
    )(gk, gv, cache_mla_ckv, cache_mla_krope, lw["w_uk"], lw["w_uv"], lw["kr_place"])


def _attention_kernel(*refs, n_src, tk, nsub, rep, shared_kv):
    q_ref, o_ref, s_scr = refs[0], refs[1 + 2 * n_src], refs[2 + 2 * n_src]
    srcs = [(refs[1 + 2 * j], refs[2 + 2 * j]) for j in range(n_src)]
    tqs = q_ref.shape[1] // nsub
    g = pl.program_id(1)

    chunks = []
    off = 0
    for k_ref, vt_ref in srcs:
        n = k_ref.shape[1]
        step = min(tk, n)
        for c in range(n // step):
            chunks.append((k_ref, vt_ref, c * step, step, off))
            off += step

    def scores(j):
        rows = slice(j * tqs, (j + 1) * tqs)
        if shared_kv:
            hd = LANES // 2
            x = q_ref[0, rows, :].astype(F32)
            lane = lax.broadcasted_iota(jnp.int32, (tqs, LANES), 1)
            pieces = []
            for r in range(rep):
                win = x[:, (r // 2) * LANES:(r // 2 + 1) * LANES]
                win = jnp.where(g == r % 2, win, pltpu.roll(win, hd, 1))
                pieces.append(jnp.where(lane // hd == g, win, 0.0).astype(BF16))
            qcat = jnp.concatenate(pieces, axis=0)
        else:
            qcat = q_ref[0, rows, :]
        W = qcat.shape[0]
        m_acc = jnp.full((SUBLANES, W), -jnp.inf, F32)
        for k_ref, _, c0, step, off in chunks:
            s = _dot_nt(k_ref[0, c0:c0 + step, :], qcat)
            s_scr[j, off:off + step, :] = s
            m_acc = jnp.maximum(m_acc, jnp.max(s.reshape(step // SUBLANES, SUBLANES, W), axis=0))
        return jnp.max(m_acc, axis=0, keepdims=True)

    def outputs(j, m):
        W = m.shape[1]
        rows = slice(j * tqs, (j + 1) * tqs)
        l_acc = jnp.zeros((SUBLANES, W), F32)
        acc = jnp.zeros((LANES, W), F32)
        for _, vt_ref, c0, step, off in chunks:
            p = jnp.exp2(s_scr[j, off:off + step, :] - m)
            l_acc = l_acc + jnp.sum(p.reshape(step // SUBLANES, SUBLANES, W), axis=0)
            acc = acc + _dot(vt_ref[0, :, c0:c0 + step], p.astype(BF16))
        ot = acc / jnp.sum(l_acc, axis=0, keepdims=True)
        if shared_kv:
            ot = jnp.where(g == 0, ot[:LANES // 2], ot[LANES // 2:])
            for pair in range(rep // 2):
                two = jnp.concatenate([ot[:, (2 * pair) * tqs:(2 * pair + 1) * tqs],
                                       ot[:, (2 * pair + 1) * tqs:(2 * pair + 2) * tqs]], axis=0)
                o_ref[0, rows, pair * LANES:(pair + 1) * LANES] = two.T
        else:
            o_ref[0, rows, :] = ot.T

    m_prev = scores(0)
    for j in range(1, nsub):
        m_next = scores(j)
        outputs(j - 1, m_prev)
        m_prev = m_next
    outputs(nsub - 1, m_prev)


def _attention(q, sources, *, n_groups, shared_kv, tq, nsub, tk):
    B, N, Wq = q.shape
    G = n_groups
    tqs = tq // nsub
    tq = min(tq, N)
    nsub = tq // tqs
    qw = Wq // G
    rep = qw // (LANES // 2) if shared_kv else 1
    kv_blk = (lambda b, g, i: (b, 0, 0)) if shared_kv else (lambda b, g, i: (b, 0, g))
    vt_blk = (lambda b, g, i: (b, 0, 0)) if shared_kv else (lambda b, g, i: (b, g, 0))
    in_specs = [pl.BlockSpec((1, tq, qw), lambda b, g, i: (b, i, g))]
    args = [q]
    nk = 0
    for k, vt in sources:
        n = k.shape[1]
        nk += n
        in_specs += [pl.BlockSpec((1, n, LANES), kv_blk), pl.BlockSpec((1, LANES, n), vt_blk)]
        args += [k, vt]
    ow = qw if shared_kv else LANES
    return pl.pallas_call(
        functools.partial(_attention_kernel, n_src=len(sources), tk=tk, nsub=nsub, rep=rep, shared_kv=shared_kv),
        grid=(B, G, N // tq),
        in_specs=in_specs,
        out_specs=pl.BlockSpec((1, tq, ow), lambda b, g, i: (b, i, g)),
        out_shape=jax.ShapeDtypeStruct((B, N, G * ow), F32),
        scratch_shapes=[pltpu.VMEM((nsub, nk, rep * tq // nsub), F32)],
        compiler_params=pltpu.CompilerParams(dimension_semantics=("parallel", "parallel", "parallel"),
                                             vmem_limit_bytes=VMEM_LIMIT),
        name="attention_gqa" if shared_kv else "attention_mla",
    )(*args)


def _gla_kernel(*refs, nchunk, nb, has_state):
    it = iter(refs)
    qkf_ref, vf_ref, gf_ref, qkb_ref, vb_ref, gb_ref = (next(it) for _ in range(6))
    if has_state:
        s0f_ref, s0b_ref = next(it), next(it)
    of_ref, ob_ref, sf_ref, sb_ref, st_scr, bd_scr = (next(it) for _ in range(6))
    C, H, DK, DV, KW, VW = GLA_CHUNK, GLA_HEADS, GLA_DK, GLA_DV, GLA_K_WIDTH, GLA_WIDTH
    i = pl.program_id(1)

    @pl.when(i == 0)
    def _():
        bd_scr[...] = jnp.zeros(bd_scr.shape, BF16)
        for d, s0_ref in enumerate((s0f_ref, s0b_ref) if has_state else (None, None)):
            for b in range(nb):
                for h in range(H):
                    s0 = s0_ref[b, 0, h] if has_state else jnp.zeros((DK, DV), F32)
                    st_scr[d, b, h] = s0
                    bd_scr[d, b, h * DK:(h + 1) * DK, h * DV:(h + 1) * DV] = s0.astype(BF16)

    row = lax.broadcasted_iota(jnp.int32, (C, H * C), 0)
    col = lax.broadcasted_iota(jnp.int32, (C, H * C), 1) % C
    keep = (row >= col, row <= col)
    tok = lax.broadcasted_iota(jnp.int32, (C, KW), 0)
    k_head = lax.broadcasted_iota(jnp.int32, (C, KW), 1) // DK
    v_head = lax.broadcasted_iota(jnp.int32, (C, VW), 1) // DV
    dirs = ((qkf_ref, vf_ref, gf_ref, of_ref, 0), (qkb_ref, vb_ref, gb_ref, ob_ref, KW))

    def chain(d, q, k, v, g, st, bd):
        cum = g
        shift = 1
        while shift < C:
            if d == 0:
                cum = cum + jnp.where(tok >= shift, pltpu.roll(cum, shift, 0), 0.0)
            else:
                cum = cum + jnp.where(tok < C - shift, pltpu.roll(cum, C - shift, 0), 0.0)
            shift *= 2
        last = cum[C - 1:C, :] if d == 0 else cum[0:1, :]
        qd = q * jnp.exp(cum)
        kd = k * jnp.exp(last - cum)
        kd_stack = jnp.concatenate([jnp.where(k_head == h, kd, 0.0).astype(BF16) for h in range(H)], axis=0)
        a = _dot_nt((qd * jnp.exp(-last)).astype(BF16), kd_stack)
        a = jnp.where(keep[d], a, 0.0).astype(BF16)
        v_stack = jnp.concatenate([jnp.where(v_head == h, v, 0.0).astype(BF16) for h in range(H)], axis=0)
        o = _dot(qd.astype(BF16), bd) + _dot(a, v_stack)
        kdt = kd.T.astype(BF16)
        decay = jnp.exp(jnp.broadcast_to(last, (LANES, KW)).T)
        vb = v.astype(BF16)
        s_new = [st[h] * decay[h * DK:(h + 1) * DK, :] + _dot(kdt[h * DK:(h + 1) * DK, :], vb[:, h * DV:(h + 1) * DV])
                 for h in range(H)]
        return o, s_new

    def body(c, carry):
        work = []
        for d, (qk_ref, v_ref, g_ref, o_ref, g0) in enumerate(dirs):
            cc = c if d == 0 else nchunk - 1 - c
            rows = pl.ds(pl.multiple_of(cc * C, C), C)
            for b in range(nb):
                work.append((d, b, o_ref, rows, qk_ref[b, rows, :KW], qk_ref[b, rows, KW:], v_ref[b, rows, :],
                             g_ref[b, rows, g0:g0 + KW], st_scr[d, b], bd_scr[d, b]))
        done = [(d, b, o_ref, rows) + chain(d, q, k, v, g, st, bd) for d, b, o_ref, rows, q, k, v, g, st, bd in work]
        for d, b, o_ref, rows, o, s_new in done:
            o_ref[b, rows, :] = o
            for h in range(H):
                st_scr[d, b, h] = s_new[h]
                bd_scr[d, b, h * DK:(h + 1) * DK, h * DV:(h + 1) * DV] = s_new[h].astype(BF16)
        return carry

    lax.fori_loop(0, nchunk, body, 0)

    @pl.when(i == pl.num_programs(1) - 1)
    def _():
        for d, s_ref in enumerate((sf_ref, sb_ref)):
            for b in range(nb):
                s_ref[b] = st_scr[d, b]


def _gla(qkg, vg, la, state_fwd, state_bwd, layer, *, tb, nb):
    B, N, _ = qkg.shape
    H, DK, DV = GLA_HEADS, GLA_DK, GLA_DV
    nblk = N // tb
    fwd = lambda w: pl.BlockSpec((nb, tb, w), lambda b, i: (b, i, 0))
    bwd = lambda w: pl.BlockSpec((nb, tb, w), lambda b, i: (b, nblk - 1 - i, 0))
    in_specs = [fwd(2 * GLA_K_WIDTH), fwd(GLA_WIDTH), fwd(2 * GLA_K_WIDTH),
                bwd(2 * GLA_K_WIDTH), bwd(GLA_WIDTH), bwd(2 * GLA_K_WIDTH)]
    args = [qkg, vg, la, qkg, vg, la]
    has_state = state_fwd is not None
    if has_state:
        s0 = pl.BlockSpec((nb, 1, H, DK, DV), lambda b, i: (b, layer, 0, 0, 0))
        in_specs += [s0, s0]
        args += [state_fwd, state_bwd]
    st = pl.BlockSpec((nb, H, DK, DV), lambda b, i: (b, 0, 0, 0))
    sds = jax.ShapeDtypeStruct
    return pl.pallas_call(
        functools.partial(_gla_kernel, nchunk=tb // GLA_CHUNK, nb=nb, has_state=has_state),
        grid=(B // nb, nblk),
        in_specs=in_specs,
        out_specs=[fwd(GLA_WIDTH), bwd(GLA_WIDTH), st, st],
        out_shape=[sds((B, N, GLA_WIDTH), F32), sds((B, N, GLA_WIDTH), F32),
                   sds((B, H, DK, DV), F32), sds((B, H, DK, DV), F32)],
        scratch_shapes=[pltpu.VMEM((2, nb, H, DK, DV), F32), pltpu.VMEM((2, nb, GLA_K_WIDTH, GLA_WIDTH), BF16)],
        compiler_params=pltpu.CompilerParams(dimension_semantics=("parallel", "arbitrary"),
                                             vmem_limit_bytes=VMEM_LIMIT),
        name="gla",
    )(*args)


def _out_proj_kernel(x_ref, mod_ref, ya_ref, ga_ref, of_ref, ob_ref, gg_ref, yc_ref, gc_ref,
                     m1_ref, m2_ref, m3_ref, woa_ref, wog_ref, woc_ref, wout_ref, ggla_ref, gpost_ref,
                     o_ref, *, d_model):
    D = d_model
    ya = _dot((ya_ref[0] * ga_ref[0]).astype(BF16), woa_ref[...])
    og = of_ref[0] + ob_ref[0]
    gg = gg_ref[0]
    heads = []
    for h in range(GLA_HEADS):
        sl = slice(h * GLA_DV, (h + 1) * GLA_DV)
        heads.append((_rms(og[:, sl], ggla_ref[...]) * gg[:, sl]).astype(BF16))
    yb = _dot(jnp.concatenate(heads, axis=-1), wog_ref[...])
    yc = _dot((yc_ref[0] * gc_ref[0]).astype(BF16), woc_ref[...])
    merged = m1_ref[0] * ya + m2_ref[0] * yb + m3_ref[0] * yc
    out = _rms(_dot(merged.astype(BF16), wout_ref[...]), gpost_ref[...])
    o_ref[0] = x_ref[0] + mod_ref[0][:, 2 * D:] * out


def _out_proj(x, mod, ya, ga, o_f, o_b, gg, yc, gc, m1, m2, m3, lw, *, tm):
    B, N, D = x.shape
    per_batch_mod = mod.shape[0] > 1
    row3 = lambda w: pl.BlockSpec((1, tm, w), lambda b, i: (b, i, 0))
    consts = [lw["w_o_gqa"], lw["w_o_gla"], lw["w_o_mla"], lw["w_out"], lw["g_gla_out"], lw["g_post"]]
    in_specs = [row3(D),
                pl.BlockSpec((1, 1, 3 * D), (lambda b, i: (b, 0, 0)) if per_batch_mod else (lambda b, i: (0, 0, 0))),
                row3(GQA_WIDTH), row3(GQA_WIDTH), row3(GLA_WIDTH), row3(GLA_WIDTH), row3(GLA_WIDTH),
                row3(MLA_WIDTH), row3(MLA_WIDTH), row3(D), row3(D), row3(D)]
    in_specs += [_const_spec(c.shape) for c in consts]
    return pl.pallas_call(
        functools.partial(_out_proj_kernel, d_model=D),
        grid=(B, N // tm),
        in_specs=in_specs,
        out_specs=row3(D),
        out_shape=jax.ShapeDtypeStruct((B, N, D), F32),
        compiler_params=pltpu.CompilerParams(dimension_semantics=("parallel", "parallel"),
                                             vmem_limit_bytes=VMEM_LIMIT),
        name="out_proj",
    )(x, mod, ya, ga, o_f, o_b, gg, yc, gc, m1, m2, m3, *consts)


def _rope_tables(n_tokens):
    t = np.arange(n_tokens)
    pos = np.stack([t // GRID_W, t % GRID_W], axis=0).astype(np.float64)

    def tables(lane_part, lane_in_part, half, active):
        freqs = ROPE_THETA ** (-(lane_in_part % half).astype(np.float64) / half)
        ang = pos[lane_part].T * freqs[None, :]
        cos = np.where(active[None, :], np.cos(ang), 1.0)
        sin = np.where(active[None, :], np.where(lane_in_part < half, -1.0, 1.0)[None, :] * np.sin(ang), 0.0)
        return jnp.asarray(cos, F32), jnp.asarray(sin, F32)

    lane = np.arange(LANES)
    ja = lane % GQA_HEAD_DIM
    ca, sa = tables(ja // (GQA_HEAD_DIM // 2), ja % (GQA_HEAD_DIM // 2), GQA_HEAD_DIM // 4, np.ones(LANES, bool))
    active = (lane >= MLA_ROPE_LANE) & (lane < MLA_ROPE_LANE + MLA_ROPE_DIM)
    jc = np.where(active, lane - MLA_ROPE_LANE, 0)
    cc, sc = tables(jc // (MLA_ROPE_DIM // 2), jc % (MLA_ROPE_DIM // 2), MLA_ROPE_DIM // 4, active)
    return ca, sa, cc, sc


def _pack_layer(l, p, w_in_bf16):
    D = w_in_bf16.shape[1]
    w = w_in_bf16[l]
    o_r = _N_HEAD_COLS
    o_lat = o_r + 2 * GLA_RANK
    o_kr = o_lat + MLA_Q_LORA + MLA_KV_LORA
    o_gc = o_kr + MLA_ROPE_DIM
    z = lambda n: jnp.zeros((D, n), BF16)
    kr = w[:, o_kr:o_gc]
    w_tail = jnp.concatenate(
        [w[:, o_r:o_lat], z(LANES - 2 * GLA_RANK), w[:, o_lat:o_kr],
         kr, z(MLA_ROPE_LANE - MLA_ROPE_DIM), kr, z(LANES - MLA_ROPE_LANE - MLA_ROPE_DIM), w[:, o_gc:]], axis=1)
    assert w_tail.shape[1] == _N_TAIL_COLS
    w_dec = jnp.zeros((LANES, 2 * GLA_K_WIDTH), F32)
    w_dec = w_dec.at[:GLA_RANK, :GLA_K_WIDTH].set(p["w_gla_decay_fwd"][l])
    w_dec = w_dec.at[GLA_RANK:2 * GLA_RANK, GLA_K_WIDTH:].set(p["w_gla_decay_bwd"][l])
    b_dec = jnp.concatenate([p["b_gla_decay_fwd"][l], p["b_gla_decay_bwd"][l]])[None, :]
    pad_q = MLA_QK_PAD - MLA_NOPE_DIM - MLA_ROPE_DIM
    w_uq = p["w_mla_uq"][l].reshape(MLA_Q_LORA, MLA_HEADS, MLA_NOPE_DIM + MLA_ROPE_DIM)
    w_uq = jnp.pad(w_uq, ((0, 0), (0, 0), (0, pad_q))).reshape(MLA_Q_LORA, MLA_QK_WIDTH)
    w_ukv = p["w_mla_ukv"][l].reshape(MLA_KV_LORA, MLA_HEADS, MLA_NOPE_DIM + MLA_V_DIM)
    w_uk = jnp.pad(w_ukv[:, :, :MLA_NOPE_DIM], ((0, 0), (0, 0), (0, MLA_QK_PAD - MLA_NOPE_DIM)))
    w_uk = w_uk.reshape(MLA_KV_LORA, MLA_QK_WIDTH)
    w_uv = w_ukv[:, :, MLA_NOPE_DIM:].reshape(MLA_KV_LORA, MLA_WIDTH)
    seg = np.arange(GQA_WIDTH) // GQA_HEAD_DIM
    place = np.zeros((MLA_ROPE_DIM, MLA_QK_WIDTH), np.float32)
    for h in range(MLA_HEADS):
        place[np.arange(MLA_ROPE_DIM), h * MLA_QK_PAD + MLA_ROPE_LANE + np.arange(MLA_ROPE_DIM)] = 1.0
    return {
        "g_pre": p["g_pre"][l][None, :], "g_post": p["g_post"][l][None, :],
        "w_tail": w_tail,
        "g_q": jnp.tile(p["g_q_norm"][l], GQA_HEADS)[None, :],
        "g_k": jnp.tile(p["g_k_norm"][l], GQA_KV_HEADS)[None, :],
        "ones": jnp.asarray(seg[:, None] == seg[None, :], BF16),
        "w_dec": w_dec.astype(BF16), "b_dec": b_dec,
        "g_mla_q": p["g_mla_q"][l][None, :], "g_mla_kv": p["g_mla_kv"][l][None, :],
        "w_uq": w_uq.astype(BF16), "w_uk": w_uk.astype(BF16), "w_uv": w_uv.astype(BF16),
        "kr_place": jnp.asarray(place, BF16),
        "w_o_gqa": p["w_o_gqa"][l].astype(BF16), "w_o_gla": p["w_o_gla"][l].astype(BF16),
        "w_o_mla": p["w_o_mla"][l].astype(BF16), "w_out": p["w_out"][l].astype(BF16),
        "g_gla_out": p["g_gla_out"][l][None, :],
    }


def _sub_layer(x, mod, w_in_bf16, layer, lw, rope, cache, *, tm, tq_a, tq_c, tk, tb, nb):
    is_ctx = cache is None
    outs = _in_proj(x, mod, w_in_bf16, layer, lw, rope, is_ctx=is_ctx, tm=tm)
    (qa, ka, vat, ga, qkg, vg, gg, la, qc, kc, vct, gc, m1, m2, m3) = outs[:15]
    src_a, src_c = [(ka, vat)], [(kc, vct)]
    s0f = s0b = None
    if not is_ctx:
        ka_p, vat_p, kc_p, vct_p = _cache_prep(cache["gqa_k"], cache["gqa_v"], cache["mla_ckv"],
                                               cache["mla_krope"], layer, lw)
        src_a, src_c = [(ka_p, vat_p)] + src_a, [(kc_p, vct_p)] + src_c
        s0f, s0b = cache["gla_fwd"], cache["gla_bwd"]
    ya = _attention(qa, src_a, n_groups=GQA_KV_HEADS, shared_kv=True, tq=tq_a[0], nsub=tq_a[1], tk=tk)
    yc = _attention(qc, src_c, n_groups=MLA_HEADS, shared_kv=False, tq=tq_c[0], nsub=tq_c[1], tk=tk)
    o_f, o_b, s_f, s_b = _gla(qkg, vg, la, s0f, s0b, layer, tb=tb, nb=nb)
    y = _out_proj(x, mod, ya, ga, o_f, o_b, gg, yc, gc, m1, m2, m3, lw, tm=tm)
    ctx = (outs[15], outs[16], outs[17], outs[18], s_f, s_b) if is_ctx else None
    return y, ctx


def kernel(x_prompt, x_sample, cache_gqa_k, cache_gqa_v, cache_mla_ckv, cache_mla_krope, state_gla_fwd, state_gla_bwd, c, c_ctx, w_mod, b_mod, g_pre, g_post, w_in, g_q_norm, g_k_norm, w_gla_decay_fwd, b_gla_decay_fwd, w_gla_decay_bwd, b_gla_decay_bwd, g_gla_out, g_mla_q, g_mla_kv, w_mla_uq, w_mla_ukv, w_o_gqa, w_o_gla, w_o_mla, w_out):
    params = dict(g_pre=g_pre, g_post=g_post, g_q_norm=g_q_norm, g_k_norm=g_k_norm,
                  w_gla_decay_fwd=w_gla_decay_fwd, b_gla_decay_fwd=b_gla_decay_fwd,
                  w_gla_decay_bwd=w_gla_decay_bwd, b_gla_decay_bwd=b_gla_decay_bwd, g_gla_out=g_gla_out,
                  g_mla_q=g_mla_q, g_mla_kv=g_mla_kv, w_mla_uq=w_mla_uq, w_mla_ukv=w_mla_ukv,
                  w_o_gqa=w_o_gqa, w_o_gla=w_o_gla, w_o_mla=w_o_mla, w_out=w_out)
    depth, D = w_in.shape[0], w_in.shape[1]
    B, S = x_prompt.shape[:2]
    Bd, Nd = x_sample.shape[:2]

    conds = jnp.concatenate([c_ctx[None, :], c, jnp.zeros((SUBLANES - 1 - Bd, D), F32)], axis=0)
    mods = _modulation(conds, w_mod, b_mod)
    rope = _rope_tables(Nd)
    w_in_bf16 = w_in.astype(BF16)
    layers = [_pack_layer(l, params, w_in_bf16) for l in range(depth)]

    xp = x_prompt
    ctx_out = []
    for l in range(depth):
        xp, ctx = _sub_layer(xp, mods[l, 0:1][:, None, :], w_in_bf16, l, layers[l], None, None,
                             tm=min(S, 256), tq_a=(min(S, 256), 2), tq_c=(min(S, 256), 1), tk=512,
                             tb=min(S, 512), nb=2 if B % 2 == 0 else 1)
        ctx_out.append(ctx)

    xs = x_sample
    cache = {"gqa_k": cache_gqa_k, "gqa_v": cache_gqa_v, "mla_ckv": cache_mla_ckv, "mla_krope": cache_mla_krope,
             "gla_fwd": state_gla_fwd, "gla_bwd": state_gla_bwd}
    for l in range(depth):
        xs, _ = _sub_layer(xs, mods[l, 1:1 + Bd][:, None, :], w_in_bf16, l, layers[l], rope, cache,
                           tm=512, tq_a=(512, 8), tq_c=(2048, 8), tk=512, tb=512, nb=2 if Bd % 2 == 0 else 1)

    stack = lambda j: jnp.stack([ctx_out[l][j] for l in range(depth)], axis=1)
    new_k = stack(0).reshape(B, depth, S, GQA_KV_HEADS, GQA_HEAD_DIM)
    new_v = stack(1).reshape(B, depth, S, GQA_KV_HEADS, GQA_HEAD_DIM)
    return (xp, xs, new_k, new_v, stack(2), stack(3), stack(4), stack(5))
```

```python
import functools

import numpy as np
import jax
import jax.numpy as jnp
from jax import lax
from jax.experimental import pallas as pl
from jax.experimental.pallas import tpu as pltpu

F32 = jnp.float32
BF16 = jnp.bfloat16

EPS = 1e-6
LOG2_E = 1.4426950408889634
ROPE_THETA = 10000.0
GRID_W = 64

GQA_HEADS, GQA_KV_HEADS, GQA_HEAD_DIM = 8, 2, 64
GQA_REP = GQA_HEADS // GQA_KV_HEADS
GQA_WIDTH = GQA_HEADS * GQA_HEAD_DIM
GQA_KV_WIDTH = GQA_KV_HEADS * GQA_HEAD_DIM
GLA_HEADS, GLA_DK, GLA_DV = 4, 64, 128
GLA_WIDTH = GLA_HEADS * GLA_DV
GLA_K_WIDTH = GLA_HEADS * GLA_DK
GLA_RANK = 16
GLA_NORMALIZER = 16.0
GLA_CHUNK = 64
MLA_HEADS, MLA_Q_LORA, MLA_KV_LORA = 4, 256, 256
MLA_NOPE_DIM, MLA_ROPE_DIM, MLA_V_DIM = 64, 32, 128
MLA_WIDTH = MLA_HEADS * MLA_V_DIM
MLA_QK_PAD = 128
MLA_QK_WIDTH = MLA_HEADS * MLA_QK_PAD
MLA_ROPE_LANE = MLA_NOPE_DIM

LANES = 128
SUBLANES = 8
VMEM_LIMIT = 56 * 1024 * 1024

assert GQA_KV_WIDTH == LANES and 2 * GQA_HEAD_DIM == LANES and MLA_V_DIM == LANES


def _sigmoid(x):
    return 1.0 / (1.0 + jnp.exp(-x))


def _silu(x):
    return x * _sigmoid(x)


def _log_sigmoid(x):
    return jnp.minimum(x, 0.0) - jnp.log(1.0 + jnp.exp(-jnp.abs(x)))


def _dot(a, b):
    return jnp.dot(a, b, preferred_element_type=F32)


def _dot_nt(a, b):
    return lax.dot_general(a, b, (((1,), (1,)), ((), ())), preferred_element_type=F32)


def _dot_tn(a, b):
    return lax.dot_general(a, b, (((0,), (0,)), ((), ())), preferred_element_type=F32)


def _split3(x):
    hi = x.astype(BF16)
    r1 = x - hi.astype(F32)
    mid = r1.astype(BF16)
    lo = (r1 - mid.astype(F32)).astype(BF16)
    return hi, mid, lo


def _segment_mean_sq(x, ones_bf16, width):
    sq = x * x
    hi = sq.astype(BF16)
    lo = (sq - hi.astype(F32)).astype(BF16)
    return (_dot(hi, ones_bf16) + _dot(lo, ones_bf16)) * (1.0 / width)


def _swap_halves(x, half):
    n = x.shape[-1]
    lane = lax.broadcasted_iota(jnp.int32, x.shape, x.ndim - 1)
    first = (lane % (2 * half)) < half
    return jnp.where(first, pltpu.roll(x, n - half, x.ndim - 1), pltpu.roll(x, half, x.ndim - 1))


def _rope(x, cos, sin_signed, half):
    return x * cos + _swap_halves(x, half) * sin_signed


def _tile_lanes(t, reps):
    return t if reps == 1 else jnp.concatenate([t] * reps, axis=-1)


def _rms(x, gain):
    return x * lax.rsqrt(jnp.mean(x * x, axis=-1, keepdims=True) + EPS) * gain


def _mod_kernel(c_ref, w_ref, b_ref, o_ref):
    c = c_ref[...]
    o_ref[0] = _dot(_silu(c).astype(BF16), w_ref[0].astype(BF16)) + b_ref[0]


def _modulation(conds, w_mod, b_mod):
    L, D, D3 = w_mod.shape
    nj = D3 // D
    return pl.pallas_call(
        _mod_kernel,
        grid=(L, nj),
        in_specs=[pl.BlockSpec((SUBLANES, D), lambda l, j: (0, 0)),
                  pl.BlockSpec((1, D, D), lambda l, j: (l, 0, j)),
                  pl.BlockSpec((1, 1, D), lambda l, j: (l, 0, j))],
        out_specs=pl.BlockSpec((1, SUBLANES, D), lambda l, j: (l, 0, j)),
        out_shape=jax.ShapeDtypeStruct((L, SUBLANES, D3), F32),
        compiler_params=pltpu.CompilerParams(dimension_semantics=("parallel", "parallel")),
        name="modulation",
    )(conds, w_mod, b_mod.reshape(L, 1, D3))


_N_HEAD_COLS = 2816
_T_R, _T_LAT, _T_KR, _T_GC, _T_M = 0, 128, 640, 768, 1280
_N_TAIL_COLS = 4352
_C_QA, _C_KV, _C_GA, _C_QKG, _C_VG, _C_GG = 0, 512, 768, 1280, 1792, 2304


def _in_proj_kernel(*refs, is_ctx, d_model):
    it = iter(refs)
    x_ref, mod_ref, gpre_ref, wh_ref, wt_ref, gq_ref, gk_ref, ones_ref, wdec_ref, bdec_ref = (next(it) for _ in range(10))
    gmq_ref, gmkv_ref, wuq_ref, wk_ref, wv_ref = (next(it) for _ in range(5))
    if not is_ctx:
        ca_ref, sa_ref, cc_ref, sc_ref = (next(it) for _ in range(4))
    (qa_o, ka_o, vat_o, ga_o, qkg_o, vg_o, gg_o, la_o,
     qc_o, kc_o, vct_o, gc_o, m1_o, m2_o, m3_o) = (next(it) for _ in range(15))
    if is_ctx:
        ka32_o, va32_o, ckv32_o, kr32_o = (next(it) for _ in range(4))

    D = d_model
    mod = mod_ref[0]
    shift, scale = mod[:, :D], mod[:, D:2 * D]
    hb = (_rms(x_ref[0], gpre_ref[...]) * (1.0 + scale) + shift).astype(BF16)

    def head(c0, width):
        return _dot(hb, wh_ref[0, :, c0:c0 + width])

    def tail(c0, width):
        return _dot(hb, wt_ref[:, c0:c0 + width])

    ones = ones_ref[...]

    qa = head(_C_QA, GQA_WIDTH)
    qa = qa * lax.rsqrt(_segment_mean_sq(qa, ones, GQA_HEAD_DIM) + EPS) * gq_ref[...]
    kv = head(_C_KV, 2 * GQA_KV_WIDTH)
    ka, va = kv[:, :GQA_KV_WIDTH], kv[:, GQA_KV_WIDTH:]
    ka = ka * lax.rsqrt(_segment_mean_sq(ka, ones[:GQA_KV_WIDTH, :GQA_KV_WIDTH], GQA_HEAD_DIM) + EPS) * gk_ref[...]
    if is_ctx:
        ka32_o[0] = ka
        va32_o[0] = va
    else:
        ca, sa = ca_ref[...], sa_ref[...]
        qa = _rope(qa, _tile_lanes(ca, GQA_WIDTH // LANES), _tile_lanes(sa, GQA_WIDTH // LANES), GQA_HEAD_DIM // 4)
        ka = _rope(ka, ca, sa, GQA_HEAD_DIM // 4)
    qa_o[0] = (qa * (GQA_HEAD_DIM ** -0.5 * LOG2_E)).astype(BF16)
    ka_o[0] = ka.astype(BF16)
    vat_o[0] = va.T.astype(BF16)
    ga_o[0] = _silu(head(_C_GA, GQA_WIDTH)).astype(BF16)

    qkg = head(_C_QKG, 2 * GLA_K_WIDTH)
    lane = lax.broadcasted_iota(jnp.int32, qkg.shape, 1)
    qkg_o[0] = jnp.where(lane < GLA_K_WIDTH, qkg * GLA_DK ** -0.5, qkg)
    vg_o[0] = head(_C_VG, GLA_WIDTH)
    r = tail(_T_R, LANES).astype(BF16)
    la_o[0] = _log_sigmoid(_dot(r, wdec_ref[...]) + bdec_ref[...]) * (1.0 / GLA_NORMALIZER)
    gg_o[0] = _silu(head(_C_GG, GLA_WIDTH)).astype(BF16)

    lat = tail(_T_LAT, MLA_Q_LORA + MLA_KV_LORA)
    ql = _rms(lat[:, :MLA_Q_LORA], gmq_ref[...])
    ckv = _rms(lat[:, MLA_Q_LORA:], gmkv_ref[...])
    qc = _dot(ql.astype(BF16), wuq_ref[...])
    krb = tail(_T_KR, LANES)
    if is_ctx:
        ckv32_o[0] = ckv
        kr32_o[0] = krb[:, :MLA_ROPE_DIM]
    else:
        cc, sc = cc_ref[...], sc_ref[...]
        qc = _rope(qc, _tile_lanes(cc, MLA_HEADS), _tile_lanes(sc, MLA_HEADS), MLA_ROPE_DIM // 4)
        krb = _rope(krb, cc, sc, MLA_ROPE_DIM // 4)
    lane = lax.broadcasted_iota(jnp.int32, krb.shape, 1)
    kr_part = jnp.where((lane >= MLA_ROPE_LANE) & (lane < MLA_ROPE_LANE + MLA_ROPE_DIM), krb, 0.0)
    qc_o[0] = (qc * ((MLA_NOPE_DIM + MLA_ROPE_DIM) ** -0.5 * LOG2_E)).astype(BF16)
    ckvb = ckv.astype(BF16)
    kc_o[0] = (_dot(ckvb, wk_ref[...]) + _tile_lanes(kr_part, MLA_HEADS)).astype(BF16)
    vct_o[0] = _dot(ckvb, wv_ref[...]).T.astype(BF16)
    gc_o[0] = _silu(tail(_T_GC, MLA_WIDTH)).astype(BF16)

    m1_o[0] = _sigmoid(tail(_T_M, D)).astype(BF16)
    m2_o[0] = _sigmoid(tail(_T_M + D, D)).astype(BF16)
    m3_o[0] = _sigmoid(tail(_T_M + 2 * D, D)).astype(BF16)


def _const_spec(shape):
    nd = len(shape)
    return pl.BlockSpec(shape, lambda b, i, _nd=nd: (0,) * _nd, pipeline_mode=pl.Buffered(1))


def _in_proj(x, mod, w_in_bf16, layer, lw, rope, *, is_ctx, tm):
    B, N, D = x.shape
    per_batch_mod = mod.shape[0] > 1
    row3 = lambda w: pl.BlockSpec((1, tm, w), lambda b, i: (b, i, 0))
    col3 = lambda h: pl.BlockSpec((1, h, tm), lambda b, i: (b, 0, i))
    consts = [lw["g_q"], lw["g_k"], lw["ones"], lw["w_dec"], lw["b_dec"],
              lw["g_mla_q"], lw["g_mla_kv"], lw["w_uq"], lw["w_uk"], lw["w_uv"]]
    in_specs = [row3(D),
                pl.BlockSpec((1, 1, 3 * D), (lambda b, i: (b, 0, 0)) if per_batch_mod else (lambda b, i: (0, 0, 0))),
                _const_spec(lw["g_pre"].shape),
                pl.BlockSpec((1, D, _N_HEAD_COLS), lambda b, i: (layer, 0, 0), pipeline_mode=pl.Buffered(1)),
                _const_spec(lw["w_tail"].shape)]
    in_specs += [_const_spec(c.shape) for c in consts]
    args = [x, mod, lw["g_pre"], w_in_bf16, lw["w_tail"]] + consts
    if not is_ctx:
        in_specs += [pl.BlockSpec((tm, LANES), lambda b, i: (i, 0))] * 4
        args += list(rope)
    sds = jax.ShapeDtypeStruct
    out_shape = [
        sds((B, N, GQA_WIDTH), BF16), sds((B, N, GQA_KV_WIDTH), BF16), sds((B, GQA_KV_WIDTH, N), BF16),
        sds((B, N, GQA_WIDTH), BF16),
        sds((B, N, 2 * GLA_K_WIDTH), F32), sds((B, N, GLA_WIDTH), F32), sds((B, N, GLA_WIDTH), BF16),
        sds((B, N, 2 * GLA_K_WIDTH), F32),
        sds((B, N, MLA_QK_WIDTH), BF16), sds((B, N, MLA_QK_WIDTH), BF16), sds((B, MLA_WIDTH, N), BF16),
        sds((B, N, MLA_WIDTH), BF16),
        sds((B, N, D), BF16), sds((B, N, D), BF16), sds((B, N, D), BF16)]
    out_specs = [
        row3(GQA_WIDTH), row3(GQA_KV_WIDTH), col3(GQA_KV_WIDTH), row3(GQA_WIDTH),
        row3(2 * GLA_K_WIDTH), row3(GLA_WIDTH), row3(GLA_WIDTH), row3(2 * GLA_K_WIDTH),
        row3(MLA_QK_WIDTH), row3(MLA_QK_WIDTH), col3(MLA_WIDTH), row3(MLA_WIDTH),
        row3(D), row3(D), row3(D)]
    if is_ctx:
        out_shape += [sds((B, N, GQA_KV_WIDTH), F32), sds((B, N, GQA_KV_WIDTH), F32),
                      sds((B, N, MLA_KV_LORA), F32), sds((B, N, MLA_ROPE_DIM), F32)]
        out_specs += [row3(GQA_KV_WIDTH), row3(GQA_KV_WIDTH), row3(MLA_KV_LORA), row3(MLA_ROPE_DIM)]
    return pl.pallas_call(
        functools.partial(_in_proj_kernel, is_ctx=is_ctx, d_model=D),
        grid=(B, N // tm),
        in_specs=in_specs, out_specs=out_specs, out_shape=out_shape,
        compiler_params=pltpu.CompilerParams(dimension_semantics=("parallel", "parallel"),
                                             vmem_limit_bytes=VMEM_LIMIT),
        name="in_proj_ctx" if is_ctx else "in_proj_lat",
    )(*args)


def _cache_prep_kernel(gk_ref, gv_ref, ckv_ref, kr_ref, wk_ref, wv_ref, place_ref, ka_o, vat_o, kc_o, vct_o):
    ka_o[0] = gk_ref[0, 0].astype(BF16)
    vat_o[0] = gv_ref[0, 0].T.astype(BF16)
    ckvb = ckv_ref[0, 0].astype(BF16)
    kc_o[0] = (_dot(ckvb, wk_ref[...]) + _dot(kr_ref[0, 0].astype(BF16), place_ref[...])).astype(BF16)
    vct_o[0] = _dot(ckvb, wv_ref[...]).T.astype(BF16)


def _cache_prep(cache_gqa_k, cache_gqa_v, cache_mla_ckv, cache_mla_krope, layer, lw):
    B, L, P = cache_gqa_k.shape[:3]
    gk = cache_gqa_k.reshape(B, L, P, GQA_KV_WIDTH)
    gv = cache_gqa_v.reshape(B, L, P, GQA_KV_WIDTH)
    lsel = lambda w: pl.BlockSpec((1, 1, P, w), lambda b: (b, layer, 0, 0))
    full = lambda a: pl.BlockSpec(a.shape, lambda b: (0,) * a.ndim)
    sds = jax.ShapeDtypeStruct
    return pl.pallas_call(
        _cache_prep_kernel,
        grid=(B,),
        in_specs=[lsel(GQA_KV_WIDTH), lsel(GQA_KV_WIDTH), lsel(MLA_KV_LORA), lsel(MLA_ROPE_DIM),
                  full(lw["w_uk"]), full(lw["w_uv"]), full(lw["kr_place"])],
        out_specs=[pl.BlockSpec((1, P, GQA_KV_WIDTH), lambda b: (b, 0, 0)),
                   pl.BlockSpec((1, GQA_KV_WIDTH, P), lambda b: (b, 0, 0)),
                   pl.BlockSpec((1, P, MLA_QK_WIDTH), lambda b: (b, 0, 0)),
                   pl.BlockSpec((1, MLA_WIDTH, P), lambda b: (b, 0, 0))],
        out_shape=[sds((B, P, GQA_KV_WIDTH), BF16), sds((B, GQA_KV_WIDTH, P), BF16),
                   sds((B, P, MLA_QK_WIDTH), BF16), sds((B, MLA_WIDTH, P), BF16)],
        compiler_params=pltpu.CompilerParams(dimension_semantics=("parallel",)),
        name="cache_prep",
    )(gk, gv, cache_mla_ckv, cache_mla_krope, lw["w_uk"], lw["w_uv"], lw["kr_place"])


def _attention_kernel(*refs, n_src, tk, nsub, rep, shared_kv):
    q_ref, o_ref, s_scr = refs[0], refs[1 + 2 * n_src], refs[2 + 2 * n_src]
    srcs = [(refs[1 + 2 * j], refs[2 + 2 * j]) for j in range(n_src)]
    nb = q_ref.shape[0]
    tqs = q_ref.shape[1] // nsub
    g = pl.program_id(1)

    chunks = []
    off = 0
    for k_ref, vt_ref in srcs:
        n = k_ref.shape[1]
        step = min(tk, n)
        for c in range(n // step):
            chunks.append((k_ref, vt_ref, c * step, step, off))
            off += step

    def scores(j):
        b, rows = j // nsub, slice((j % nsub) * tqs, (j % nsub + 1) * tqs)
        if shared_kv:
            hd = LANES // 2
            x = q_ref[b, rows, :].astype(F32)
            lane = lax.broadcasted_iota(jnp.int32, (tqs, LANES), 1)
            pieces = []
            for r in range(rep):
                win = x[:, (r // 2) * LANES:(r // 2 + 1) * LANES]
                win = jnp.where(g == r % 2, win, pltpu.roll(win, hd, 1))
                pieces.append(jnp.where(lane // hd == g, win, 0.0).astype(BF16))
            qcat = jnp.concatenate(pieces, axis=0)
        else:
            qcat = q_ref[b, rows, :]
        W = qcat.shape[0]
        m_acc = jnp.full((SUBLANES, W), -jnp.inf, F32)
        for k_ref, _, c0, step, off in chunks:
            s = _dot_nt(k_ref[b, c0:c0 + step, :], qcat)
            s_scr[j, off:off + step, :] = s
            m_acc = jnp.maximum(m_acc, jnp.max(s.reshape(step // SUBLANES, SUBLANES, W), axis=0))
        return jnp.max(m_acc, axis=0, keepdims=True)

    def outputs(j, m):
        W = m.shape[1]
        b, rows = j // nsub, slice((j % nsub) * tqs, (j % nsub + 1) * tqs)
        l_acc = jnp.zeros((SUBLANES, W), F32)
        acc = jnp.zeros((LANES, W), F32)
        for _, vt_ref, c0, step, off in chunks:
            p = jnp.exp2(s_scr[j, off:off + step, :] - m)
            l_acc = l_acc + jnp.sum(p.reshape(step // SUBLANES, SUBLANES, W), axis=0)
            acc = acc + _dot(vt_ref[b, :, c0:c0 + step], p.astype(BF16))
        ot = acc / jnp.sum(l_acc, axis=0, keepdims=True)
        if shared_kv:
            ot = jnp.where(g == 0, ot[:LANES // 2], ot[LANES // 2:])
            for pair in range(rep // 2):
                two = jnp.concatenate([ot[:, (2 * pair) * tqs:(2 * pair + 1) * tqs],
                                       ot[:, (2 * pair + 1) * tqs:(2 * pair + 2) * tqs]], axis=0)
                o_ref[b, rows, pair * LANES:(pair + 1) * LANES] = two.T
        else:
            o_ref[b, rows, :] = ot.T

    m_prev = scores(0)
    for j in range(1, nb * nsub):
        m_next = scores(j)
        outputs(j - 1, m_prev)
        m_prev = m_next
    outputs(nb * nsub - 1, m_prev)


def _attention(q, sources, *, n_groups, shared_kv, tq, nsub, tk, nb):
    B, N, Wq = q.shape
    G = n_groups
    tqs = tq // nsub
    tq = min(tq, N)
    nsub = tq // tqs
    qw = Wq // G
    rep = qw // (LANES // 2) if shared_kv else 1
    kv_blk = (lambda b, g, i: (b, 0, 0)) if shared_kv else (lambda b, g, i: (b, 0, g))
    vt_blk = (lambda b, g, i: (b, 0, 0)) if shared_kv else (lambda b, g, i: (b, g, 0))
    in_specs = [pl.BlockSpec((nb, tq, qw), lambda b, g, i: (b, i, g))]
    args = [q]
    nk = 0
    for k, vt in sources:
        n = k.shape[1]
        nk += n
        in_specs += [pl.BlockSpec((nb, n, LANES), kv_blk), pl.BlockSpec((nb, LANES, n), vt_blk)]
        args += [k, vt]
    ow = qw if shared_kv else LANES
    return pl.pallas_call(
        functools.partial(_attention_kernel, n_src=len(sources), tk=tk, nsub=nsub, rep=rep, shared_kv=shared_kv),
        grid=(B // nb, G, N // tq),
        in_specs=in_specs,
        out_specs=pl.BlockSpec((nb, tq, ow), lambda b, g, i: (b, i, g)),
        out_shape=jax.ShapeDtypeStruct((B, N, G * ow), F32),
        scratch_shapes=[pltpu.VMEM((nb * nsub, nk, rep * tq // nsub), F32)],
        compiler_params=pltpu.CompilerParams(dimension_semantics=("parallel", "parallel", "parallel"),
                                             vmem_limit_bytes=VMEM_LIMIT),
        name="attention_gqa" if shared_kv else "attention_mla",
    )(*args)


def _gla_kernel(*refs, nchunk, nb, has_state):
    it = iter(refs)
    qkf_ref, vf_ref, gf_ref, qkb_ref, vb_ref, gb_ref = (next(it) for _ in range(6))
    if has_state:
        s0f_ref, s0b_ref = next(it), next(it)
    of_ref, ob_ref, sf_ref, sb_ref, st_scr, bd_scr = (next(it) for _ in range(6))
    C, H, DK, DV, KW, VW = GLA_CHUNK, GLA_HEADS, GLA_DK, GLA_DV, GLA_K_WIDTH, GLA_WIDTH
    i = pl.program_id(1)

    @pl.when(i == 0)
    def _():
        bd_scr[...] = jnp.zeros(bd_scr.shape, BF16)
        for d, s0_ref in enumerate((s0f_ref, s0b_ref) if has_state else (None, None)):
            for b in range(nb):
                for h in range(H):
                    s0 = s0_ref[b, 0, h] if has_state else jnp.zeros((DK, DV), F32)
                    st_scr[d, b, h] = s0
                    bd_scr[d, b, h * DK:(h + 1) * DK, h * DV:(h + 1) * DV] = s0.astype(BF16)

    row = lax.broadcasted_iota(jnp.int32, (C, H * C), 0)
    col = lax.broadcasted_iota(jnp.int32, (C, H * C), 1) % C
    keep = (row >= col, row <= col)
    tok = lax.broadcasted_iota(jnp.int32, (C, KW), 0)
    k_head = lax.broadcasted_iota(jnp.int32, (C, KW), 1) // DK
    v_head = lax.broadcasted_iota(jnp.int32, (C, VW), 1) // DV
    dirs = ((qkf_ref, vf_ref, gf_ref, of_ref, 0), (qkb_ref, vb_ref, gb_ref, ob_ref, KW))

    def chain(d, q, k, v, g, st, bd):
        cum = g
        shift = 1
        while shift < C:
            if d == 0:
                cum = cum + jnp.where(tok >= shift, pltpu.roll(cum, shift, 0), 0.0)
            else:
                cum = cum + jnp.where(tok < C - shift, pltpu.roll(cum, C - shift, 0), 0.0)
            shift *= 2
        last = cum[C - 1:C, :] if d == 0 else cum[0:1, :]
        qd = q * jnp.exp(cum)
        kd = k * jnp.exp(last - cum)
        kd_stack = jnp.concatenate([jnp.where(k_head == h, kd, 0.0).astype(BF16) for h in range(H)], axis=0)
        a = _dot_nt((qd * jnp.exp(-last)).astype(BF16), kd_stack)
        a = jnp.where(keep[d], a, 0.0).astype(BF16)
        v_stack = jnp.concatenate([jnp.where(v_head == h, v, 0.0).astype(BF16) for h in range(H)], axis=0)
        o = _dot(qd.astype(BF16), bd) + _dot(a, v_stack)
        kdt = kd.T.astype(BF16)
        decay = jnp.exp(jnp.broadcast_to(last, (LANES, KW)).T)
        vb = v.astype(BF16)
        s_new = [st[h] * decay[h * DK:(h + 1) * DK, :] + _dot(kdt[h * DK:(h + 1) * DK, :], vb[:, h * DV:(h + 1) * DV])
                 for h in range(H)]
        return o, s_new

    def body(c, carry):
        work = []
        for d, (qk_ref, v_ref, g_ref, o_ref, g0) in enumerate(dirs):
            cc = c if d == 0 else nchunk - 1 - c
            rows = pl.ds(pl.multiple_of(cc * C, C), C)
            for b in range(nb):
                work.append((d, b, o_ref, rows, qk_ref[b, rows, :KW], qk_ref[b, rows, KW:], v_ref[b, rows, :],
                             g_ref[b, rows, g0:g0 + KW], st_scr[d, b], bd_scr[d, b]))
        done = [(d, b, o_ref, rows) + chain(d, q, k, v, g, st, bd) for d, b, o_ref, rows, q, k, v, g, st, bd in work]
        for d, b, o_ref, rows, o, s_new in done:
            o_ref[b, rows, :] = o
            for h in range(H):
                st_scr[d, b, h] = s_new[h]
                bd_scr[d, b, h * DK:(h + 1) * DK, h * DV:(h + 1) * DV] = s_new[h].astype(BF16)
        return carry

    lax.fori_loop(0, nchunk, body, 0)

    @pl.when(i == pl.num_programs(1) - 1)
    def _():
        for d, s_ref in enumerate((sf_ref, sb_ref)):
            for b in range(nb):
                s_ref[b] = st_scr[d, b]


def _gla(qkg, vg, la, state_fwd, state_bwd, layer, *, tb, nb):
    B, N, _ = qkg.shape
    H, DK, DV = GLA_HEADS, GLA_DK, GLA_DV
    nblk = N // tb
    fwd = lambda w: pl.BlockSpec((nb, tb, w), lambda b, i: (b, i, 0))
    bwd = lambda w: pl.BlockSpec((nb, tb, w), lambda b, i: (b, nblk - 1 - i, 0))
    in_specs = [fwd(2 * GLA_K_WIDTH), fwd(GLA_WIDTH), fwd(2 * GLA_K_WIDTH),
                bwd(2 * GLA_K_WIDTH), bwd(GLA_WIDTH), bwd(2 * GLA_K_WIDTH)]
    args = [qkg, vg, la, qkg, vg, la]
    has_state = state_fwd is not None
    if has_state:
        s0 = pl.BlockSpec((nb, 1, H, DK, DV), lambda b, i: (b, layer, 0, 0, 0))
        in_specs += [s0, s0]
        args += [state_fwd, state_bwd]
    st = pl.BlockSpec((nb, H, DK, DV), lambda b, i: (b, 0, 0, 0))
    sds = jax.ShapeDtypeStruct
    return pl.pallas_call(
        functools.partial(_gla_kernel, nchunk=tb // GLA_CHUNK, nb=nb, has_state=has_state),
        grid=(B // nb, nblk),
        in_specs=in_specs,
        out_specs=[fwd(GLA_WIDTH), bwd(GLA_WIDTH), st, st],
        out_shape=[sds((B, N, GLA_WIDTH), F32), sds((B, N, GLA_WIDTH), F32),
                   sds((B, H, DK, DV), F32), sds((B, H, DK, DV), F32)],
        scratch_shapes=[pltpu.VMEM((2, nb, H, DK, DV), F32), pltpu.VMEM((2, nb, GLA_K_WIDTH, GLA_WIDTH), BF16)],
        compiler_params=pltpu.CompilerParams(dimension_semantics=("parallel", "arbitrary"),
                                             vmem_limit_bytes=VMEM_LIMIT),
        name="gla",
    )(*args)


def _out_proj_kernel(x_ref, mod_ref, ya_ref, ga_ref, of_ref, ob_ref, gg_ref, yc_ref, gc_ref,
                     m1_ref, m2_ref, m3_ref, woa_ref, wog_ref, woc_ref, wout_ref, ggla_ref, gpost_ref,
                     o_ref, *, d_model):
    D = d_model
    ya = _dot((ya_ref[0] * ga_ref[0]).astype(BF16), woa_ref[...])
    og = of_ref[0] + ob_ref[0]
    gg = gg_ref[0]
    heads = []
    for h in range(GLA_HEADS):
        sl = slice(h * GLA_DV, (h + 1) * GLA_DV)
        heads.append((_rms(og[:, sl], ggla_ref[...]) * gg[:, sl]).astype(BF16))
    yb = _dot(jnp.concatenate(heads, axis=-1), wog_ref[...])
    yc = _dot((yc_ref[0] * gc_ref[0]).astype(BF16), woc_ref[...])
    merged = m1_ref[0] * ya + m2_ref[0] * yb + m3_ref[0] * yc
    out = _rms(_dot(merged.astype(BF16), wout_ref[...]), gpost_ref[...])
    o_ref[0] = x_ref[0] + mod_ref[0][:, 2 * D:] * out


def _out_proj(x, mod, ya, ga, o_f, o_b, gg, yc, gc, m1, m2, m3, lw, *, tm):
    B, N, D = x.shape
    per_batch_mod = mod.shape[0] > 1
    row3 = lambda w: pl.BlockSpec((1, tm, w), lambda b, i: (b, i, 0))
    consts = [lw["w_o_gqa"], lw["w_o_gla"], lw["w_o_mla"], lw["w_out"], lw["g_gla_out"], lw["g_post"]]
    in_specs = [row3(D),
                pl.BlockSpec((1, 1, 3 * D), (lambda b, i: (b, 0, 0)) if per_batch_mod else (lambda b, i: (0, 0, 0))),
                row3(GQA_WIDTH), row3(GQA_WIDTH), row3(GLA_WIDTH), row3(GLA_WIDTH), row3(GLA_WIDTH),
                row3(MLA_WIDTH), row3(MLA_WIDTH), row3(D), row3(D), row3(D)]
    in_specs += [_const_spec(c.shape) for c in consts]
    return pl.pallas_call(
        functools.partial(_out_proj_kernel, d_model=D),
        grid=(B, N // tm),
        in_specs=in_specs,
        out_specs=row3(D),
        out_shape=jax.ShapeDtypeStruct((B, N, D), F32),
        compiler_params=pltpu.CompilerParams(dimension_semantics=("parallel", "parallel"),
                                             vmem_limit_bytes=VMEM_LIMIT),
        name="out_proj",
    )(x, mod, ya, ga, o_f, o_b, gg, yc, gc, m1, m2, m3, *consts)


def _rope_tables(n_tokens):
    t = np.arange(n_tokens)
    pos = np.stack([t // GRID_W, t % GRID_W], axis=0).astype(np.float64)

    def tables(lane_part, lane_in_part, half, active):
        freqs = ROPE_THETA ** (-(lane_in_part % half).astype(np.float64) / half)
        ang = pos[lane_part].T * freqs[None, :]
        cos = np.where(active[None, :], np.cos(ang), 1.0)
        sin = np.where(active[None, :], np.where(lane_in_part < half, -1.0, 1.0)[None, :] * np.sin(ang), 0.0)
        return jnp.asarray(cos, F32), jnp.asarray(sin, F32)

    lane = np.arange(LANES)
    ja = lane % GQA_HEAD_DIM
    ca, sa = tables(ja // (GQA_HEAD_DIM // 2), ja % (GQA_HEAD_DIM // 2), GQA_HEAD_DIM // 4, np.ones(LANES, bool))
    active = (lane >= MLA_ROPE_LANE) & (lane < MLA_ROPE_LANE + MLA_ROPE_DIM)
    jc = np.where(active, lane - MLA_ROPE_LANE, 0)
    cc, sc = tables(jc // (MLA_ROPE_DIM // 2), jc % (MLA_ROPE_DIM // 2), MLA_ROPE_DIM // 4, active)
    return ca, sa, cc, sc


def _pack_layer(l, p, w_in):
    D = w_in.shape[1]
    o_r = _N_HEAD_COLS
    o_lat = o_r + 2 * GLA_RANK
    o_kr = o_lat + MLA_Q_LORA + MLA_KV_LORA
    o_gc = o_kr + MLA_ROPE_DIM
    w = w_in[l, :, o_r:]
    z = lambda n: jnp.zeros((D, n), F32)
    kr = w[:, o_kr - o_r:o_gc - o_r]
    w_tail = jnp.concatenate(
        [w[:, :o_lat - o_r], z(LANES - 2 * GLA_RANK), w[:, o_lat - o_r:o_kr - o_r],
         kr, z(MLA_ROPE_LANE - MLA_ROPE_DIM), kr, z(LANES - MLA_ROPE_LANE - MLA_ROPE_DIM), w[:, o_gc - o_r:]],
        axis=1).astype(BF16)
    assert w_tail.shape[1] == _N_TAIL_COLS
    w_dec = jnp.zeros((LANES, 2 * GLA_K_WIDTH), F32)
    w_dec = w_dec.at[:GLA_RANK, :GLA_K_WIDTH].set(p["w_gla_decay_fwd"][l])
    w_dec = w_dec.at[GLA_RANK:2 * GLA_RANK, GLA_K_WIDTH:].set(p["w_gla_decay_bwd"][l])
    b_dec = jnp.concatenate([p["b_gla_decay_fwd"][l], p["b_gla_decay_bwd"][l]])[None, :]
    pad_q = MLA_QK_PAD - MLA_NOPE_DIM - MLA_ROPE_DIM
    w_uq = p["w_mla_uq"][l].reshape(MLA_Q_LORA, MLA_HEADS, MLA_NOPE_DIM + MLA_ROPE_DIM)
    w_uq = jnp.pad(w_uq, ((0, 0), (0, 0), (0, pad_q))).reshape(MLA_Q_LORA, MLA_QK_WIDTH)
    w_ukv = p["w_mla_ukv"][l].reshape(MLA_KV_LORA, MLA_HEADS, MLA_NOPE_DIM + MLA_V_DIM)
    w_uk = jnp.pad(w_ukv[:, :, :MLA_NOPE_DIM], ((0, 0), (0, 0), (0, MLA_QK_PAD - MLA_NOPE_DIM)))
    w_uk = w_uk.reshape(MLA_KV_LORA, MLA_QK_WIDTH)
    w_uv = w_ukv[:, :, MLA_NOPE_DIM:].reshape(MLA_KV_LORA, MLA_WIDTH)
    seg = np.arange(GQA_WIDTH) // GQA_HEAD_DIM
    place = np.zeros((MLA_ROPE_DIM, MLA_QK_WIDTH), np.float32)
    for h in range(MLA_HEADS):
        place[np.arange(MLA_ROPE_DIM), h * MLA_QK_PAD + MLA_ROPE_LANE + np.arange(MLA_ROPE_DIM)] = 1.0
    return {
        "g_pre": p["g_pre"][l][None, :], "g_post": p["g_post"][l][None, :],
        "w_tail": w_tail,
        "g_q": jnp.tile(p["g_q_norm"][l], GQA_HEADS)[None, :],
        "g_k": jnp.tile(p["g_k_norm"][l], GQA_KV_HEADS)[None, :],
        "ones": jnp.asarray(seg[:, None] == seg[None, :], BF16),
        "w_dec": w_dec.astype(BF16), "b_dec": b_dec,
        "g_mla_q": p["g_mla_q"][l][None, :], "g_mla_kv": p["g_mla_kv"][l][None, :],
        "w_uq": w_uq.astype(BF16), "w_uk": w_uk.astype(BF16), "w_uv": w_uv.astype(BF16),
        "kr_place": jnp.asarray(place, BF16),
        "w_o_gqa": p["w_o_gqa"][l].astype(BF16), "w_o_gla": p["w_o_gla"][l].astype(BF16),
        "w_o_mla": p["w_o_mla"][l].astype(BF16), "w_out": p["w_out"][l].astype(BF16),
        "g_gla_out": p["g_gla_out"][l][None, :],
    }


def _sub_layer(x, mod, w_in_bf16, layer, lw, rope, cache, *, tm, tq_a, tq_c, tk, tb, nb):
    is_ctx = cache is None
    outs = _in_proj(x, mod, w_in_bf16, layer, lw, rope, is_ctx=is_ctx, tm=tm)
    (qa, ka, vat, ga, qkg, vg, gg, la, qc, kc, vct, gc, m1, m2, m3) = outs[:15]
    src_a, src_c = [(ka, vat)], [(kc, vct)]
    s0f = s0b = None
    if not is_ctx:
        ka_p, vat_p, kc_p, vct_p = _cache_prep(cache["gqa_k"], cache["gqa_v"], cache["mla_ckv"],
                                               cache["mla_krope"], layer, lw)
        src_a, src_c = [(ka_p, vat_p)] + src_a, [(kc_p, vct_p)] + src_c
        s0f, s0b = cache["gla_fwd"], cache["gla_bwd"]
    ya = _attention(qa, src_a, n_groups=GQA_KV_HEADS, shared_kv=True, tq=tq_a[0], nsub=tq_a[1], nb=tq_a[2], tk=tk)
    yc = _attention(qc, src_c, n_groups=MLA_HEADS, shared_kv=False, tq=tq_c[0], nsub=tq_c[1], nb=tq_c[2], tk=tk)
    o_f, o_b, s_f, s_b = _gla(qkg, vg, la, s0f, s0b, layer, tb=tb, nb=nb)
    y = _out_proj(x, mod, ya, ga, o_f, o_b, gg, yc, gc, m1, m2, m3, lw, tm=tm)
    ctx = (outs[15], outs[16], outs[17], outs[18], s_f, s_b) if is_ctx else None
    return y, ctx


def kernel(x_prompt, x_sample, cache_gqa_k, cache_gqa_v, cache_mla_ckv, cache_mla_krope, state_gla_fwd, state_gla_bwd, c, c_ctx, w_mod, b_mod, g_pre, g_post, w_in, g_q_norm, g_k_norm, w_gla_decay_fwd, b_gla_decay_fwd, w_gla_decay_bwd, b_gla_decay_bwd, g_gla_out, g_mla_q, g_mla_kv, w_mla_uq, w_mla_ukv, w_o_gqa, w_o_gla, w_o_mla, w_out):
    params = dict(g_pre=g_pre, g_post=g_post, g_q_norm=g_q_norm, g_k_norm=g_k_norm,
                  w_gla_decay_fwd=w_gla_decay_fwd, b_gla_decay_fwd=b_gla_decay_fwd,
                  w_gla_decay_bwd=w_gla_decay_bwd, b_gla_decay_bwd=b_gla_decay_bwd, g_gla_out=g_gla_out,
                  g_mla_q=g_mla_q, g_mla_kv=g_mla_kv, w_mla_uq=w_mla_uq, w_mla_ukv=w_mla_ukv,
                  w_o_gqa=w_o_gqa, w_o_gla=w_o_gla, w_o_mla=w_o_mla, w_out=w_out)
    depth, D = w_in.shape[0], w_in.shape[1]
    B, S = x_prompt.shape[:2]
    Bd, Nd = x_sample.shape[:2]

    conds = jnp.concatenate([c_ctx[None, :], c, jnp.zeros((SUBLANES - 1 - Bd, D), F32)], axis=0)
    mods = _modulation(conds, w_mod, b_mod)
    rope = _rope_tables(Nd)
    w_in_bf16 = w_in[:, :, :_N_HEAD_COLS].astype(BF16)
    layers = [_pack_layer(l, params, w_in) for l in range(depth)]

    xp = x_prompt
    ctx_out = []
    for l in range(depth):
        xp, ctx = _sub_layer(xp, mods[l, 0:1][:, None, :], w_in_bf16, l, layers[l], None, None,
                             tm=min(S, 256), tq_a=(min(S, 256), 2, 4 if B % 4 == 0 else 1), tq_c=(min(S, 256), 1, 4 if B % 4 == 0 else 1), tk=512,
                             tb=min(S, 512), nb=2 if B % 2 == 0 else 1)
        ctx_out.append(ctx)

    xs = x_sample
    cache = {"gqa_k": cache_gqa_k, "gqa_v": cache_gqa_v, "mla_ckv": cache_mla_ckv, "mla_krope": cache_mla_krope,
             "gla_fwd": state_gla_fwd, "gla_bwd": state_gla_bwd}
    for l in range(depth):
        xs, _ = _sub_layer(xs, mods[l, 1:1 + Bd][:, None, :], w_in_bf16, l, layers[l], rope, cache,
                           tm=512, tq_a=(512, 8, 1), tq_c=(2048, 8, 1), tk=512, tb=512, nb=2 if Bd % 2 == 0 else 1)

    stack = lambda j: jnp.stack([ctx_out[l][j] for l in range(depth)], axis=1)
    new_k = stack(0).reshape(B, depth, S, GQA_KV_HEADS, GQA_HEAD_DIM)
    new_v = stack(1).reshape(B, depth, S, GQA_KV_HEADS, GQA_HEAD_DIM)
    return (xp, xs, new_k, new_v, stack(2), stack(3), stack(4), stack(5))
```

```python
import functools

import numpy as np
import jax
import jax.numpy as jnp
from jax import lax
from jax.experimental import pallas as pl
from jax.experimental.pallas import tpu as pltpu

F32 = jnp.float32
BF16 = jnp.bfloat16

EPS = 1e-6
LOG2_E = 1.4426950408889634
ROPE_THETA = 10000.0
GRID_W = 64

GQA_HEADS, GQA_KV_HEADS, GQA_HEAD_DIM = 8, 2, 64
GQA_REP = GQA_HEADS // GQA_KV_HEADS
GQA_WIDTH = GQA_HEADS * GQA_HEAD_DIM
GQA_KV_WIDTH = GQA_KV_HEADS * GQA_HEAD_DIM
GLA_HEADS, GLA_DK, GLA_DV = 4, 64, 128
GLA_WIDTH = GLA_HEADS * GLA_DV
GLA_K_WIDTH = GLA_HEADS * GLA_DK
GLA_RANK = 16
GLA_NORMALIZER = 16.0
GLA_CHUNK = 64
MLA_HEADS, MLA_Q_LORA, MLA_KV_LORA = 4, 256, 256
MLA_NOPE_DIM, MLA_ROPE_DIM, MLA_V_DIM = 64, 32, 128
MLA_WIDTH = MLA_HEADS * MLA_V_DIM
MLA_QK_PAD = 128
MLA_QK_WIDTH = MLA_HEADS * MLA_QK_PAD
MLA_ROPE_LANE = MLA_NOPE_DIM

LANES = 128
SUBLANES = 8
VMEM_LIMIT = 56 * 1024 * 1024

assert GQA_KV_WIDTH == LANES and 2 * GQA_HEAD_DIM == LANES and MLA_V_DIM == LANES


def _sigmoid(x):
    return 1.0 / (1.0 + jnp.exp(-x))


def _silu(x):
    return x * _sigmoid(x)


def _log_sigmoid(x):
    return jnp.minimum(x, 0.0) - jnp.log(1.0 + jnp.exp(-jnp.abs(x)))


def _dot(a, b):
    return jnp.dot(a, b, preferred_element_type=F32)


def _dot_nt(a, b):
    return lax.dot_general(a, b, (((1,), (1,)), ((), ())), preferred_element_type=F32)


def _dot_tn(a, b):
    return lax.dot_general(a, b, (((0,), (0,)), ((), ())), preferred_element_type=F32)


def _split3(x):
    hi = x.astype(BF16)
    r1 = x - hi.astype(F32)
    mid = r1.astype(BF16)
    lo = (r1 - mid.astype(F32)).astype(BF16)
    return hi, mid, lo


def _segment_mean_sq(x, ones_bf16, width):
    sq = x * x
    hi = sq.astype(BF16)
    lo = (sq - hi.astype(F32)).astype(BF16)
    return (_dot(hi, ones_bf16) + _dot(lo, ones_bf16)) * (1.0 / width)


def _swap_halves(x, half):
    n = x.shape[-1]
    lane = lax.broadcasted_iota(jnp.int32, x.shape, x.ndim - 1)
    first = (lane % (2 * half)) < half
    return jnp.where(first, pltpu.roll(x, n - half, x.ndim - 1), pltpu.roll(x, half, x.ndim - 1))


def _rope(x, cos, sin_signed, half):
    return x * cos + _swap_halves(x, half) * sin_signed


def _tile_lanes(t, reps):
    return t if reps == 1 else jnp.concatenate([t] * reps, axis=-1)


def _rms(x, gain):
    return x * lax.rsqrt(jnp.mean(x * x, axis=-1, keepdims=True) + EPS) * gain


def _mod_kernel(c_ref, w_ref, b_ref, o_ref):
    c = c_ref[...]
    o_ref[0] = _dot(_silu(c).astype(BF16), w_ref[0].astype(BF16)) + b_ref[0]


def _modulation(conds, w_mod, b_mod):
    L, D, D3 = w_mod.shape
    nj = D3 // D
    return pl.pallas_call(
        _mod_kernel,
        grid=(L, nj),
        in_specs=[pl.BlockSpec((SUBLANES, D), lambda l, j: (0, 0)),
                  pl.BlockSpec((1, D, D), lambda l, j: (l, 0, j)),
                  pl.BlockSpec((1, 1, D), lambda l, j: (l, 0, j))],
        out_specs=pl.BlockSpec((1, SUBLANES, D), lambda l, j: (l, 0, j)),
        out_shape=jax.ShapeDtypeStruct((L, SUBLANES, D3), F32),
        compiler_params=pltpu.CompilerParams(dimension_semantics=("parallel", "parallel")),
        name="modulation",
    )(conds, w_mod, b_mod.reshape(L, 1, D3))


_N_HEAD_COLS = 2816
_T_R, _T_LAT, _T_KR, _T_GC, _T_M = 0, 128, 640, 768, 1280
_N_TAIL_COLS = 4352
_U_LAT = 2 * GLA_RANK
_U_KR = _U_LAT + MLA_Q_LORA + MLA_KV_LORA
_U_GC = _U_KR + MLA_ROPE_DIM


def _pack_w_in_kernel(w_ref, head_o, tail_o):
    w = w_ref[0]
    head_o[0] = w[:, :_N_HEAD_COLS].astype(BF16)
    t = w[:, _N_HEAD_COLS:]
    lane = lax.broadcasted_iota(jnp.int32, (w.shape[0], LANES), 1)
    tail_o[0, :, _T_R:_T_LAT] = jnp.where(lane < _U_LAT, t[:, :LANES], 0.0).astype(BF16)
    tail_o[0, :, _T_LAT:_T_KR] = t[:, _U_LAT:_U_KR].astype(BF16)
    win0 = _U_KR // LANES * LANES
    win = t[:, win0:win0 + LANES]
    lo = pltpu.roll(win, LANES - (_U_KR - win0), 1)
    hi = pltpu.roll(lo, MLA_ROPE_LANE, 1)
    kr = jnp.where(lane < MLA_ROPE_DIM, lo,
                   jnp.where((lane >= MLA_ROPE_LANE) & (lane < MLA_ROPE_LANE + MLA_ROPE_DIM), hi, 0.0))
    tail_o[0, :, _T_KR:_T_GC] = kr.astype(BF16)
    tail_o[0, :, _T_GC:] = t[:, _U_GC:].astype(BF16)


def _pack_w_in(w_in, *, tr):
    L, D, n_in = w_in.shape
    assert n_in - _N_HEAD_COLS - _U_GC == _N_TAIL_COLS - _T_GC
    return pl.pallas_call(
        _pack_w_in_kernel,
        grid=(L, D // tr),
        in_specs=[pl.BlockSpec((1, tr, n_in), lambda l, i: (l, i, 0))],
        out_specs=[pl.BlockSpec((1, tr, _N_HEAD_COLS), lambda l, i: (l, i, 0)),
                   pl.BlockSpec((1, tr, _N_TAIL_COLS), lambda l, i: (l, i, 0))],
        out_shape=[jax.ShapeDtypeStruct((L, D, _N_HEAD_COLS), BF16), jax.ShapeDtypeStruct((L, D, _N_TAIL_COLS), BF16)],
        compiler_params=pltpu.CompilerParams(dimension_semantics=("parallel", "parallel"),
                                             vmem_limit_bytes=VMEM_LIMIT),
        name="pack_w_in",
    )(w_in)
_C_QA, _C_KV, _C_GA, _C_QKG, _C_VG, _C_GG = 0, 512, 768, 1280, 1792, 2304


def _in_proj_kernel(*refs, is_ctx, d_model):
    it = iter(refs)
    x_ref, mod_ref, gpre_ref, wh_ref, wt_ref, gq_ref, gk_ref, ones_ref, wdec_ref, bdec_ref = (next(it) for _ in range(10))
    gmq_ref, gmkv_ref, wuq_ref, wk_ref, wv_ref = (next(it) for _ in range(5))
    if not is_ctx:
        ca_ref, sa_ref, cc_ref, sc_ref = (next(it) for _ in range(4))
    (qa_o, ka_o, vat_o, ga_o, qkg_o, vg_o, gg_o, la_o,
     qc_o, kc_o, vct_o, gc_o, m1_o, m2_o, m3_o) = (next(it) for _ in range(15))
    if is_ctx:
        ka32_o, va32_o, ckv32_o, kr32_o = (next(it) for _ in range(4))

    D = d_model
    mod = mod_ref[0]
    shift, scale = mod[:, :D], mod[:, D:2 * D]
    hb = (_rms(x_ref[0], gpre_ref[...]) * (1.0 + scale) + shift).astype(BF16)

    z_head = _dot(hb, wh_ref[0])
    z_tail = _dot(hb, wt_ref[0])

    def head(c0, width):
        return z_head[:, c0:c0 + width]

    def tail(c0, width):
        return z_tail[:, c0:c0 + width]

    ones = ones_ref[...]

    qa = head(_C_QA, GQA_WIDTH)
    qa = qa * lax.rsqrt(_segment_mean_sq(qa, ones, GQA_HEAD_DIM) + EPS) * gq_ref[...]
    kv = head(_C_KV, 2 * GQA_KV_WIDTH)
    ka, va = kv[:, :GQA_KV_WIDTH], kv[:, GQA_KV_WIDTH:]
    ka = ka * lax.rsqrt(_segment_mean_sq(ka, ones[:GQA_KV_WIDTH, :GQA_KV_WIDTH], GQA_HEAD_DIM) + EPS) * gk_ref[...]
    if is_ctx:
        ka32_o[0] = ka
        va32_o[0] = va
    else:
        ca, sa = ca_ref[...], sa_ref[...]
        qa = _rope(qa, _tile_lanes(ca, GQA_WIDTH // LANES), _tile_lanes(sa, GQA_WIDTH // LANES), GQA_HEAD_DIM // 4)
        ka = _rope(ka, ca, sa, GQA_HEAD_DIM // 4)
    qa_o[0] = (qa * (GQA_HEAD_DIM ** -0.5 * LOG2_E)).astype(BF16)
    ka_o[0] = ka.astype(BF16)
    vat_o[0] = va.T.astype(BF16)
    ga_o[0] = _silu(head(_C_GA, GQA_WIDTH)).astype(BF16)

    qkg = head(_C_QKG, 2 * GLA_K_WIDTH)
    lane = lax.broadcasted_iota(jnp.int32, qkg.shape, 1)
    qkg_o[0] = jnp.where(lane < GLA_K_WIDTH, qkg * GLA_DK ** -0.5, qkg)
    vg_o[0] = head(_C_VG, GLA_WIDTH)
    r = tail(_T_R, LANES).astype(BF16)
    la_o[0] = _log_sigmoid(_dot(r, wdec_ref[...]) + bdec_ref[...]) * (1.0 / GLA_NORMALIZER)
    gg_o[0] = _silu(head(_C_GG, GLA_WIDTH)).astype(BF16)

    lat = tail(_T_LAT, MLA_Q_LORA + MLA_KV_LORA)
    ql = _rms(lat[:, :MLA_Q_LORA], gmq_ref[...])
    ckv = _rms(lat[:, MLA_Q_LORA:], gmkv_ref[...])
    qc = _dot(ql.astype(BF16), wuq_ref[...])
    krb = tail(_T_KR, LANES)
    if is_ctx:
        ckv32_o[0] = ckv
        kr32_o[0] = krb[:, :MLA_ROPE_DIM]
    else:
        cc, sc = cc_ref[...], sc_ref[...]
        qc = _rope(qc, _tile_lanes(cc, MLA_HEADS), _tile_lanes(sc, MLA_HEADS), MLA_ROPE_DIM // 4)
        krb = _rope(krb, cc, sc, MLA_ROPE_DIM // 4)
    lane = lax.broadcasted_iota(jnp.int32, krb.shape, 1)
    kr_part = jnp.where((lane >= MLA_ROPE_LANE) & (lane < MLA_ROPE_LANE + MLA_ROPE_DIM), krb, 0.0)
    qc_o[0] = (qc * ((MLA_NOPE_DIM + MLA_ROPE_DIM) ** -0.5 * LOG2_E)).astype(BF16)
    ckvb = ckv.astype(BF16)
    kc_o[0] = (_dot(ckvb, wk_ref[...]) + _tile_lanes(kr_part, MLA_HEADS)).astype(BF16)
    vct_o[0] = _dot(ckvb, wv_ref[...]).T.astype(BF16)
    gc_o[0] = _silu(tail(_T_GC, MLA_WIDTH)).astype(BF16)

    m1_o[0] = _sigmoid(tail(_T_M, D)).astype(BF16)
    m2_o[0] = _sigmoid(tail(_T_M + D, D)).astype(BF16)
    m3_o[0] = _sigmoid(tail(_T_M + 2 * D, D)).astype(BF16)


def _const_spec(shape):
    nd = len(shape)
    return pl.BlockSpec(shape, lambda b, i, _nd=nd: (0,) * _nd, pipeline_mode=pl.Buffered(1))


def _in_proj(x, mod, w_packed, layer, lw, rope, *, is_ctx, tm):
    B, N, D = x.shape
    w_head, w_tail = w_packed
    per_batch_mod = mod.shape[0] > 1
    row3 = lambda w: pl.BlockSpec((1, tm, w), lambda b, i: (b, i, 0))
    col3 = lambda h: pl.BlockSpec((1, h, tm), lambda b, i: (b, 0, i))
    consts = [lw["g_q"], lw["g_k"], lw["ones"], lw["w_dec"], lw["b_dec"],
              lw["g_mla_q"], lw["g_mla_kv"], lw["w_uq"], lw["w_uk"], lw["w_uv"]]
    in_specs = [row3(D),
                pl.BlockSpec((1, 1, 3 * D), (lambda b, i: (b, 0, 0)) if per_batch_mod else (lambda b, i: (0, 0, 0))),
                _const_spec(lw["g_pre"].shape),
                pl.BlockSpec((1, D, _N_HEAD_COLS), lambda b, i: (layer, 0, 0), pipeline_mode=pl.Buffered(1)),
                pl.BlockSpec((1, D, _N_TAIL_COLS), lambda b, i: (layer, 0, 0), pipeline_mode=pl.Buffered(1))]
    in_specs += [_const_spec(c.shape) for c in consts]
    args = [x, mod, lw["g_pre"], w_head, w_tail] + consts
    if not is_ctx:
        in_specs += [pl.BlockSpec((tm, LANES), lambda b, i: (i, 0))] * 4
        args += list(rope)
    sds = jax.ShapeDtypeStruct
    out_shape = [
        sds((B, N, GQA_WIDTH), BF16), sds((B, N, GQA_KV_WIDTH), BF16), sds((B, GQA_KV_WIDTH, N), BF16),
        sds((B, N, GQA_WIDTH), BF16),
        sds((B, N, 2 * GLA_K_WIDTH), F32), sds((B, N, GLA_WIDTH), F32), sds((B, N, GLA_WIDTH), BF16),
        sds((B, N, 2 * GLA_K_WIDTH), F32),
        sds((B, N, MLA_QK_WIDTH), BF16), sds((B, N, MLA_QK_WIDTH), BF16), sds((B, MLA_WIDTH, N), BF16),
        sds((B, N, MLA_WIDTH), BF16),
        sds((B, N, D), BF16), sds((B, N, D), BF16), sds((B, N, D), BF16)]
    out_specs = [
        row3(GQA_WIDTH), row3(GQA_KV_WIDTH), col3(GQA_KV_WIDTH), row3(GQA_WIDTH),
        row3(2 * GLA_K_WIDTH), row3(GLA_WIDTH), row3(GLA_WIDTH), row3(2 * GLA_K_WIDTH),
        row3(MLA_QK_WIDTH), row3(MLA_QK_WIDTH), col3(MLA_WIDTH), row3(MLA_WIDTH),
        row3(D), row3(D), row3(D)]
    if is_ctx:
        out_shape += [sds((B, N, GQA_KV_WIDTH), F32), sds((B, N, GQA_KV_WIDTH), F32),
                      sds((B, N, MLA_KV_LORA), F32), sds((B, N, MLA_ROPE_DIM), F32)]
        out_specs += [row3(GQA_KV_WIDTH), row3(GQA_KV_WIDTH), row3(MLA_KV_LORA), row3(MLA_ROPE_DIM)]
    return pl.pallas_call(
        functools.partial(_in_proj_kernel, is_ctx=is_ctx, d_model=D),
        grid=(B, N // tm),
        in_specs=in_specs, out_specs=out_specs, out_shape=out_shape,
        compiler_params=pltpu.CompilerParams(dimension_semantics=("parallel", "parallel"),
                                             vmem_limit_bytes=VMEM_LIMIT),
        name="in_proj_ctx" if is_ctx else "in_proj_lat",
    )(*args)


def _cache_prep_kernel(gk_ref, gv_ref, ckv_ref, kr_ref, wk_ref, wv_ref, place_ref, ka_o, vat_o, kc_o, vct_o):
    ka_o[0] = gk_ref[0, 0].astype(BF16)
    vat_o[0] = gv_ref[0, 0].T.astype(BF16)
    ckvb = ckv_ref[0, 0].astype(BF16)
    kc_o[0] = (_dot(ckvb, wk_ref[...]) + _dot(kr_ref[0, 0].astype(BF16), place_ref[...])).astype(BF16)
    vct_o[0] = _dot(ckvb, wv_ref[...]).T.astype(BF16)


def _cache_prep(cache_gqa_k, cache_gqa_v, cache_mla_ckv, cache_mla_krope, layer, lw):
    B, L, P = cache_gqa_k.shape[:3]
    gk = cache_gqa_k.reshape(B, L, P, GQA_KV_WIDTH)
    gv = cache_gqa_v.reshape(B, L, P, GQA_KV_WIDTH)
    lsel = lambda w: pl.BlockSpec((1, 1, P, w), lambda b: (b, layer, 0, 0))
    full = lambda a: pl.BlockSpec(a.shape, lambda b: (0,) * a.ndim)
    sds = jax.ShapeDtypeStruct
    return pl.pallas_call(
        _cache_prep_kernel,
        grid=(B,),
        in_specs=[lsel(GQA_KV_WIDTH), lsel(GQA_KV_WIDTH), lsel(MLA_KV_LORA), lsel(MLA_ROPE_DIM),
                  full(lw["w_uk"]), full(lw["w_uv"]), full(lw["kr_place"])],
        out_specs=[pl.BlockSpec((1, P, GQA_KV_WIDTH), lambda b: (b, 0, 0)),
                   pl.BlockSpec((1, GQA_KV_WIDTH, P), lambda b: (b, 0, 0)),
                   pl.BlockSpec((1, P, MLA_QK_WIDTH), lambda b: (b, 0, 0)),
                   pl.BlockSpec((1, MLA_WIDTH, P), lambda b: (b, 0, 0))],
        out_shape=[sds((B, P, GQA_KV_WIDTH), BF16), sds((B, GQA_KV_WIDTH, P), BF16),
                   sds((B, P, MLA_QK_WIDTH), BF16), sds((B, MLA_WIDTH, P), BF16)],
        compiler_params=pltpu.CompilerParams(dimension_semantics=("parallel",)),
        name="cache_prep",
    )(gk, gv, cache_mla_ckv, cache_mla_krope, lw["w_uk"], lw["w_uv"], lw["kr_place"])


def _attention_kernel(*refs, n_src, tk, nsub, rep, shared_kv):
    q_ref, o_ref, s_scr = refs[0], refs[1 + 2 * n_src], refs[2 + 2 * n_src]
    srcs = [(refs[1 + 2 * j], refs[2 + 2 * j]) for j in range(n_src)]
    nb = q_ref.shape[0]
    tqs = q_ref.shape[1] // nsub
    g = pl.program_id(1)

    chunks = []
    off = 0
    for k_ref, vt_ref in srcs:
        n = k_ref.shape[1]
        step = min(tk, n)
        for c in range(n // step):
            chunks.append((k_ref, vt_ref, c * step, step, off))
            off += step

    def scores(j):
        b, rows = j // nsub, slice((j % nsub) * tqs, (j % nsub + 1) * tqs)
        if shared_kv:
            hd = LANES // 2
            x = q_ref[b, rows, :].astype(F32)
            lane = lax.broadcasted_iota(jnp.int32, (tqs, LANES), 1)
            pieces = []
            for r in range(rep):
                win = x[:, (r // 2) * LANES:(r // 2 + 1) * LANES]
                win = jnp.where(g == r % 2, win, pltpu.roll(win, hd, 1))
                pieces.append(jnp.where(lane // hd == g, win, 0.0).astype(BF16))
            qcat = jnp.concatenate(pieces, axis=0)
        else:
            qcat = q_ref[b, rows, :]
        W = qcat.shape[0]
        m_acc = jnp.full((SUBLANES, W), -jnp.inf, F32)
        for k_ref, _, c0, step, off in chunks:
            s = _dot_nt(k_ref[b, c0:c0 + step, :], qcat)
            s_scr[j, off:off + step, :] = s
            m_acc = jnp.maximum(m_acc, jnp.max(s.reshape(step // SUBLANES, SUBLANES, W), axis=0))
        return jnp.max(m_acc, axis=0, keepdims=True)

    def outputs(j, m):
        W = m.shape[1]
        b, rows = j // nsub, slice((j % nsub) * tqs, (j % nsub + 1) * tqs)
        l_acc = jnp.zeros((SUBLANES, W), F32)
        acc = jnp.zeros((LANES, W), F32)
        for _, vt_ref, c0, step, off in chunks:
            p = jnp.exp2(s_scr[j, off:off + step, :] - m)
            l_acc = l_acc + jnp.sum(p.reshape(step // SUBLANES, SUBLANES, W), axis=0)
            acc = acc + _dot(vt_ref[b, :, c0:c0 + step], p.astype(BF16))
        ot = acc / jnp.sum(l_acc, axis=0, keepdims=True)
        if shared_kv:
            ot = jnp.where(g == 0, ot[:LANES // 2], ot[LANES // 2:])
            for pair in range(rep // 2):
                two = jnp.concatenate([ot[:, (2 * pair) * tqs:(2 * pair + 1) * tqs],
                                       ot[:, (2 * pair + 1) * tqs:(2 * pair + 2) * tqs]], axis=0)
                o_ref[b, rows, pair * LANES:(pair + 1) * LANES] = two.T
        else:
            o_ref[b, rows, :] = ot.T

    m_prev = scores(0)
    for j in range(1, nb * nsub):
        m_next = scores(j)
        outputs(j - 1, m_prev)
        m_prev = m_next
    outputs(nb * nsub - 1, m_prev)


def _attention(q, sources, *, n_groups, shared_kv, tq, nsub, tk, nb):
    B, N, Wq = q.shape
    G = n_groups
    tqs = tq // nsub
    tq = min(tq, N)
    nsub = tq // tqs
    qw = Wq // G
    rep = qw // (LANES // 2) if shared_kv else 1
    kv_blk = (lambda b, g, i: (b, 0, 0)) if shared_kv else (lambda b, g, i: (b, 0, g))
    vt_blk = (lambda b, g, i: (b, 0, 0)) if shared_kv else (lambda b, g, i: (b, g, 0))
    in_specs = [pl.BlockSpec((nb, tq, qw), lambda b, g, i: (b, i, g))]
    args = [q]
    nk = 0
    for k, vt in sources:
        n = k.shape[1]
        nk += n
        in_specs += [pl.BlockSpec((nb, n, LANES), kv_blk), pl.BlockSpec((nb, LANES, n), vt_blk)]
        args += [k, vt]
    ow = qw if shared_kv else LANES
    return pl.pallas_call(
        functools.partial(_attention_kernel, n_src=len(sources), tk=tk, nsub=nsub, rep=rep, shared_kv=shared_kv),
        grid=(B // nb, G, N // tq),
        in_specs=in_specs,
        out_specs=pl.BlockSpec((nb, tq, ow), lambda b, g, i: (b, i, g)),
        out_shape=jax.ShapeDtypeStruct((B, N, G * ow), F32),
        scratch_shapes=[pltpu.VMEM((nb * nsub, nk, rep * tq // nsub), F32)],
        compiler_params=pltpu.CompilerParams(dimension_semantics=("parallel", "parallel", "parallel"),
                                             vmem_limit_bytes=VMEM_LIMIT),
        name="attention_gqa" if shared_kv else "attention_mla",
    )(*args)


def _gla_kernel(*refs, nchunk, nb, has_state):
    it = iter(refs)
    qkf_ref, vf_ref, gf_ref, qkb_ref, vb_ref, gb_ref = (next(it) for _ in range(6))
    if has_state:
        s0f_ref, s0b_ref = next(it), next(it)
    of_ref, ob_ref, sf_ref, sb_ref, st_scr, bd_scr = (next(it) for _ in range(6))
    C, H, DK, DV, KW, VW = GLA_CHUNK, GLA_HEADS, GLA_DK, GLA_DV, GLA_K_WIDTH, GLA_WIDTH
    i = pl.program_id(1)

    @pl.when(i == 0)
    def _():
        bd_scr[...] = jnp.zeros(bd_scr.shape, BF16)
        for d, s0_ref in enumerate((s0f_ref, s0b_ref) if has_state else (None, None)):
            for b in range(nb):
                for h in range(H):
                    s0 = s0_ref[b, 0, h] if has_state else jnp.zeros((DK, DV), F32)
                    st_scr[d, b, h] = s0
                    bd_scr[d, b, h * DK:(h + 1) * DK, h * DV:(h + 1) * DV] = s0.astype(BF16)

    row = lax.broadcasted_iota(jnp.int32, (C, H * C), 0)
    col = lax.broadcasted_iota(jnp.int32, (C, H * C), 1) % C
    keep = (row >= col, row <= col)
    tok = lax.broadcasted_iota(jnp.int32, (C, KW), 0)
    k_head = lax.broadcasted_iota(jnp.int32, (C, KW), 1) // DK
    v_head = lax.broadcasted_iota(jnp.int32, (C, VW), 1) // DV
    dirs = ((qkf_ref, vf_ref, gf_ref, of_ref, 0), (qkb_ref, vb_ref, gb_ref, ob_ref, KW))

    def chain(d, q, k, v, g, st, bd):
        cum = g
        shift = 1
        while shift < C:
            if d == 0:
                cum = cum + jnp.where(tok >= shift, pltpu.roll(cum, shift, 0), 0.0)
            else:
                cum = cum + jnp.where(tok < C - shift, pltpu.roll(cum, C - shift, 0), 0.0)
            shift *= 2
        last = cum[C - 1:C, :] if d == 0 else cum[0:1, :]
        qd = q * jnp.exp(cum)
        kd = k * jnp.exp(last - cum)
        kd_stack = jnp.concatenate([jnp.where(k_head == h, kd, 0.0).astype(BF16) for h in range(H)], axis=0)
        a = _dot_nt((qd * jnp.exp(-last)).astype(BF16), kd_stack)
        a = jnp.where(keep[d], a, 0.0).astype(BF16)
        v_stack = jnp.concatenate([jnp.where(v_head == h, v, 0.0).astype(BF16) for h in range(H)], axis=0)
        o = _dot(qd.astype(BF16), bd) + _dot(a, v_stack)
        kdt = kd.T.astype(BF16)
        decay = jnp.exp(jnp.broadcast_to(last, (LANES, KW)).T)
        vb = v.astype(BF16)
        s_new = [st[h] * decay[h * DK:(h + 1) * DK, :] + _dot(kdt[h * DK:(h + 1) * DK, :], vb[:, h * DV:(h + 1) * DV])
                 for h in range(H)]
        return o, s_new

    def body(c, carry):
        work = []
        for d, (qk_ref, v_ref, g_ref, o_ref, g0) in enumerate(dirs):
            cc = c if d == 0 else nchunk - 1 - c
            rows = pl.ds(pl.multiple_of(cc * C, C), C)
            for b in range(nb):
                work.append((d, b, o_ref, rows, qk_ref[b, rows, :KW], qk_ref[b, rows, KW:], v_ref[b, rows, :],
                             g_ref[b, rows, g0:g0 + KW], st_scr[d, b], bd_scr[d, b]))
        done = [(d, b, o_ref, rows) + chain(d, q, k, v, g, st, bd) for d, b, o_ref, rows, q, k, v, g, st, bd in work]
        for d, b, o_ref, rows, o, s_new in done:
            o_ref[b, rows, :] = o
            for h in range(H):
                st_scr[d, b, h] = s_new[h]
                bd_scr[d, b, h * DK:(h + 1) * DK, h * DV:(h + 1) * DV] = s_new[h].astype(BF16)
        return carry

    lax.fori_loop(0, nchunk, body, 0)

    @pl.when(i == pl.num_programs(1) - 1)
    def _():
        for d, s_ref in enumerate((sf_ref, sb_ref)):
            for b in range(nb):
                s_ref[b] = st_scr[d, b]


def _gla(qkg, vg, la, state_fwd, state_bwd, layer, *, tb, nb):
    B, N, _ = qkg.shape
    H, DK, DV = GLA_HEADS, GLA_DK, GLA_DV
    nblk = N // tb
    fwd = lambda w: pl.BlockSpec((nb, tb, w), lambda b, i: (b, i, 0))
    bwd = lambda w: pl.BlockSpec((nb, tb, w), lambda b, i: (b, nblk - 1 - i, 0))
    in_specs = [fwd(2 * GLA_K_WIDTH), fwd(GLA_WIDTH), fwd(2 * GLA_K_WIDTH),
                bwd(2 * GLA_K_WIDTH), bwd(GLA_WIDTH), bwd(2 * GLA_K_WIDTH)]
    args = [qkg, vg, la, qkg, vg, la]
    has_state = state_fwd is not None
    if has_state:
        s0 = pl.BlockSpec((nb, 1, H, DK, DV), lambda b, i: (b, layer, 0, 0, 0))
        in_specs += [s0, s0]
        args += [state_fwd, state_bwd]
    st = pl.BlockSpec((nb, H, DK, DV), lambda b, i: (b, 0, 0, 0))
    sds = jax.ShapeDtypeStruct
    return pl.pallas_call(
        functools.partial(_gla_kernel, nchunk=tb // GLA_CHUNK, nb=nb, has_state=has_state),
        grid=(B // nb, nblk),
        in_specs=in_specs,
        out_specs=[fwd(GLA_WIDTH), bwd(GLA_WIDTH), st, st],
        out_shape=[sds((B, N, GLA_WIDTH), F32), sds((B, N, GLA_WIDTH), F32),
                   sds((B, H, DK, DV), F32), sds((B, H, DK, DV), F32)],
        scratch_shapes=[pltpu.VMEM((2, nb, H, DK, DV), F32), pltpu.VMEM((2, nb, GLA_K_WIDTH, GLA_WIDTH), BF16)],
        compiler_params=pltpu.CompilerParams(dimension_semantics=("parallel", "arbitrary"),
                                             vmem_limit_bytes=VMEM_LIMIT),
        name="gla",
    )(*args)


def _out_proj_kernel(x_ref, mod_ref, ya_ref, ga_ref, of_ref, ob_ref, gg_ref, yc_ref, gc_ref,
                     m1_ref, m2_ref, m3_ref, woa_ref, wog_ref, woc_ref, wout_ref, ggla_ref, gpost_ref,
                     o_ref, *, d_model):
    D = d_model
    ya = _dot((ya_ref[0] * ga_ref[0]).astype(BF16), woa_ref[...])
    og = of_ref[0] + ob_ref[0]
    gg = gg_ref[0]
    heads = []
    for h in range(GLA_HEADS):
        sl = slice(h * GLA_DV, (h + 1) * GLA_DV)
        heads.append((_rms(og[:, sl], ggla_ref[...]) * gg[:, sl]).astype(BF16))
    yb = _dot(jnp.concatenate(heads, axis=-1), wog_ref[...])
    yc = _dot((yc_ref[0] * gc_ref[0]).astype(BF16), woc_ref[...])
    merged = m1_ref[0] * ya + m2_ref[0] * yb + m3_ref[0] * yc
    out = _rms(_dot(merged.astype(BF16), wout_ref[...]), gpost_ref[...])
    o_ref[0] = x_ref[0] + mod_ref[0][:, 2 * D:] * out


def _out_proj(x, mod, ya, ga, o_f, o_b, gg, yc, gc, m1, m2, m3, lw, *, tm):
    B, N, D = x.shape
    per_batch_mod = mod.shape[0] > 1
    row3 = lambda w: pl.BlockSpec((1, tm, w), lambda b, i: (b, i, 0))
    consts = [lw["w_o_gqa"], lw["w_o_gla"], lw["w_o_mla"], lw["w_out"], lw["g_gla_out"], lw["g_post"]]
    in_specs = [row3(D),
                pl.BlockSpec((1, 1, 3 * D), (lambda b, i: (b, 0, 0)) if per_batch_mod else (lambda b, i: (0, 0, 0))),
                row3(GQA_WIDTH), row3(GQA_WIDTH), row3(GLA_WIDTH), row3(GLA_WIDTH), row3(GLA_WIDTH),
                row3(MLA_WIDTH), row3(MLA_WIDTH), row3(D), row3(D), row3(D)]
    in_specs += [_const_spec(c.shape) for c in consts]
    return pl.pallas_call(
        functools.partial(_out_proj_kernel, d_model=D),
        grid=(B, N // tm),
        in_specs=in_specs,
        out_specs=row3(D),
        out_shape=jax.ShapeDtypeStruct((B, N, D), F32),
        compiler_params=pltpu.CompilerParams(dimension_semantics=("parallel", "parallel"),
                                             vmem_limit_bytes=VMEM_LIMIT),
        name="out_proj",
    )(x, mod, ya, ga, o_f, o_b, gg, yc, gc, m1, m2, m3, *consts)


def _rope_tables(n_tokens):
    t = np.arange(n_tokens)
    pos = np.stack([t // GRID_W, t % GRID_W], axis=0).astype(np.float64)

    def tables(lane_part, lane_in_part, half, active):
        freqs = ROPE_THETA ** (-(lane_in_part % half).astype(np.float64) / half)
        ang = pos[lane_part].T * freqs[None, :]
        cos = np.where(active[None, :], np.cos(ang), 1.0)
        sin = np.where(active[None, :], np.where(lane_in_part < half, -1.0, 1.0)[None, :] * np.sin(ang), 0.0)
        return jnp.asarray(cos, F32), jnp.asarray(sin, F32)

    lane = np.arange(LANES)
    ja = lane % GQA_HEAD_DIM
    ca, sa = tables(ja // (GQA_HEAD_DIM // 2), ja % (GQA_HEAD_DIM // 2), GQA_HEAD_DIM // 4, np.ones(LANES, bool))
    active = (lane >= MLA_ROPE_LANE) & (lane < MLA_ROPE_LANE + MLA_ROPE_DIM)
    jc = np.where(active, lane - MLA_ROPE_LANE, 0)
    cc, sc = tables(jc // (MLA_ROPE_DIM // 2), jc % (MLA_ROPE_DIM // 2), MLA_ROPE_DIM // 4, active)
    return ca, sa, cc, sc


def _pack_layer(l, p):
    w_dec = jnp.zeros((LANES, 2 * GLA_K_WIDTH), F32)
    w_dec = w_dec.at[:GLA_RANK, :GLA_K_WIDTH].set(p["w_gla_decay_fwd"][l])
    w_dec = w_dec.at[GLA_RANK:2 * GLA_RANK, GLA_K_WIDTH:].set(p["w_gla_decay_bwd"][l])
    b_dec = jnp.concatenate([p["b_gla_decay_fwd"][l], p["b_gla_decay_bwd"][l]])[None, :]
    pad_q = MLA_QK_PAD - MLA_NOPE_DIM - MLA_ROPE_DIM
    w_uq = p["w_mla_uq"][l].reshape(MLA_Q_LORA, MLA_HEADS, MLA_NOPE_DIM + MLA_ROPE_DIM)
    w_uq = jnp.pad(w_uq, ((0, 0), (0, 0), (0, pad_q))).reshape(MLA_Q_LORA, MLA_QK_WIDTH)
    w_ukv = p["w_mla_ukv"][l].reshape(MLA_KV_LORA, MLA_HEADS, MLA_NOPE_DIM + MLA_V_DIM)
    w_uk = jnp.pad(w_ukv[:, :, :MLA_NOPE_DIM], ((0, 0), (0, 0), (0, MLA_QK_PAD - MLA_NOPE_DIM)))
    w_uk = w_uk.reshape(MLA_KV_LORA, MLA_QK_WIDTH)
    w_uv = w_ukv[:, :, MLA_NOPE_DIM:].reshape(MLA_KV_LORA, MLA_WIDTH)
    seg = np.arange(GQA_WIDTH) // GQA_HEAD_DIM
    place = np.zeros((MLA_ROPE_DIM, MLA_QK_WIDTH), np.float32)
    for h in range(MLA_HEADS):
        place[np.arange(MLA_ROPE_DIM), h * MLA_QK_PAD + MLA_ROPE_LANE + np.arange(MLA_ROPE_DIM)] = 1.0
    return {
        "g_pre": p["g_pre"][l][None, :], "g_post": p["g_post"][l][None, :],
        "g_q": jnp.tile(p["g_q_norm"][l], GQA_HEADS)[None, :],
        "g_k": jnp.tile(p["g_k_norm"][l], GQA_KV_HEADS)[None, :],
        "ones": jnp.asarray(seg[:, None] == seg[None, :], BF16),
        "w_dec": w_dec.astype(BF16), "b_dec": b_dec,
        "g_mla_q": p["g_mla_q"][l][None, :], "g_mla_kv": p["g_mla_kv"][l][None, :],
        "w_uq": w_uq.astype(BF16), "w_uk": w_uk.astype(BF16), "w_uv": w_uv.astype(BF16),
        "kr_place": jnp.asarray(place, BF16),
        "w_o_gqa": p["w_o_gqa"][l].astype(BF16), "w_o_gla": p["w_o_gla"][l].astype(BF16),
        "w_o_mla": p["w_o_mla"][l].astype(BF16), "w_out": p["w_out"][l].astype(BF16),
        "g_gla_out": p["g_gla_out"][l][None, :],
    }


def _tiles(batch, seq):
    if seq <= 256:
        nb = 4 if batch % 4 == 0 else 1
        return dict(tm_in=seq, tm_out=seq, att_gqa=(seq, seq // 128, nb), att_mla=(seq, 1, nb), tk=512,
                    gla_tb=seq, gla_nb=2 if batch % 2 == 0 else 1)
    return dict(tm_in=512, tm_out=512, att_gqa=(512, 8, 1), att_mla=(2048, 8, 1), tk=512,
                gla_tb=512, gla_nb=2 if batch % 2 == 0 else 1)


def _sub_layer(x, mod, w_packed, layer, lw, rope, cache):
    is_ctx = cache is None
    t = _tiles(x.shape[0], x.shape[1])
    outs = _in_proj(x, mod, w_packed, layer, lw, rope, is_ctx=is_ctx, tm=t["tm_in"])
    (qa, ka, vat, ga, qkg, vg, gg, la, qc, kc, vct, gc, m1, m2, m3) = outs[:15]
    src_a, src_c = [(ka, vat)], [(kc, vct)]
    s0f = s0b = None
    if not is_ctx:
        ka_p, vat_p, kc_p, vct_p = _cache_prep(cache["gqa_k"], cache["gqa_v"], cache["mla_ckv"],
                                               cache["mla_krope"], layer, lw)
        src_a, src_c = [(ka_p, vat_p)] + src_a, [(kc_p, vct_p)] + src_c
        s0f, s0b = cache["gla_fwd"], cache["gla_bwd"]
    (tq_a, nsub_a, nb_a), (tq_c, nsub_c, nb_c) = t["att_gqa"], t["att_mla"]
    ya = _attention(qa, src_a, n_groups=GQA_KV_HEADS, shared_kv=True, tq=tq_a, nsub=nsub_a, nb=nb_a, tk=t["tk"])
    yc = _attention(qc, src_c, n_groups=MLA_HEADS, shared_kv=False, tq=tq_c, nsub=nsub_c, nb=nb_c, tk=t["tk"])
    o_f, o_b, s_f, s_b = _gla(qkg, vg, la, s0f, s0b, layer, tb=t["gla_tb"], nb=t["gla_nb"])
    y = _out_proj(x, mod, ya, ga, o_f, o_b, gg, yc, gc, m1, m2, m3, lw, tm=t["tm_out"])
    ctx = (outs[15], outs[16], outs[17], outs[18], s_f, s_b) if is_ctx else None
    return y, ctx


def kernel(x_prompt, x_sample, cache_gqa_k, cache_gqa_v, cache_mla_ckv, cache_mla_krope, state_gla_fwd, state_gla_bwd, c, c_ctx, w_mod, b_mod, g_pre, g_post, w_in, g_q_norm, g_k_norm, w_gla_decay_fwd, b_gla_decay_fwd, w_gla_decay_bwd, b_gla_decay_bwd, g_gla_out, g_mla_q, g_mla_kv, w_mla_uq, w_mla_ukv, w_o_gqa, w_o_gla, w_o_mla, w_out):
    params = dict(g_pre=g_pre, g_post=g_post, g_q_norm=g_q_norm, g_k_norm=g_k_norm,
                  w_gla_decay_fwd=w_gla_decay_fwd, b_gla_decay_fwd=b_gla_decay_fwd,
                  w_gla_decay_bwd=w_gla_decay_bwd, b_gla_decay_bwd=b_gla_decay_bwd, g_gla_out=g_gla_out,
                  g_mla_q=g_mla_q, g_mla_kv=g_mla_kv, w_mla_uq=w_mla_uq, w_mla_ukv=w_mla_ukv,
                  w_o_gqa=w_o_gqa, w_o_gla=w_o_gla, w_o_mla=w_o_mla, w_out=w_out)
    depth, D = w_in.shape[0], w_in.shape[1]
    B, S = x_prompt.shape[:2]
    Bd, Nd = x_sample.shape[:2]

    conds = jnp.concatenate([c_ctx[None, :], c, jnp.zeros((SUBLANES - 1 - Bd, D), F32)], axis=0)
    mods = _modulation(conds, w_mod, b_mod)
    rope = _rope_tables(Nd)
    w_packed = _pack_w_in(w_in, tr=128)
    layers = [_pack_layer(l, params) for l in range(depth)]

    xp = x_prompt
    ctx_out = []
    for l in range(depth):
        xp, ctx = _sub_layer(xp, mods[l, 0:1][:, None, :], w_packed, l, layers[l], None, None)
        ctx_out.append(ctx)

    xs = x_sample
    cache = {"gqa_k": cache_gqa_k, "gqa_v": cache_gqa_v, "mla_ckv": cache_mla_ckv, "mla_krope": cache_mla_krope,
             "gla_fwd": state_gla_fwd, "gla_bwd": state_gla_bwd}
    for l in range(depth):
        xs, _ = _sub_layer(xs, mods[l, 1:1 + Bd][:, None, :], w_packed, l, layers[l], rope, cache)

    stack = lambda j: jnp.stack([ctx_out[l][j] for l in range(depth)], axis=1)
    new_k = stack(0).reshape(B, depth, S, GQA_KV_HEADS, GQA_HEAD_DIM)
    new_v = stack(1).reshape(B, depth, S, GQA_KV_HEADS, GQA_HEAD_DIM)
    return (xp, xs, new_k, new_v, stack(2), stack(3), stack(4), stack(5))
```

```python
import functools

import numpy as np
import jax
import jax.numpy as jnp
from jax import lax
from jax.experimental import pallas as pl
from jax.experimental.pallas import tpu as pltpu

F32 = jnp.float32
BF16 = jnp.bfloat16

EPS = 1e-6
LOG2_E = 1.4426950408889634
ROPE_THETA = 10000.0
GRID_W = 64

GQA_HEADS, GQA_KV_HEADS, GQA_HEAD_DIM = 8, 2, 64
GQA_REP = GQA_HEADS // GQA_KV_HEADS
GQA_WIDTH = GQA_HEADS * GQA_HEAD_DIM
GQA_KV_WIDTH = GQA_KV_HEADS * GQA_HEAD_DIM
GLA_HEADS, GLA_DK, GLA_DV = 4, 64, 128
GLA_WIDTH = GLA_HEADS * GLA_DV
GLA_K_WIDTH = GLA_HEADS * GLA_DK
GLA_RANK = 16
GLA_NORMALIZER = 16.0
GLA_CHUNK = 64
MLA_HEADS, MLA_Q_LORA, MLA_KV_LORA = 4, 256, 256
MLA_NOPE_DIM, MLA_ROPE_DIM, MLA_V_DIM = 64, 32, 128
MLA_WIDTH = MLA_HEADS * MLA_V_DIM
MLA_QK_PAD = 128
MLA_QK_WIDTH = MLA_HEADS * MLA_QK_PAD
MLA_ROPE_LANE = MLA_NOPE_DIM

LANES = 128
SUBLANES = 8
SCORE_SLOTS = 3
VMEM_LIMIT = 56 * 1024 * 1024

assert GQA_KV_WIDTH == LANES and 2 * GQA_HEAD_DIM == LANES and MLA_V_DIM == LANES


def _sigmoid(x):
    return 1.0 / (1.0 + jnp.exp(-x))


def _silu(x):
    return x * _sigmoid(x)


def _log_sigmoid(x):
    return jnp.minimum(x, 0.0) - jnp.log(1.0 + jnp.exp(-jnp.abs(x)))


def _dot(a, b):
    return jnp.dot(a, b, preferred_element_type=F32)


def _dot_nt(a, b):
    return lax.dot_general(a, b, (((1,), (1,)), ((), ())), preferred_element_type=F32)


def _dot_tn(a, b):
    return lax.dot_general(a, b, (((0,), (0,)), ((), ())), preferred_element_type=F32)


def _split3(x):
    hi = x.astype(BF16)
    r1 = x - hi.astype(F32)
    mid = r1.astype(BF16)
    lo = (r1 - mid.astype(F32)).astype(BF16)
    return hi, mid, lo


def _segment_mean_sq(x, ones_bf16, width):
    sq = x * x
    hi = sq.astype(BF16)
    lo = (sq - hi.astype(F32)).astype(BF16)
    return (_dot(hi, ones_bf16) + _dot(lo, ones_bf16)) * (1.0 / width)


def _swap_halves(x, half):
    n = x.shape[-1]
    lane = lax.broadcasted_iota(jnp.int32, x.shape, x.ndim - 1)
    first = (lane % (2 * half)) < half
    return jnp.where(first, pltpu.roll(x, n - half, x.ndim - 1), pltpu.roll(x, half, x.ndim - 1))


def _rope(x, cos, sin_signed, half):
    return x * cos + _swap_halves(x, half) * sin_signed


def _tile_lanes(t, reps):
    return t if reps == 1 else jnp.concatenate([t] * reps, axis=-1)


def _rms(x, gain):
    return x * lax.rsqrt(jnp.mean(x * x, axis=-1, keepdims=True) + EPS) * gain


def _mod_kernel(c_ref, w_ref, b_ref, o_ref):
    c = c_ref[...]
    o_ref[0] = _dot(_silu(c).astype(BF16), w_ref[0].astype(BF16)) + b_ref[0]


def _modulation(conds, w_mod, b_mod):
    L, D, D3 = w_mod.shape
    nj = D3 // D
    return pl.pallas_call(
        _mod_kernel,
        grid=(L, nj),
        in_specs=[pl.BlockSpec((SUBLANES, D), lambda l, j: (0, 0)),
                  pl.BlockSpec((1, D, D), lambda l, j: (l, 0, j)),
                  pl.BlockSpec((1, 1, D), lambda l, j: (l, 0, j))],
        out_specs=pl.BlockSpec((1, SUBLANES, D), lambda l, j: (l, 0, j)),
        out_shape=jax.ShapeDtypeStruct((L, SUBLANES, D3), F32),
        compiler_params=pltpu.CompilerParams(dimension_semantics=("parallel", "parallel")),
        name="modulation",
    )(conds, w_mod, b_mod.reshape(L, 1, D3))


_R_R = 2816
_R_LAT = _R_R + 2 * GLA_RANK
_R_KR = _R_LAT + MLA_Q_LORA + MLA_KV_LORA
_R_GC = _R_KR + MLA_ROPE_DIM
_C_QA, _C_KV, _C_GA, _C_QKG, _C_VG, _C_GG = 0, 512, 768, 1280, 1792, 2304


def _in_proj_kernel(*refs, is_ctx, d_model):
    it = iter(refs)
    x_ref, mod_ref, gpre_ref, wt_ref, ws_ref, gq_ref, gk_ref, ones_ref, wdec_ref, bdec_ref = (next(it) for _ in range(10))
    gmq_ref, gmkv_ref, wuq_ref, wk_ref, wv_ref = (next(it) for _ in range(5))
    if not is_ctx:
        ca_ref, sa_ref, cc_ref, sc_ref = (next(it) for _ in range(4))
    (qa_o, ka_o, vat_o, ga_o, qkg_o, vg_o, gg_o, la_o,
     qc_o, kc_o, vct_o, gc_o, m1_o, m2_o, m3_o) = (next(it) for _ in range(15))
    if is_ctx:
        ka32_o, va32_o, ckv32_o, kr32_o = (next(it) for _ in range(4))

    D = d_model
    mod = mod_ref[0]
    shift, scale = mod[:, :D], mod[:, D:2 * D]
    hb = (_rms(x_ref[0], gpre_ref[...]) * (1.0 + scale) + shift).astype(BF16)

    z_head = _dot_nt(hb, wt_ref[0, :_R_R, :])
    z_lat = _dot_nt(hb, wt_ref[0, _R_LAT:_R_KR, :])
    z_rest = _dot_nt(hb, wt_ref[0, _R_GC:, :])
    z_small = _dot_nt(hb, ws_ref[0])

    def head(c0, width):
        return z_head[:, c0:c0 + width]

    ones = ones_ref[...]

    qa = head(_C_QA, GQA_WIDTH)
    qa = qa * lax.rsqrt(_segment_mean_sq(qa, ones, GQA_HEAD_DIM) + EPS) * gq_ref[...]
    kv = head(_C_KV, 2 * GQA_KV_WIDTH)
    ka, va = kv[:, :GQA_KV_WIDTH], kv[:, GQA_KV_WIDTH:]
    ka = ka * lax.rsqrt(_segment_mean_sq(ka, ones[:GQA_KV_WIDTH, :GQA_KV_WIDTH], GQA_HEAD_DIM) + EPS) * gk_ref[...]
    if is_ctx:
        ka32_o[0] = ka
        va32_o[0] = va
    else:
        ca, sa = ca_ref[...], sa_ref[...]
        qa = _rope(qa, _tile_lanes(ca, GQA_WIDTH // LANES), _tile_lanes(sa, GQA_WIDTH // LANES), GQA_HEAD_DIM // 4)
        ka = _rope(ka, ca, sa, GQA_HEAD_DIM // 4)
    qa_o[0] = (qa * (GQA_HEAD_DIM ** -0.5 * LOG2_E)).astype(BF16)
    ka_o[0] = ka.astype(BF16)
    vat_o[0] = va.T.astype(BF16)
    ga_o[0] = _silu(head(_C_GA, GQA_WIDTH)).astype(BF16)

    qkg = head(_C_QKG, 2 * GLA_K_WIDTH)
    lane = lax.broadcasted_iota(jnp.int32, qkg.shape, 1)
    qkg_o[0] = jnp.where(lane < GLA_K_WIDTH, qkg * GLA_DK ** -0.5, qkg)
    vg_o[0] = head(_C_VG, GLA_WIDTH)
    r = z_small[:, :LANES].astype(BF16)
    la_o[0] = _log_sigmoid(_dot(r, wdec_ref[...]) + bdec_ref[...]) * (1.0 / GLA_NORMALIZER)
    gg_o[0] = _silu(head(_C_GG, GLA_WIDTH)).astype(BF16)

    lat = z_lat
    ql = _rms(lat[:, :MLA_Q_LORA], gmq_ref[...])
    ckv = _rms(lat[:, MLA_Q_LORA:], gmkv_ref[...])
    qc = _dot(ql.astype(BF16), wuq_ref[...])
    krb = z_small[:, LANES:]
    if is_ctx:
        ckv32_o[0] = ckv
        kr32_o[0] = krb[:, :MLA_ROPE_DIM]
    else:
        cc, sc = cc_ref[...], sc_ref[...]
        qc = _rope(qc, _tile_lanes(cc, MLA_HEADS), _tile_lanes(sc, MLA_HEADS), MLA_ROPE_DIM // 4)
        krb = _rope(krb, cc, sc, MLA_ROPE_DIM // 4)
    lane = lax.broadcasted_iota(jnp.int32, krb.shape, 1)
    kr_part = jnp.where((lane >= MLA_ROPE_LANE) & (lane < MLA_ROPE_LANE + MLA_ROPE_DIM), krb, 0.0)
    qc_o[0] = (qc * ((MLA_NOPE_DIM + MLA_ROPE_DIM) ** -0.5 * LOG2_E)).astype(BF16)
    ckvb = ckv.astype(BF16)
    kc_o[0] = (_dot(ckvb, wk_ref[...]) + _tile_lanes(kr_part, MLA_HEADS)).astype(BF16)
    vct_o[0] = _dot(ckvb, wv_ref[...]).T.astype(BF16)
    gc_o[0] = _silu(z_rest[:, :MLA_WIDTH]).astype(BF16)

    m1_o[0] = _sigmoid(z_rest[:, MLA_WIDTH:MLA_WIDTH + D]).astype(BF16)
    m2_o[0] = _sigmoid(z_rest[:, MLA_WIDTH + D:MLA_WIDTH + 2 * D]).astype(BF16)
    m3_o[0] = _sigmoid(z_rest[:, MLA_WIDTH + 2 * D:]).astype(BF16)


def _const_spec(shape):
    nd = len(shape)
    return pl.BlockSpec(shape, lambda b, i, _nd=nd: (0,) * _nd, pipeline_mode=pl.Buffered(1))


def _in_proj(x, mod, w_packed, w_small, layer, lw, rope, *, is_ctx, tm):
    B, N, D = x.shape
    n_in = w_packed.shape[1]
    per_batch_mod = mod.shape[0] > 1
    row3 = lambda w: pl.BlockSpec((1, tm, w), lambda b, i: (b, i, 0))
    col3 = lambda h: pl.BlockSpec((1, h, tm), lambda b, i: (b, 0, i))
    consts = [lw["g_q"], lw["g_k"], lw["ones"], lw["w_dec"], lw["b_dec"],
              lw["g_mla_q"], lw["g_mla_kv"], lw["w_uq"], lw["w_uk"], lw["w_uv"]]
    in_specs = [row3(D),
                pl.BlockSpec((1, 1, 3 * D), (lambda b, i: (b, 0, 0)) if per_batch_mod else (lambda b, i: (0, 0, 0))),
                _const_spec(lw["g_pre"].shape),
                pl.BlockSpec((1, n_in, D), lambda b, i: (layer, 0, 0), pipeline_mode=pl.Buffered(1)),
                pl.BlockSpec((1, 2 * LANES, D), lambda b, i: (layer, 0, 0), pipeline_mode=pl.Buffered(1))]
    in_specs += [_const_spec(c.shape) for c in consts]
    args = [x, mod, lw["g_pre"], w_packed, w_small] + consts
    if not is_ctx:
        in_specs += [pl.BlockSpec((tm, LANES), lambda b, i: (i, 0))] * 4
        args += list(rope)
    sds = jax.ShapeDtypeStruct
    out_shape = [
        sds((B, N, GQA_WIDTH), BF16), sds((B, N, GQA_KV_WIDTH), BF16), sds((B, GQA_KV_WIDTH, N), BF16),
        sds((B, N, GQA_WIDTH), BF16),
        sds((B, N, 2 * GLA_K_WIDTH), F32), sds((B, N, GLA_WIDTH), F32), sds((B, N, GLA_WIDTH), BF16),
        sds((B, N, 2 * GLA_K_WIDTH), F32),
        sds((B, N, MLA_QK_WIDTH), BF16), sds((B, N, MLA_QK_WIDTH), BF16), sds((B, MLA_WIDTH, N), BF16),
        sds((B, N, MLA_WIDTH), BF16),
        sds((B, N, D), BF16), sds((B, N, D), BF16), sds((B, N, D), BF16)]
    out_specs = [
        row3(GQA_WIDTH), row3(GQA_KV_WIDTH), col3(GQA_KV_WIDTH), row3(GQA_WIDTH),
        row3(2 * GLA_K_WIDTH), row3(GLA_WIDTH), row3(GLA_WIDTH), row3(2 * GLA_K_WIDTH),
        row3(MLA_QK_WIDTH), row3(MLA_QK_WIDTH), col3(MLA_WIDTH), row3(MLA_WIDTH),
        row3(D), row3(D), row3(D)]
    if is_ctx:
        out_shape += [sds((B, N, GQA_KV_WIDTH), F32), sds((B, N, GQA_KV_WIDTH), F32),
                      sds((B, N, MLA_KV_LORA), F32), sds((B, N, MLA_ROPE_DIM), F32)]
        out_specs += [row3(GQA_KV_WIDTH), row3(GQA_KV_WIDTH), row3(MLA_KV_LORA), row3(MLA_ROPE_DIM)]
    return pl.pallas_call(
        functools.partial(_in_proj_kernel, is_ctx=is_ctx, d_model=D),
        grid=(B, N // tm),
        in_specs=in_specs, out_specs=out_specs, out_shape=out_shape,
        compiler_params=pltpu.CompilerParams(dimension_semantics=("parallel", "parallel"),
                                             vmem_limit_bytes=VMEM_LIMIT),
        name="in_proj_ctx" if is_ctx else "in_proj_lat",
    )(*args)


def _cache_prep_kernel(gk_ref, gv_ref, ckv_ref, kr_ref, wk_ref, wv_ref, place_ref, ka_o, vat_o, kc_o, vct_o):
    ka_o[0] = gk_ref[0, 0].astype(BF16)
    vat_o[0] = gv_ref[0, 0].T.astype(BF16)
    ckvb = ckv_ref[0, 0].astype(BF16)
    kc_o[0] = (_dot(ckvb, wk_ref[...]) + _dot(kr_ref[0, 0].astype(BF16), place_ref[...])).astype(BF16)
    vct_o[0] = _dot(ckvb, wv_ref[...]).T.astype(BF16)


def _cache_prep(cache_gqa_k, cache_gqa_v, cache_mla_ckv, cache_mla_krope, layer, lw):
    B, L, P = cache_gqa_k.shape[:3]
    gk = cache_gqa_k.reshape(B, L, P, GQA_KV_WIDTH)
    gv = cache_gqa_v.reshape(B, L, P, GQA_KV_WIDTH)
    lsel = lambda w: pl.BlockSpec((1, 1, P, w), lambda b: (b, layer, 0, 0))
    full = lambda a: pl.BlockSpec(a.shape, lambda b: (0,) * a.ndim)
    sds = jax.ShapeDtypeStruct
    return pl.pallas_call(
        _cache_prep_kernel,
        grid=(B,),
        in_specs=[lsel(GQA_KV_WIDTH), lsel(GQA_KV_WIDTH), lsel(MLA_KV_LORA), lsel(MLA_ROPE_DIM),
                  full(lw["w_uk"]), full(lw["w_uv"]), full(lw["kr_place"])],
        out_specs=[pl.BlockSpec((1, P, GQA_KV_WIDTH), lambda b: (b, 0, 0)),
                   pl.BlockSpec((1, GQA_KV_WIDTH, P), lambda b: (b, 0, 0)),
                   pl.BlockSpec((1, P, MLA_QK_WIDTH), lambda b: (b, 0, 0)),
                   pl.BlockSpec((1, MLA_WIDTH, P), lambda b: (b, 0, 0))],
        out_shape=[sds((B, P, GQA_KV_WIDTH), BF16), sds((B, GQA_KV_WIDTH, P), BF16),
                   sds((B, P, MLA_QK_WIDTH), BF16), sds((B, MLA_WIDTH, P), BF16)],
        compiler_params=pltpu.CompilerParams(dimension_semantics=("parallel",)),
        name="cache_prep",
    )(gk, gv, cache_mla_ckv, cache_mla_krope, lw["w_uk"], lw["w_uv"], lw["kr_place"])


def _attention_kernel(*refs, n_src, tk, nsub, rep, shared_kv):
    q_ref, o_ref, s_scr = refs[0], refs[1 + 2 * n_src], refs[2 + 2 * n_src]
    srcs = [(refs[1 + 2 * j], refs[2 + 2 * j]) for j in range(n_src)]
    nb = q_ref.shape[0]
    tqs = q_ref.shape[1] // nsub
    g = pl.program_id(1)

    chunks = []
    off = 0
    for k_ref, vt_ref in srcs:
        n = k_ref.shape[1]
        step = min(tk, n)
        for c in range(n // step):
            chunks.append((k_ref, vt_ref, c * step, step, off))
            off += step

    def scores(j):
        b, rows = j // nsub, slice((j % nsub) * tqs, (j % nsub + 1) * tqs)
        if shared_kv:
            hd = LANES // 2
            x = q_ref[b, rows, :].astype(F32)
            lane = lax.broadcasted_iota(jnp.int32, (tqs, LANES), 1)
            pieces = []
            for r in range(rep):
                win = x[:, (r // 2) * LANES:(r // 2 + 1) * LANES]
                win = jnp.where(g == r % 2, win, pltpu.roll(win, hd, 1))
                pieces.append(jnp.where(lane // hd == g, win, 0.0).astype(BF16))
            qcat = jnp.concatenate(pieces, axis=0)
        else:
            qcat = q_ref[b, rows, :]
        W = qcat.shape[0]
        m_acc = jnp.full((SUBLANES, W), -jnp.inf, F32)
        for k_ref, _, c0, step, off in chunks:
            s = _dot_nt(k_ref[b, c0:c0 + step, :], qcat)
            s_scr[j % SCORE_SLOTS, off:off + step, :] = s
            m_acc = jnp.maximum(m_acc, jnp.max(s.reshape(step // SUBLANES, SUBLANES, W), axis=0))
        return jnp.max(m_acc, axis=0, keepdims=True)

    def outputs(j, m):
        W = m.shape[1]
        b, rows = j // nsub, slice((j % nsub) * tqs, (j % nsub + 1) * tqs)
        l_acc = jnp.zeros((SUBLANES, W), F32)
        acc = jnp.zeros((LANES, W), F32)
        for _, vt_ref, c0, step, off in chunks:
            p = jnp.exp2(s_scr[j % SCORE_SLOTS, off:off + step, :] - m)
            l_acc = l_acc + jnp.sum(p.reshape(step // SUBLANES, SUBLANES, W), axis=0)
            acc = acc + _dot(vt_ref[b, :, c0:c0 + step], p.astype(BF16))
        ot = acc / jnp.sum(l_acc, axis=0, keepdims=True)
        if shared_kv:
            ot = jnp.where(g == 0, ot[:LANES // 2], ot[LANES // 2:])
            for pair in range(rep // 2):
                two = jnp.concatenate([ot[:, (2 * pair) * tqs:(2 * pair + 1) * tqs],
                                       ot[:, (2 * pair + 1) * tqs:(2 * pair + 2) * tqs]], axis=0)
                o_ref[b, rows, pair * LANES:(pair + 1) * LANES] = two.T.astype(o_ref.dtype)
        else:
            o_ref[b, rows, :] = ot.T.astype(o_ref.dtype)

    m_prev = scores(0)
    for j in range(1, nb * nsub):
        m_next = scores(j)
        outputs(j - 1, m_prev)
        m_prev = m_next
    outputs(nb * nsub - 1, m_prev)


def _attention(q, sources, *, n_groups, shared_kv, tq, nsub, tk, nb):
    B, N, Wq = q.shape
    G = n_groups
    tqs = tq // nsub
    tq = min(tq, N)
    nsub = tq // tqs
    qw = Wq // G
    rep = qw // (LANES // 2) if shared_kv else 1
    kv_blk = (lambda b, g, i: (b, 0, 0)) if shared_kv else (lambda b, g, i: (b, 0, g))
    vt_blk = (lambda b, g, i: (b, 0, 0)) if shared_kv else (lambda b, g, i: (b, g, 0))
    in_specs = [pl.BlockSpec((nb, tq, qw), lambda b, g, i: (b, i, g))]
    args = [q]
    nk = 0
    for k, vt in sources:
        n = k.shape[1]
        nk += n
        in_specs += [pl.BlockSpec((nb, n, LANES), kv_blk), pl.BlockSpec((nb, LANES, n), vt_blk)]
        args += [k, vt]
    ow = qw if shared_kv else LANES
    return pl.pallas_call(
        functools.partial(_attention_kernel, n_src=len(sources), tk=tk, nsub=nsub, rep=rep, shared_kv=shared_kv),
        grid=(B // nb, G, N // tq),
        in_specs=in_specs,
        out_specs=pl.BlockSpec((nb, tq, ow), lambda b, g, i: (b, i, g)),
        out_shape=jax.ShapeDtypeStruct((B, N, G * ow), BF16),
        scratch_shapes=[pltpu.VMEM((min(SCORE_SLOTS, nb * nsub), nk, rep * tq // nsub), F32)],
        compiler_params=pltpu.CompilerParams(dimension_semantics=("parallel", "parallel", "parallel"),
                                             vmem_limit_bytes=VMEM_LIMIT),
        name="attention_gqa" if shared_kv else "attention_mla",
    )(*args)


def _gla_kernel(*refs, nchunk, nb, has_state):
    it = iter(refs)
    qkf_ref, vf_ref, gf_ref, qkb_ref, vb_ref, gb_ref = (next(it) for _ in range(6))
    if has_state:
        s0f_ref, s0b_ref = next(it), next(it)
    of_ref, ob_ref, sf_ref, sb_ref, st_scr, bd_scr = (next(it) for _ in range(6))
    C, H, DK, DV, KW, VW = GLA_CHUNK, GLA_HEADS, GLA_DK, GLA_DV, GLA_K_WIDTH, GLA_WIDTH
    i = pl.program_id(1)

    @pl.when(i == 0)
    def _():
        bd_scr[...] = jnp.zeros(bd_scr.shape, BF16)
        for d, s0_ref in enumerate((s0f_ref, s0b_ref) if has_state else (None, None)):
            for b in range(nb):
                for h in range(H):
                    s0 = s0_ref[b, 0, h] if has_state else jnp.zeros((DK, DV), F32)
                    st_scr[d, b, h] = s0
                    bd_scr[d, b, h * DK:(h + 1) * DK, h * DV:(h + 1) * DV] = s0.astype(BF16)

    row = lax.broadcasted_iota(jnp.int32, (C, H * C), 0)
    col = lax.broadcasted_iota(jnp.int32, (C, H * C), 1) % C
    keep = (row >= col, row <= col)
    tok = lax.broadcasted_iota(jnp.int32, (C, KW), 0)
    k_head = lax.broadcasted_iota(jnp.int32, (C, KW), 1) // DK
    v_head = lax.broadcasted_iota(jnp.int32, (C, VW), 1) // DV
    dirs = ((qkf_ref, vf_ref, gf_ref, of_ref, 0), (qkb_ref, vb_ref, gb_ref, ob_ref, KW))

    def chain(d, q, k, v, g, st, bd):
        cum = g
        shift = 1
        while shift < C:
            if d == 0:
                cum = cum + jnp.where(tok >= shift, pltpu.roll(cum, shift, 0), 0.0)
            else:
                cum = cum + jnp.where(tok < C - shift, pltpu.roll(cum, C - shift, 0), 0.0)
            shift *= 2
        last = cum[C - 1:C, :] if d == 0 else cum[0:1, :]
        qd = q * jnp.exp(cum)
        kd = k * jnp.exp(last - cum)
        kd_stack = jnp.concatenate([jnp.where(k_head == h, kd, 0.0).astype(BF16) for h in range(H)], axis=0)
        a = _dot_nt((qd * jnp.exp(-last)).astype(BF16), kd_stack)
        a = jnp.where(keep[d], a, 0.0).astype(BF16)
        v_stack = jnp.concatenate([jnp.where(v_head == h, v, 0.0).astype(BF16) for h in range(H)], axis=0)
        o = _dot(qd.astype(BF16), bd) + _dot(a, v_stack)
        kdt = kd.T.astype(BF16)
        decay = jnp.exp(jnp.broadcast_to(last, (LANES, KW)).T)
        vb = v.astype(BF16)
        s_new = [st[h] * decay[h * DK:(h + 1) * DK, :] + _dot(kdt[h * DK:(h + 1) * DK, :], vb[:, h * DV:(h + 1) * DV])
                 for h in range(H)]
        return o, s_new

    def body(c, carry):
        work = []
        for d, (qk_ref, v_ref, g_ref, o_ref, g0) in enumerate(dirs):
            cc = c if d == 0 else nchunk - 1 - c
            rows = pl.ds(pl.multiple_of(cc * C, C), C)
            for b in range(nb):
                work.append((d, b, o_ref, rows, qk_ref[b, rows, :KW], qk_ref[b, rows, KW:], v_ref[b, rows, :],
                             g_ref[b, rows, g0:g0 + KW], st_scr[d, b], bd_scr[d, b]))
        done = [(d, b, o_ref, rows) + chain(d, q, k, v, g, st, bd) for d, b, o_ref, rows, q, k, v, g, st, bd in work]
        for d, b, o_ref, rows, o, s_new in done:
            o_ref[b, rows, :] = o.astype(o_ref.dtype)
            for h in range(H):
                st_scr[d, b, h] = s_new[h]
                bd_scr[d, b, h * DK:(h + 1) * DK, h * DV:(h + 1) * DV] = s_new[h].astype(BF16)
        return carry

    lax.fori_loop(0, nchunk, body, 0)

    @pl.when(i == pl.num_programs(1) - 1)
    def _():
        for d, s_ref in enumerate((sf_ref, sb_ref)):
            for b in range(nb):
                s_ref[b] = st_scr[d, b]


def _gla(qkg, vg, la, state_fwd, state_bwd, layer, *, tb, nb):
    B, N, _ = qkg.shape
    H, DK, DV = GLA_HEADS, GLA_DK, GLA_DV
    nblk = N // tb
    fwd = lambda w: pl.BlockSpec((nb, tb, w), lambda b, i: (b, i, 0))
    bwd = lambda w: pl.BlockSpec((nb, tb, w), lambda b, i: (b, nblk - 1 - i, 0))
    in_specs = [fwd(2 * GLA_K_WIDTH), fwd(GLA_WIDTH), fwd(2 * GLA_K_WIDTH),
                bwd(2 * GLA_K_WIDTH), bwd(GLA_WIDTH), bwd(2 * GLA_K_WIDTH)]
    args = [qkg, vg, la, qkg, vg, la]
    has_state = state_fwd is not None
    if has_state:
        s0 = pl.BlockSpec((nb, 1, H, DK, DV), lambda b, i: (b, layer, 0, 0, 0))
        in_specs += [s0, s0]
        args += [state_fwd, state_bwd]
    st = pl.BlockSpec((nb, H, DK, DV), lambda b, i: (b, 0, 0, 0))
    sds = jax.ShapeDtypeStruct
    return pl.pallas_call(
        functools.partial(_gla_kernel, nchunk=tb // GLA_CHUNK, nb=nb, has_state=has_state),
        grid=(B // nb, nblk),
        in_specs=in_specs,
        out_specs=[fwd(GLA_WIDTH), bwd(GLA_WIDTH), st, st],
        out_shape=[sds((B, N, GLA_WIDTH), BF16), sds((B, N, GLA_WIDTH), BF16),
                   sds((B, H, DK, DV), F32), sds((B, H, DK, DV), F32)],
        scratch_shapes=[pltpu.VMEM((2, nb, H, DK, DV), F32), pltpu.VMEM((2, nb, GLA_K_WIDTH, GLA_WIDTH), BF16)],
        compiler_params=pltpu.CompilerParams(dimension_semantics=("parallel", "arbitrary"),
                                             vmem_limit_bytes=VMEM_LIMIT),
        name="gla",
    )(*args)


def _out_proj_kernel(x_ref, mod_ref, ya_ref, ga_ref, of_ref, ob_ref, gg_ref, yc_ref, gc_ref,
                     m1_ref, m2_ref, m3_ref, woa_ref, wog_ref, woc_ref, wout_ref, ggla_ref, gpost_ref,
                     o_ref, *, d_model):
    D = d_model
    ya = _dot((ya_ref[0] * ga_ref[0]).astype(BF16), woa_ref[...])
    og = of_ref[0].astype(F32) + ob_ref[0].astype(F32)
    gg = gg_ref[0]
    heads = []
    for h in range(GLA_HEADS):
        sl = slice(h * GLA_DV, (h + 1) * GLA_DV)
        heads.append((_rms(og[:, sl], ggla_ref[...]) * gg[:, sl]).astype(BF16))
    yb = _dot(jnp.concatenate(heads, axis=-1), wog_ref[...])
    yc = _dot((yc_ref[0] * gc_ref[0]).astype(BF16), woc_ref[...])
    merged = m1_ref[0] * ya + m2_ref[0] * yb + m3_ref[0] * yc
    out = _rms(_dot(merged.astype(BF16), wout_ref[...]), gpost_ref[...])
    o_ref[0] = x_ref[0] + mod_ref[0][:, 2 * D:] * out


def _out_proj(x, mod, ya, ga, o_f, o_b, gg, yc, gc, m1, m2, m3, lw, *, tm):
    B, N, D = x.shape
    per_batch_mod = mod.shape[0] > 1
    row3 = lambda w: pl.BlockSpec((1, tm, w), lambda b, i: (b, i, 0))
    consts = [lw["w_o_gqa"], lw["w_o_gla"], lw["w_o_mla"], lw["w_out"], lw["g_gla_out"], lw["g_post"]]
    in_specs = [row3(D),
                pl.BlockSpec((1, 1, 3 * D), (lambda b, i: (b, 0, 0)) if per_batch_mod else (lambda b, i: (0, 0, 0))),
                row3(GQA_WIDTH), row3(GQA_WIDTH), row3(GLA_WIDTH), row3(GLA_WIDTH), row3(GLA_WIDTH),
                row3(MLA_WIDTH), row3(MLA_WIDTH), row3(D), row3(D), row3(D)]
    in_specs += [_const_spec(c.shape) for c in consts]
    return pl.pallas_call(
        functools.partial(_out_proj_kernel, d_model=D),
        grid=(B, N // tm),
        in_specs=in_specs,
        out_specs=row3(D),
        out_shape=jax.ShapeDtypeStruct((B, N, D), F32),
        compiler_params=pltpu.CompilerParams(dimension_semantics=("parallel", "parallel"),
                                             vmem_limit_bytes=VMEM_LIMIT),
        name="out_proj",
    )(x, mod, ya, ga, o_f, o_b, gg, yc, gc, m1, m2, m3, *consts)


def _rope_tables(n_tokens):
    t = np.arange(n_tokens)
    pos = np.stack([t // GRID_W, t % GRID_W], axis=0).astype(np.float64)

    def tables(lane_part, lane_in_part, half, active):
        freqs = ROPE_THETA ** (-(lane_in_part % half).astype(np.float64) / half)
        ang = pos[lane_part].T * freqs[None, :]
        cos = np.where(active[None, :], np.cos(ang), 1.0)
        sin = np.where(active[None, :], np.where(lane_in_part < half, -1.0, 1.0)[None, :] * np.sin(ang), 0.0)
        return jnp.asarray(cos, F32), jnp.asarray(sin, F32)

    lane = np.arange(LANES)
    ja = lane % GQA_HEAD_DIM
    ca, sa = tables(ja // (GQA_HEAD_DIM // 2), ja % (GQA_HEAD_DIM // 2), GQA_HEAD_DIM // 4, np.ones(LANES, bool))
    active = (lane >= MLA_ROPE_LANE) & (lane < MLA_ROPE_LANE + MLA_ROPE_DIM)
    jc = np.where(active, lane - MLA_ROPE_LANE, 0)
    cc, sc = tables(jc // (MLA_ROPE_DIM // 2), jc % (MLA_ROPE_DIM // 2), MLA_ROPE_DIM // 4, active)
    return ca, sa, cc, sc


def _pack_layer(l, p):
    w_dec = jnp.zeros((LANES, 2 * GLA_K_WIDTH), F32)
    w_dec = w_dec.at[:GLA_RANK, :GLA_K_WIDTH].set(p["w_gla_decay_fwd"][l])
    w_dec = w_dec.at[GLA_RANK:2 * GLA_RANK, GLA_K_WIDTH:].set(p["w_gla_decay_bwd"][l])
    b_dec = jnp.concatenate([p["b_gla_decay_fwd"][l], p["b_gla_decay_bwd"][l]])[None, :]
    pad_q = MLA_QK_PAD - MLA_NOPE_DIM - MLA_ROPE_DIM
    w_uq = p["w_mla_uq"][l].reshape(MLA_Q_LORA, MLA_HEADS, MLA_NOPE_DIM + MLA_ROPE_DIM)
    w_uq = jnp.pad(w_uq, ((0, 0), (0, 0), (0, pad_q))).reshape(MLA_Q_LORA, MLA_QK_WIDTH)
    w_ukv = p["w_mla_ukv"][l].reshape(MLA_KV_LORA, MLA_HEADS, MLA_NOPE_DIM + MLA_V_DIM)
    w_uk = jnp.pad(w_ukv[:, :, :MLA_NOPE_DIM], ((0, 0), (0, 0), (0, MLA_QK_PAD - MLA_NOPE_DIM)))
    w_uk = w_uk.reshape(MLA_KV_LORA, MLA_QK_WIDTH)
    w_uv = w_ukv[:, :, MLA_NOPE_DIM:].reshape(MLA_KV_LORA, MLA_WIDTH)
    seg = np.arange(GQA_WIDTH) // GQA_HEAD_DIM
    place = np.zeros((MLA_ROPE_DIM, MLA_QK_WIDTH), np.float32)
    for h in range(MLA_HEADS):
        place[np.arange(MLA_ROPE_DIM), h * MLA_QK_PAD + MLA_ROPE_LANE + np.arange(MLA_ROPE_DIM)] = 1.0
    return {
        "g_pre": p["g_pre"][l][None, :], "g_post": p["g_post"][l][None, :],
        "g_q": jnp.tile(p["g_q_norm"][l], GQA_HEADS)[None, :],
        "g_k": jnp.tile(p["g_k_norm"][l], GQA_KV_HEADS)[None, :],
        "ones": jnp.asarray(seg[:, None] == seg[None, :], BF16),
        "w_dec": w_dec.astype(BF16), "b_dec": b_dec,
        "g_mla_q": p["g_mla_q"][l][None, :], "g_mla_kv": p["g_mla_kv"][l][None, :],
        "w_uq": w_uq.astype(BF16), "w_uk": w_uk.astype(BF16), "w_uv": w_uv.astype(BF16),
        "kr_place": jnp.asarray(place, BF16),
        "w_o_gqa": p["w_o_gqa"][l].astype(BF16), "w_o_gla": p["w_o_gla"][l].astype(BF16),
        "w_o_mla": p["w_o_mla"][l].astype(BF16), "w_out": p["w_out"][l].astype(BF16),
        "g_gla_out": p["g_gla_out"][l][None, :],
    }


def _tiles(batch, seq):
    if seq <= 256:
        nb = 4 if batch % 4 == 0 else 1
        return dict(tm_in=seq, tm_out=seq, att_gqa=(seq, seq // 128, nb), att_mla=(seq, 1, nb), tk=512,
                    gla_tb=seq, gla_nb=2 if batch % 2 == 0 else 1)
    return dict(tm_in=512, tm_out=512, att_gqa=(1024, 16, 1), att_mla=(4096, 16, 1), tk=512,
                gla_tb=512, gla_nb=2 if batch % 2 == 0 else 1)


def _sub_layer(x, mod, w_packed, layer, lw, rope, cache):
    is_ctx = cache is None
    t = _tiles(x.shape[0], x.shape[1])
    outs = _in_proj(x, mod, w_packed[0], w_packed[1], layer, lw, rope, is_ctx=is_ctx, tm=t["tm_in"])
    (qa, ka, vat, ga, qkg, vg, gg, la, qc, kc, vct, gc, m1, m2, m3) = outs[:15]
    src_a, src_c = [(ka, vat)], [(kc, vct)]
    s0f = s0b = None
    if not is_ctx:
        ka_p, vat_p, kc_p, vct_p = _cache_prep(cache["gqa_k"], cache["gqa_v"], cache["mla_ckv"],
                                               cache["mla_krope"], layer, lw)
        src_a, src_c = [(ka_p, vat_p)] + src_a, [(kc_p, vct_p)] + src_c
        s0f, s0b = cache["gla_fwd"], cache["gla_bwd"]
    (tq_a, nsub_a, nb_a), (tq_c, nsub_c, nb_c) = t["att_gqa"], t["att_mla"]
    ya = _attention(qa, src_a, n_groups=GQA_KV_HEADS, shared_kv=True, tq=tq_a, nsub=nsub_a, nb=nb_a, tk=t["tk"])
    yc = _attention(qc, src_c, n_groups=MLA_HEADS, shared_kv=False, tq=tq_c, nsub=nsub_c, nb=nb_c, tk=t["tk"])
    o_f, o_b, s_f, s_b = _gla(qkg, vg, la, s0f, s0b, layer, tb=t["gla_tb"], nb=t["gla_nb"])
    y = _out_proj(x, mod, ya, ga, o_f, o_b, gg, yc, gc, m1, m2, m3, lw, tm=t["tm_out"])
    ctx = (outs[15], outs[16], outs[17], outs[18], s_f, s_b) if is_ctx else None
    return y, ctx


def kernel(x_prompt, x_sample, cache_gqa_k, cache_gqa_v, cache_mla_ckv, cache_mla_krope, state_gla_fwd, state_gla_bwd, c, c_ctx, w_mod, b_mod, g_pre, g_post, w_in, g_q_norm, g_k_norm, w_gla_decay_fwd, b_gla_decay_fwd, w_gla_decay_bwd, b_gla_decay_bwd, g_gla_out, g_mla_q, g_mla_kv, w_mla_uq, w_mla_ukv, w_o_gqa, w_o_gla, w_o_mla, w_out):
    params = dict(g_pre=g_pre, g_post=g_post, g_q_norm=g_q_norm, g_k_norm=g_k_norm,
                  w_gla_decay_fwd=w_gla_decay_fwd, b_gla_decay_fwd=b_gla_decay_fwd,
                  w_gla_decay_bwd=w_gla_decay_bwd, b_gla_decay_bwd=b_gla_decay_bwd, g_gla_out=g_gla_out,
                  g_mla_q=g_mla_q, g_mla_kv=g_mla_kv, w_mla_uq=w_mla_uq, w_mla_ukv=w_mla_ukv,
                  w_o_gqa=w_o_gqa, w_o_gla=w_o_gla, w_o_mla=w_o_mla, w_out=w_out)
    depth, D = w_in.shape[0], w_in.shape[1]
    B, S = x_prompt.shape[:2]
    Bd, Nd = x_sample.shape[:2]

    conds = jnp.concatenate([c_ctx[None, :], c, jnp.zeros((SUBLANES - 1 - Bd, D), F32)], axis=0)
    mods = _modulation(conds, w_mod, b_mod)
    rope = _rope_tables(Nd)
    w_t = jnp.swapaxes(w_in, 1, 2).astype(BF16)
    zrow = lambda n: jnp.zeros((depth, n, D), BF16)
    kr_rows = w_t[:, _R_KR:_R_GC]
    w_small = jnp.concatenate(
        [w_t[:, _R_R:_R_LAT], zrow(LANES - 2 * GLA_RANK),
         kr_rows, zrow(MLA_ROPE_LANE - MLA_ROPE_DIM), kr_rows, zrow(LANES - MLA_ROPE_LANE - MLA_ROPE_DIM)], axis=1)
    w_packed = (w_t, w_small)
    layers = [_pack_layer(l, params) for l in range(depth)]

    xp = x_prompt
    ctx_out = []
    for l in range(depth):
        xp, ctx = _sub_layer(xp, mods[l, 0:1][:, None, :], w_packed, l, layers[l], None, None)
        ctx_out.append(ctx)

    xs = x_sample
    cache = {"gqa_k": cache_gqa_k, "gqa_v": cache_gqa_v, "mla_ckv": cache_mla_ckv, "mla_krope": cache_mla_krope,
             "gla_fwd": state_gla_fwd, "gla_bwd": state_gla_bwd}
    for l in range(depth):
        xs, _ = _sub_layer(xs, mods[l, 1:1 + Bd][:, None, :], w_packed, l, layers[l], rope, cache)

    stack = lambda j: jnp.stack([ctx_out[l][j] for l in range(depth)], axis=1)
    new_k = stack(0).reshape(B, depth, S, GQA_KV_HEADS, GQA_HEAD_DIM)
    new_v = stack(1).reshape(B, depth, S, GQA_KV_HEADS, GQA_HEAD_DIM)
    return (xp, xs, new_k, new_v, stack(2), stack(3), stack(4), stack(5))
```

```python
import functools

import numpy as np
import jax
import jax.numpy as jnp
from jax import lax
from jax.experimental import pallas as pl
from jax.experimental.pallas import tpu as pltpu

F32 = jnp.float32
BF16 = jnp.bfloat16

EPS = 1e-6
LOG2_E = 1.4426950408889634
ROPE_THETA = 10000.0
GRID_W = 64

GQA_HEADS, GQA_KV_HEADS, GQA_HEAD_DIM = 8, 2, 64
GQA_REP = GQA_HEADS // GQA_KV_HEADS
GQA_WIDTH = GQA_HEADS * GQA_HEAD_DIM
GQA_KV_WIDTH = GQA_KV_HEADS * GQA_HEAD_DIM
GLA_HEADS, GLA_DK, GLA_DV = 4, 64, 128
GLA_WIDTH = GLA_HEADS * GLA_DV
GLA_K_WIDTH = GLA_HEADS * GLA_DK
GLA_RANK = 16
GLA_NORMALIZER = 16.0
GLA_CHUNK = 64
MLA_HEADS, MLA_Q_LORA, MLA_KV_LORA = 4, 256, 256
MLA_NOPE_DIM, MLA_ROPE_DIM, MLA_V_DIM = 64, 32, 128
MLA_WIDTH = MLA_HEADS * MLA_V_DIM
MLA_QK_PAD = 128
MLA_QK_WIDTH = MLA_HEADS * MLA_QK_PAD
MLA_ROPE_LANE = MLA_NOPE_DIM

LANES = 128
SUBLANES = 8
SCORE_SLOTS = 3
VMEM_LIMIT = 56 * 1024 * 1024

assert GQA_KV_WIDTH == LANES and 2 * GQA_HEAD_DIM == LANES and MLA_V_DIM == LANES


def _sigmoid(x):
    return 1.0 / (1.0 + jnp.exp(-x))


def _silu(x):
    return x * _sigmoid(x)


def _log_sigmoid(x):
    return jnp.minimum(x, 0.0) - jnp.log(1.0 + jnp.exp(-jnp.abs(x)))


def _dot(a, b):
    return jnp.dot(a, b, preferred_element_type=F32)


def _dot_nt(a, b):
    return lax.dot_general(a, b, (((1,), (1,)), ((), ())), preferred_element_type=F32)


def _dot_tn(a, b):
    return lax.dot_general(a, b, (((0,), (0,)), ((), ())), preferred_element_type=F32)


def _split3(x):
    hi = x.astype(BF16)
    r1 = x - hi.astype(F32)
    mid = r1.astype(BF16)
    lo = (r1 - mid.astype(F32)).astype(BF16)
    return hi, mid, lo


def _segment_mean_sq(x, ones_bf16, width):
    sq = x * x
    hi = sq.astype(BF16)
    lo = (sq - hi.astype(F32)).astype(BF16)
    return (_dot(hi, ones_bf16) + _dot(lo, ones_bf16)) * (1.0 / width)


def _swap_halves(x, half):
    n = x.shape[-1]
    lane = lax.broadcasted_iota(jnp.int32, x.shape, x.ndim - 1)
    first = (lane % (2 * half)) < half
    return jnp.where(first, pltpu.roll(x, n - half, x.ndim - 1), pltpu.roll(x, half, x.ndim - 1))


def _rope(x, cos, sin_signed, half):
    return x * cos + _swap_halves(x, half) * sin_signed


def _tile_lanes(t, reps):
    return t if reps == 1 else jnp.concatenate([t] * reps, axis=-1)


def _rms(x, gain):
    return x * lax.rsqrt(jnp.mean(x * x, axis=-1, keepdims=True) + EPS) * gain


def _mod_kernel(c_ref, w_ref, b_ref, o_ref):
    c = c_ref[...]
    o_ref[0] = _dot(_silu(c).astype(BF16), w_ref[0].astype(BF16)) + b_ref[0]


def _modulation(conds, w_mod, b_mod):
    L, D, D3 = w_mod.shape
    nj = D3 // D
    return pl.pallas_call(
        _mod_kernel,
        grid=(L, nj),
        in_specs=[pl.BlockSpec((SUBLANES, D), lambda l, j: (0, 0)),
                  pl.BlockSpec((1, D, D), lambda l, j: (l, 0, j)),
                  pl.BlockSpec((1, 1, D), lambda l, j: (l, 0, j))],
        out_specs=pl.BlockSpec((1, SUBLANES, D), lambda l, j: (l, 0, j)),
        out_shape=jax.ShapeDtypeStruct((L, SUBLANES, D3), F32),
        compiler_params=pltpu.CompilerParams(dimension_semantics=("parallel", "parallel")),
        name="modulation",
    )(conds, w_mod, b_mod.reshape(L, 1, D3))


_R_R = 2816
_R_LAT = _R_R + 2 * GLA_RANK
_R_KR = _R_LAT + MLA_Q_LORA + MLA_KV_LORA
_R_GC = _R_KR + MLA_ROPE_DIM
_R_M = _R_GC + MLA_WIDTH
_C_QA, _C_KV, _C_GA, _C_QKG, _C_VG, _C_GG = 0, 512, 768, 1280, 1792, 2304


def _modulated_input(x, mod, g_pre, d_model):
    shift, scale = mod[:, :d_model], mod[:, d_model:2 * d_model]
    return (_rms(x, g_pre) * (1.0 + scale) + shift).astype(BF16)


def _in_proj_kernel(*refs, is_ctx, d_model):
    it = iter(refs)
    x_ref, mod_ref, gpre_ref, wt_ref, ws_ref, gq_ref, gk_ref, ones_ref, wdec_ref, bdec_ref = (next(it) for _ in range(10))
    gmq_ref, gmkv_ref, wuq_ref, wk_ref, wv_ref = (next(it) for _ in range(5))
    if not is_ctx:
        ca_ref, sa_ref, cc_ref, sc_ref = (next(it) for _ in range(4))
    (qa_o, ka_o, vat_o, ga_o, qkg_o, vg_o, gg_o, la_o,
     qc_o, kc_o, vct_o, gc_o) = (next(it) for _ in range(12))
    if is_ctx:
        ka32_o, va32_o, ckv32_o, kr32_o = (next(it) for _ in range(4))

    D = d_model
    hb = _modulated_input(x_ref[0], mod_ref[0], gpre_ref[...], D)

    z_head = _dot_nt(hb, wt_ref[0, :_R_R, :])
    z_lat = _dot_nt(hb, wt_ref[0, _R_LAT:_R_KR, :])
    z_gc = _dot_nt(hb, wt_ref[0, _R_GC:_R_M, :])
    z_small = _dot_nt(hb, ws_ref[0])

    def head(c0, width):
        return z_head[:, c0:c0 + width]

    ones = ones_ref[...]

    qa = head(_C_QA, GQA_WIDTH)
    qa = qa * lax.rsqrt(_segment_mean_sq(qa, ones, GQA_HEAD_DIM) + EPS) * gq_ref[...]
    kv = head(_C_KV, 2 * GQA_KV_WIDTH)
    ka, va = kv[:, :GQA_KV_WIDTH], kv[:, GQA_KV_WIDTH:]
    ka = ka * lax.rsqrt(_segment_mean_sq(ka, ones[:GQA_KV_WIDTH, :GQA_KV_WIDTH], GQA_HEAD_DIM) + EPS) * gk_ref[...]
    if is_ctx:
        ka32_o[0] = ka
        va32_o[0] = va
    else:
        ca, sa = ca_ref[...], sa_ref[...]
        qa = _rope(qa, _tile_lanes(ca, GQA_WIDTH // LANES), _tile_lanes(sa, GQA_WIDTH // LANES), GQA_HEAD_DIM // 4)
        ka = _rope(ka, ca, sa, GQA_HEAD_DIM // 4)
    qa_o[0] = (qa * (GQA_HEAD_DIM ** -0.5 * LOG2_E)).astype(BF16)
    ka_o[0] = ka.astype(BF16)
    vat_o[0] = va.T.astype(BF16)
    ga_o[0] = _silu(head(_C_GA, GQA_WIDTH)).astype(BF16)

    qkg = head(_C_QKG, 2 * GLA_K_WIDTH)
    lane = lax.broadcasted_iota(jnp.int32, qkg.shape, 1)
    qkg_o[0] = jnp.where(lane < GLA_K_WIDTH, qkg * GLA_DK ** -0.5, qkg)
    vg_o[0] = head(_C_VG, GLA_WIDTH)
    r = z_small[:, :LANES].astype(BF16)
    la_o[0] = _log_sigmoid(_dot(r, wdec_ref[...]) + bdec_ref[...]) * (1.0 / GLA_NORMALIZER)
    gg_o[0] = _silu(head(_C_GG, GLA_WIDTH)).astype(BF16)

    lat = z_lat
    ql = _rms(lat[:, :MLA_Q_LORA], gmq_ref[...])
    ckv = _rms(lat[:, MLA_Q_LORA:], gmkv_ref[...])
    qc = _dot(ql.astype(BF16), wuq_ref[...])
    krb = z_small[:, LANES:]
    if is_ctx:
        ckv32_o[0] = ckv
        kr32_o[0] = krb[:, :MLA_ROPE_DIM]
    else:
        cc, sc = cc_ref[...], sc_ref[...]
        qc = _rope(qc, _tile_lanes(cc, MLA_HEADS), _tile_lanes(sc, MLA_HEADS), MLA_ROPE_DIM // 4)
        krb = _rope(krb, cc, sc, MLA_ROPE_DIM // 4)
    lane = lax.broadcasted_iota(jnp.int32, krb.shape, 1)
    kr_part = jnp.where((lane >= MLA_ROPE_LANE) & (lane < MLA_ROPE_LANE + MLA_ROPE_DIM), krb, 0.0)
    qc_o[0] = (qc * ((MLA_NOPE_DIM + MLA_ROPE_DIM) ** -0.5 * LOG2_E)).astype(BF16)
    ckvb = ckv.astype(BF16)
    kc_o[0] = (_dot(ckvb, wk_ref[...]) + _tile_lanes(kr_part, MLA_HEADS)).astype(BF16)
    vct_o[0] = _dot(ckvb, wv_ref[...]).T.astype(BF16)
    gc_o[0] = _silu(z_gc).astype(BF16)


def _const_spec(shape):
    nd = len(shape)
    return pl.BlockSpec(shape, lambda b, i, _nd=nd: (0,) * _nd, pipeline_mode=pl.Buffered(1))


def _in_proj(x, mod, w_packed, w_small, layer, lw, rope, *, is_ctx, tm):
    B, N, D = x.shape
    assert w_packed.shape[1] == _R_M + 3 * D
    per_batch_mod = mod.shape[0] > 1
    row3 = lambda w: pl.BlockSpec((1, tm, w), lambda b, i: (b, i, 0))
    col3 = lambda h: pl.BlockSpec((1, h, tm), lambda b, i: (b, 0, i))
    consts = [lw["g_q"], lw["g_k"], lw["ones"], lw["w_dec"], lw["b_dec"],
              lw["g_mla_q"], lw["g_mla_kv"], lw["w_uq"], lw["w_uk"], lw["w_uv"]]
    in_specs = [row3(D),
                pl.BlockSpec((1, 1, 3 * D), (lambda b, i: (b, 0, 0)) if per_batch_mod else (lambda b, i: (0, 0, 0))),
                _const_spec(lw["g_pre"].shape),
                pl.BlockSpec((1, _R_M, D), lambda b, i: (layer, 0, 0), pipeline_mode=pl.Buffered(1)),
                pl.BlockSpec((1, 2 * LANES, D), lambda b, i: (layer, 0, 0), pipeline_mode=pl.Buffered(1))]
    in_specs += [_const_spec(c.shape) for c in consts]
    args = [x, mod, lw["g_pre"], w_packed, w_small] + consts
    if not is_ctx:
        in_specs += [pl.BlockSpec((tm, LANES), lambda b, i: (i, 0))] * 4
        args += list(rope)
    sds = jax.ShapeDtypeStruct
    out_shape = [
        sds((B, N, GQA_WIDTH), BF16), sds((B, N, GQA_KV_WIDTH), BF16), sds((B, GQA_KV_WIDTH, N), BF16),
        sds((B, N, GQA_WIDTH), BF16),
        sds((B, N, 2 * GLA_K_WIDTH), F32), sds((B, N, GLA_WIDTH), F32), sds((B, N, GLA_WIDTH), BF16),
        sds((B, N, 2 * GLA_K_WIDTH), F32),
        sds((B, N, MLA_QK_WIDTH), BF16), sds((B, N, MLA_QK_WIDTH), BF16), sds((B, MLA_WIDTH, N), BF16),
        sds((B, N, MLA_WIDTH), BF16)]
    out_specs = [
        row3(GQA_WIDTH), row3(GQA_KV_WIDTH), col3(GQA_KV_WIDTH), row3(GQA_WIDTH),
        row3(2 * GLA_K_WIDTH), row3(GLA_WIDTH), row3(GLA_WIDTH), row3(2 * GLA_K_WIDTH),
        row3(MLA_QK_WIDTH), row3(MLA_QK_WIDTH), col3(MLA_WIDTH), row3(MLA_WIDTH)]
    if is_ctx:
        out_shape += [sds((B, N, GQA_KV_WIDTH), F32), sds((B, N, GQA_KV_WIDTH), F32),
                      sds((B, N, MLA_KV_LORA), F32), sds((B, N, MLA_ROPE_DIM), F32)]
        out_specs += [row3(GQA_KV_WIDTH), row3(GQA_KV_WIDTH), row3(MLA_KV_LORA), row3(MLA_ROPE_DIM)]
    return pl.pallas_call(
        functools.partial(_in_proj_kernel, is_ctx=is_ctx, d_model=D),
        grid=(B, N // tm),
        in_specs=in_specs, out_specs=out_specs, out_shape=out_shape,
        compiler_params=pltpu.CompilerParams(dimension_semantics=("parallel", "parallel"),
                                             vmem_limit_bytes=VMEM_LIMIT),
        name="in_proj_ctx" if is_ctx else "in_proj_lat",
    )(*args)


def _cache_prep_kernel(gk_ref, gv_ref, ckv_ref, kr_ref, wk_ref, wv_ref, place_ref, ka_o, vat_o, kc_o, vct_o):
    ka_o[0] = gk_ref[0, 0].astype(BF16)
    vat_o[0] = gv_ref[0, 0].T.astype(BF16)
    ckvb = ckv_ref[0, 0].astype(BF16)
    kc_o[0] = (_dot(ckvb, wk_ref[...]) + _dot(kr_ref[0, 0].astype(BF16), place_ref[...])).astype(BF16)
    vct_o[0] = _dot(ckvb, wv_ref[...]).T.astype(BF16)


def _cache_prep(cache_gqa_k, cache_gqa_v, cache_mla_ckv, cache_mla_krope, layer, lw):
    B, L, P = cache_gqa_k.shape[:3]
    gk = cache_gqa_k.reshape(B, L, P, GQA_KV_WIDTH)
    gv = cache_gqa_v.reshape(B, L, P, GQA_KV_WIDTH)
    lsel = lambda w: pl.BlockSpec((1, 1, P, w), lambda b: (b, layer, 0, 0))
    full = lambda a: pl.BlockSpec(a.shape, lambda b: (0,) * a.ndim)
    sds = jax.ShapeDtypeStruct
    return pl.pallas_call(
        _cache_prep_kernel,
        grid=(B,),
        in_specs=[lsel(GQA_KV_WIDTH), lsel(GQA_KV_WIDTH), lsel(MLA_KV_LORA), lsel(MLA_ROPE_DIM),
                  full(lw["w_uk"]), full(lw["w_uv"]), full(lw["kr_place"])],
        out_specs=[pl.BlockSpec((1, P, GQA_KV_WIDTH), lambda b: (b, 0, 0)),
                   pl.BlockSpec((1, GQA_KV_WIDTH, P), lambda b: (b, 0, 0)),
                   pl.BlockSpec((1, P, MLA_QK_WIDTH), lambda b: (b, 0, 0)),
                   pl.BlockSpec((1, MLA_WIDTH, P), lambda b: (b, 0, 0))],
        out_shape=[sds((B, P, GQA_KV_WIDTH), BF16), sds((B, GQA_KV_WIDTH, P), BF16),
                   sds((B, P, MLA_QK_WIDTH), BF16), sds((B, MLA_WIDTH, P), BF16)],
        compiler_params=pltpu.CompilerParams(dimension_semantics=("parallel",)),
        name="cache_prep",
    )(gk, gv, cache_mla_ckv, cache_mla_krope, lw["w_uk"], lw["w_uv"], lw["kr_place"])


def _attention_kernel(*refs, n_src, tk, nsub, rep, shared_kv):
    q_ref, o_ref, s_scr = refs[0], refs[1 + 2 * n_src], refs[2 + 2 * n_src]
    srcs = [(refs[1 + 2 * j], refs[2 + 2 * j]) for j in range(n_src)]
    nb = q_ref.shape[0]
    tqs = q_ref.shape[1] // nsub
    g = pl.program_id(1)

    chunks = []
    off = 0
    for k_ref, vt_ref in srcs:
        n = k_ref.shape[1]
        step = min(tk, n)
        for c in range(n // step):
            chunks.append((k_ref, vt_ref, c * step, step, off))
            off += step

    def scores(j):
        b, rows = j // nsub, slice((j % nsub) * tqs, (j % nsub + 1) * tqs)
        if shared_kv:
            hd = LANES // 2
            x = q_ref[b, rows, :].astype(F32)
            lane = lax.broadcasted_iota(jnp.int32, (tqs, LANES), 1)
            pieces = []
            for r in range(rep):
                win = x[:, (r // 2) * LANES:(r // 2 + 1) * LANES]
                win = jnp.where(g == r % 2, win, pltpu.roll(win, hd, 1))
                pieces.append(jnp.where(lane // hd == g, win, 0.0).astype(BF16))
            qcat = jnp.concatenate(pieces, axis=0)
        else:
            qcat = q_ref[b, rows, :]
        W = qcat.shape[0]
        m_acc = jnp.full((SUBLANES, W), -jnp.inf, F32)
        for k_ref, _, c0, step, off in chunks:
            s = _dot_nt(k_ref[b, c0:c0 + step, :], qcat)
            s_scr[j % SCORE_SLOTS, off:off + step, :] = s
            m_acc = jnp.maximum(m_acc, jnp.max(s.reshape(step // SUBLANES, SUBLANES, W), axis=0))
        return jnp.max(m_acc, axis=0, keepdims=True)

    def outputs(j, m):
        W = m.shape[1]
        b, rows = j // nsub, slice((j % nsub) * tqs, (j % nsub + 1) * tqs)
        l_acc = jnp.zeros((SUBLANES, W), F32)
        acc = jnp.zeros((LANES, W), F32)
        for _, vt_ref, c0, step, off in chunks:
            p = jnp.exp2(s_scr[j % SCORE_SLOTS, off:off + step, :] - m)
            l_acc = l_acc + jnp.sum(p.reshape(step // SUBLANES, SUBLANES, W), axis=0)
            acc = acc + _dot(vt_ref[b, :, c0:c0 + step], p.astype(BF16))
        ot = acc / jnp.sum(l_acc, axis=0, keepdims=True)
        if shared_kv:
            ot = jnp.where(g == 0, ot[:LANES // 2], ot[LANES // 2:])
            for pair in range(rep // 2):
                two = jnp.concatenate([ot[:, (2 * pair) * tqs:(2 * pair + 1) * tqs],
                                       ot[:, (2 * pair + 1) * tqs:(2 * pair + 2) * tqs]], axis=0)
                o_ref[b, rows, pair * LANES:(pair + 1) * LANES] = two.T.astype(o_ref.dtype)
        else:
            o_ref[b, rows, :] = ot.T.astype(o_ref.dtype)

    m_prev = scores(0)
    for j in range(1, nb * nsub):
        m_next = scores(j)
        outputs(j - 1, m_prev)
        m_prev = m_next
    outputs(nb * nsub - 1, m_prev)


def _attention(q, sources, *, n_groups, shared_kv, tq, nsub, tk, nb):
    B, N, Wq = q.shape
    G = n_groups
    tqs = tq // nsub
    tq = min(tq, N)
    nsub = tq // tqs
    qw = Wq // G
    rep = qw // (LANES // 2) if shared_kv else 1
    kv_blk = (lambda b, g, i: (b, 0, 0)) if shared_kv else (lambda b, g, i: (b, 0, g))
    vt_blk = (lambda b, g, i: (b, 0, 0)) if shared_kv else (lambda b, g, i: (b, g, 0))
    in_specs = [pl.BlockSpec((nb, tq, qw), lambda b, g, i: (b, i, g))]
    args = [q]
    nk = 0
    for k, vt in sources:
        n = k.shape[1]
        nk += n
        in_specs += [pl.BlockSpec((nb, n, LANES), kv_blk), pl.BlockSpec((nb, LANES, n), vt_blk)]
        args += [k, vt]
    ow = qw if shared_kv else LANES
    return pl.pallas_call(
        functools.partial(_attention_kernel, n_src=len(sources), tk=tk, nsub=nsub, rep=rep, shared_kv=shared_kv),
        grid=(B // nb, G, N // tq),
        in_specs=in_specs,
        out_specs=pl.BlockSpec((nb, tq, ow), lambda b, g, i: (b, i, g)),
        out_shape=jax.ShapeDtypeStruct((B, N, G * ow), BF16),
        scratch_shapes=[pltpu.VMEM((min(SCORE_SLOTS, nb * nsub), nk, rep * tq // nsub), F32)],
        compiler_params=pltpu.CompilerParams(dimension_semantics=("parallel", "parallel", "parallel"),
                                             vmem_limit_bytes=VMEM_LIMIT),
        name="attention_gqa" if shared_kv else "attention_mla",
    )(*args)


def _gla_kernel(*refs, nchunk, nb, has_state):
    it = iter(refs)
    qkf_ref, vf_ref, gf_ref, qkb_ref, vb_ref, gb_ref = (next(it) for _ in range(6))
    if has_state:
        s0f_ref, s0b_ref = next(it), next(it)
    of_ref, ob_ref, sf_ref, sb_ref, st_scr, bd_scr = (next(it) for _ in range(6))
    C, H, DK, DV, KW, VW = GLA_CHUNK, GLA_HEADS, GLA_DK, GLA_DV, GLA_K_WIDTH, GLA_WIDTH
    i = pl.program_id(1)

    @pl.when(i == 0)
    def _():
        bd_scr[...] = jnp.zeros(bd_scr.shape, BF16)
        for d, s0_ref in enumerate((s0f_ref, s0b_ref) if has_state else (None, None)):
            for b in range(nb):
                for h in range(H):
                    s0 = s0_ref[b, 0, h] if has_state else jnp.zeros((DK, DV), F32)
                    st_scr[d, b, h] = s0
                    bd_scr[d, b, h * DK:(h + 1) * DK, h * DV:(h + 1) * DV] = s0.astype(BF16)

    row = lax.broadcasted_iota(jnp.int32, (C, H * C), 0)
    col = lax.broadcasted_iota(jnp.int32, (C, H * C), 1) % C
    keep = (row >= col, row <= col)
    tok = lax.broadcasted_iota(jnp.int32, (C, KW), 0)
    k_head = lax.broadcasted_iota(jnp.int32, (C, KW), 1) // DK
    v_head = lax.broadcasted_iota(jnp.int32, (C, VW), 1) // DV
    dirs = ((qkf_ref, vf_ref, gf_ref, of_ref, 0), (qkb_ref, vb_ref, gb_ref, ob_ref, KW))

    def chain(d, q, k, v, g, st, bd):
        cum = g
        shift = 1
        while shift < C:
            if d == 0:
                cum = cum + jnp.where(tok >= shift, pltpu.roll(cum, shift, 0), 0.0)
            else:
                cum = cum + jnp.where(tok < C - shift, pltpu.roll(cum, C - shift, 0), 0.0)
            shift *= 2
        last = cum[C - 1:C, :] if d == 0 else cum[0:1, :]
        qd = q * jnp.exp(cum)
        kd = k * jnp.exp(last - cum)
        kd_stack = jnp.concatenate([jnp.where(k_head == h, kd, 0.0).astype(BF16) for h in range(H)], axis=0)
        a = _dot_nt((qd * jnp.exp(-last)).astype(BF16), kd_stack)
        a = jnp.where(keep[d], a, 0.0).astype(BF16)
        v_stack = jnp.concatenate([jnp.where(v_head == h, v, 0.0).astype(BF16) for h in range(H)], axis=0)
        o = _dot(qd.astype(BF16), bd) + _dot(a, v_stack)
        kdt = kd.T.astype(BF16)
        decay = jnp.exp(jnp.broadcast_to(last, (LANES, KW)).T)
        vb = v.astype(BF16)
        s_new = [st[h] * decay[h * DK:(h + 1) * DK, :] + _dot(kdt[h * DK:(h + 1) * DK, :], vb[:, h * DV:(h + 1) * DV])
                 for h in range(H)]
        return o, s_new

    def body(c, carry):
        work = []
        for d, (qk_ref, v_ref, g_ref, o_ref, g0) in enumerate(dirs):
            cc = c if d == 0 else nchunk - 1 - c
            rows = pl.ds(pl.multiple_of(cc * C, C), C)
            for b in range(nb):
                work.append((d, b, o_ref, rows, qk_ref[b, rows, :KW], qk_ref[b, rows, KW:], v_ref[b, rows, :],
                             g_ref[b, rows, g0:g0 + KW], st_scr[d, b], bd_scr[d, b]))
        done = [(d, b, o_ref, rows) + chain(d, q, k, v, g, st, bd) for d, b, o_ref, rows, q, k, v, g, st, bd in work]
        for d, b, o_ref, rows, o, s_new in done:
            o_ref[b, rows, :] = o.astype(o_ref.dtype)
            for h in range(H):
                st_scr[d, b, h] = s_new[h]
                bd_scr[d, b, h * DK:(h + 1) * DK, h * DV:(h + 1) * DV] = s_new[h].astype(BF16)
        return carry

    lax.fori_loop(0, nchunk, body, 0)

    @pl.when(i == pl.num_programs(1) - 1)
    def _():
        for d, s_ref in enumerate((sf_ref, sb_ref)):
            for b in range(nb):
                s_ref[b] = st_scr[d, b]


def _gla(qkg, vg, la, state_fwd, state_bwd, layer, *, tb, nb):
    B, N, _ = qkg.shape
    H, DK, DV = GLA_HEADS, GLA_DK, GLA_DV
    nblk = N // tb
    fwd = lambda w: pl.BlockSpec((nb, tb, w), lambda b, i: (b, i, 0))
    bwd = lambda w: pl.BlockSpec((nb, tb, w), lambda b, i: (b, nblk - 1 - i, 0))
    in_specs = [fwd(2 * GLA_K_WIDTH), fwd(GLA_WIDTH), fwd(2 * GLA_K_WIDTH),
                bwd(2 * GLA_K_WIDTH), bwd(GLA_WIDTH), bwd(2 * GLA_K_WIDTH)]
    args = [qkg, vg, la, qkg, vg, la]
    has_state = state_fwd is not None
    if has_state:
        s0 = pl.BlockSpec((nb, 1, H, DK, DV), lambda b, i: (b, layer, 0, 0, 0))
        in_specs += [s0, s0]
        args += [state_fwd, state_bwd]
    st = pl.BlockSpec((nb, H, DK, DV), lambda b, i: (b, 0, 0, 0))
    sds = jax.ShapeDtypeStruct
    return pl.pallas_call(
        functools.partial(_gla_kernel, nchunk=tb // GLA_CHUNK, nb=nb, has_state=has_state),
        grid=(B // nb, nblk),
        in_specs=in_specs,
        out_specs=[fwd(GLA_WIDTH), bwd(GLA_WIDTH), st, st],
        out_shape=[sds((B, N, GLA_WIDTH), BF16), sds((B, N, GLA_WIDTH), BF16),
                   sds((B, H, DK, DV), F32), sds((B, H, DK, DV), F32)],
        scratch_shapes=[pltpu.VMEM((2, nb, H, DK, DV), F32), pltpu.VMEM((2, nb, GLA_K_WIDTH, GLA_WIDTH), BF16)],
        compiler_params=pltpu.CompilerParams(dimension_semantics=("parallel", "arbitrary"),
                                             vmem_limit_bytes=VMEM_LIMIT),
        name="gla",
    )(*args)


def _out_proj_kernel(x_ref, mod_ref, ya_ref, ga_ref, of_ref, ob_ref, gg_ref, yc_ref, gc_ref,
                     gpre_ref, wm_ref, woa_ref, wog_ref, woc_ref, wout_ref, ggla_ref, gpost_ref,
                     o_ref, *, d_model):
    D = d_model
    zm = _dot_nt(_modulated_input(x_ref[0], mod_ref[0], gpre_ref[...], D), wm_ref[0])
    ya = _dot((ya_ref[0] * ga_ref[0]).astype(BF16), woa_ref[...])
    og = of_ref[0].astype(F32) + ob_ref[0].astype(F32)
    gg = gg_ref[0]
    heads = []
    for h in range(GLA_HEADS):
        sl = slice(h * GLA_DV, (h + 1) * GLA_DV)
        heads.append((_rms(og[:, sl], ggla_ref[...]) * gg[:, sl]).astype(BF16))
    yb = _dot(jnp.concatenate(heads, axis=-1), wog_ref[...])
    yc = _dot((yc_ref[0] * gc_ref[0]).astype(BF16), woc_ref[...])
    merged = (_sigmoid(zm[:, :D]) * ya + _sigmoid(zm[:, D:2 * D]) * yb + _sigmoid(zm[:, 2 * D:]) * yc)
    out = _rms(_dot(merged.astype(BF16), wout_ref[...]), gpost_ref[...])
    o_ref[0] = x_ref[0] + mod_ref[0][:, 2 * D:] * out


def _out_proj(x, mod, ya, ga, o_f, o_b, gg, yc, gc, w_merge, layer, lw, *, tm):
    B, N, D = x.shape
    per_batch_mod = mod.shape[0] > 1
    row3 = lambda w: pl.BlockSpec((1, tm, w), lambda b, i: (b, i, 0))
    consts = [lw["w_o_gqa"], lw["w_o_gla"], lw["w_o_mla"], lw["w_out"], lw["g_gla_out"], lw["g_post"]]
    in_specs = [row3(D),
                pl.BlockSpec((1, 1, 3 * D), (lambda b, i: (b, 0, 0)) if per_batch_mod else (lambda b, i: (0, 0, 0))),
                row3(GQA_WIDTH), row3(GQA_WIDTH), row3(GLA_WIDTH), row3(GLA_WIDTH), row3(GLA_WIDTH),
                row3(MLA_WIDTH), row3(MLA_WIDTH),
                _const_spec(lw["g_pre"].shape),
                pl.BlockSpec((1, 3 * D, D), lambda b, i: (layer, 0, 0), pipeline_mode=pl.Buffered(1))]
    in_specs += [_const_spec(c.shape) for c in consts]
    return pl.pallas_call(
        functools.partial(_out_proj_kernel, d_model=D),
        grid=(B, N // tm),
        in_specs=in_specs,
        out_specs=row3(D),
        out_shape=jax.ShapeDtypeStruct((B, N, D), F32),
        compiler_params=pltpu.CompilerParams(dimension_semantics=("parallel", "parallel"),
                                             vmem_limit_bytes=VMEM_LIMIT),
        name="out_proj",
    )(x, mod, ya, ga, o_f, o_b, gg, yc, gc, lw["g_pre"], w_merge, *consts)


def _rope_tables(n_tokens):
    t = np.arange(n_tokens)
    pos = np.stack([t // GRID_W, t % GRID_W], axis=0).astype(np.float64)

    def tables(lane_part, lane_in_part, half, active):
        freqs = ROPE_THETA ** (-(lane_in_part % half).astype(np.float64) / half)
        ang = pos[lane_part].T * freqs[None, :]
        cos = np.where(active[None, :], np.cos(ang), 1.0)
        sin = np.where(active[None, :], np.where(lane_in_part < half, -1.0, 1.0)[None, :] * np.sin(ang), 0.0)
        return jnp.asarray(cos, F32), jnp.asarray(sin, F32)

    lane = np.arange(LANES)
    ja = lane % GQA_HEAD_DIM
    ca, sa = tables(ja // (GQA_HEAD_DIM // 2), ja % (GQA_HEAD_DIM // 2), GQA_HEAD_DIM // 4, np.ones(LANES, bool))
    active = (lane >= MLA_ROPE_LANE) & (lane < MLA_ROPE_LANE + MLA_ROPE_DIM)
    jc = np.where(active, lane - MLA_ROPE_LANE, 0)
    cc, sc = tables(jc // (MLA_ROPE_DIM // 2), jc % (MLA_ROPE_DIM // 2), MLA_ROPE_DIM // 4, active)
    return ca, sa, cc, sc


def _pack_layer(l, p):
    w_dec = jnp.zeros((LANES, 2 * GLA_K_WIDTH), F32)
    w_dec = w_dec.at[:GLA_RANK, :GLA_K_WIDTH].set(p["w_gla_decay_fwd"][l])
    w_dec = w_dec.at[GLA_RANK:2 * GLA_RANK, GLA_K_WIDTH:].set(p["w_gla_decay_bwd"][l])
    b_dec = jnp.concatenate([p["b_gla_decay_fwd"][l], p["b_gla_decay_bwd"][l]])[None, :]
    pad_q = MLA_QK_PAD - MLA_NOPE_DIM - MLA_ROPE_DIM
    w_uq = p["w_mla_uq"][l].reshape(MLA_Q_LORA, MLA_HEADS, MLA_NOPE_DIM + MLA_ROPE_DIM)
    w_uq = jnp.pad(w_uq, ((0, 0), (0, 0), (0, pad_q))).reshape(MLA_Q_LORA, MLA_QK_WIDTH)
    w_ukv = p["w_mla_ukv"][l].reshape(MLA_KV_LORA, MLA_HEADS, MLA_NOPE_DIM + MLA_V_DIM)
    w_uk = jnp.pad(w_ukv[:, :, :MLA_NOPE_DIM], ((0, 0), (0, 0), (0, MLA_QK_PAD - MLA_NOPE_DIM)))
    w_uk = w_uk.reshape(MLA_KV_LORA, MLA_QK_WIDTH)
    w_uv = w_ukv[:, :, MLA_NOPE_DIM:].reshape(MLA_KV_LORA, MLA_WIDTH)
    seg = np.arange(GQA_WIDTH) // GQA_HEAD_DIM
    place = np.zeros((MLA_ROPE_DIM, MLA_QK_WIDTH), np.float32)
    for h in range(MLA_HEADS):
        place[np.arange(MLA_ROPE_DIM), h * MLA_QK_PAD + MLA_ROPE_LANE + np.arange(MLA_ROPE_DIM)] = 1.0
    return {
        "g_pre": p["g_pre"][l][None, :], "g_post": p["g_post"][l][None, :],
        "g_q": jnp.tile(p["g_q_norm"][l], GQA_HEADS)[None, :],
        "g_k": jnp.tile(p["g_k_norm"][l], GQA_KV_HEADS)[None, :],
        "ones": jnp.asarray(seg[:, None] == seg[None, :], BF16),
        "w_dec": w_dec.astype(BF16), "b_dec": b_dec,
        "g_mla_q": p["g_mla_q"][l][None, :], "g_mla_kv": p["g_mla_kv"][l][None, :],
        "w_uq": w_uq.astype(BF16), "w_uk": w_uk.astype(BF16), "w_uv": w_uv.astype(BF16),
        "kr_place": jnp.asarray(place, BF16),
        "w_o_gqa": p["w_o_gqa"][l].astype(BF16), "w_o_gla": p["w_o_gla"][l].astype(BF16),
        "w_o_mla": p["w_o_mla"][l].astype(BF16), "w_out": p["w_out"][l].astype(BF16),
        "g_gla_out": p["g_gla_out"][l][None, :],
    }


def _tiles(batch, seq):
    if seq <= 256:
        nb = 4 if batch % 4 == 0 else 1
        return dict(tm_in=seq, tm_out=seq, att_gqa=(seq, seq // 128, nb), att_mla=(seq, 1, nb), tk=512,
                    gla_tb=seq, gla_nb=2 if batch % 2 == 0 else 1)
    return dict(tm_in=512, tm_out=512, att_gqa=(1024, 16, 1), att_mla=(4096, 16, 1), tk=512,
                gla_tb=512, gla_nb=2 if batch % 2 == 0 else 1)


def _sub_layer(x, mod, w_packed, layer, lw, rope, cache):
    is_ctx = cache is None
    t = _tiles(x.shape[0], x.shape[1])
    outs = _in_proj(x, mod, w_packed[0], w_packed[1], layer, lw, rope, is_ctx=is_ctx, tm=t["tm_in"])
    (qa, ka, vat, ga, qkg, vg, gg, la, qc, kc, vct, gc) = outs[:12]
    src_a, src_c = [(ka, vat)], [(kc, vct)]
    s0f = s0b = None
    if not is_ctx:
        ka_p, vat_p, kc_p, vct_p = _cache_prep(cache["gqa_k"], cache["gqa_v"], cache["mla_ckv"],
                                               cache["mla_krope"], layer, lw)
        src_a, src_c = [(ka_p, vat_p)] + src_a, [(kc_p, vct_p)] + src_c
        s0f, s0b = cache["gla_fwd"], cache["gla_bwd"]
    (tq_a, nsub_a, nb_a), (tq_c, nsub_c, nb_c) = t["att_gqa"], t["att_mla"]
    ya = _attention(qa, src_a, n_groups=GQA_KV_HEADS, shared_kv=True, tq=tq_a, nsub=nsub_a, nb=nb_a, tk=t["tk"])
    yc = _attention(qc, src_c, n_groups=MLA_HEADS, shared_kv=False, tq=tq_c, nsub=nsub_c, nb=nb_c, tk=t["tk"])
    o_f, o_b, s_f, s_b = _gla(qkg, vg, la, s0f, s0b, layer, tb=t["gla_tb"], nb=t["gla_nb"])
    y = _out_proj(x, mod, ya, ga, o_f, o_b, gg, yc, gc, w_packed[2], layer, lw, tm=t["tm_out"])
    ctx = (outs[12], outs[13], outs[14], outs[15], s_f, s_b) if is_ctx else None
    return y, ctx


def kernel(x_prompt, x_sample, cache_gqa_k, cache_gqa_v, cache_mla_ckv, cache_mla_krope, state_gla_fwd, state_gla_bwd, c, c_ctx, w_mod, b_mod, g_pre, g_post, w_in, g_q_norm, g_k_norm, w_gla_decay_fwd, b_gla_decay_fwd, w_gla_decay_bwd, b_gla_decay_bwd, g_gla_out, g_mla_q, g_mla_kv, w_mla_uq, w_mla_ukv, w_o_gqa, w_o_gla, w_o_mla, w_out):
    params = dict(g_pre=g_pre, g_post=g_post, g_q_norm=g_q_norm, g_k_norm=g_k_norm,
                  w_gla_decay_fwd=w_gla_decay_fwd, b_gla_decay_fwd=b_gla_decay_fwd,
                  w_gla_decay_bwd=w_gla_decay_bwd, b_gla_decay_bwd=b_gla_decay_bwd, g_gla_out=g_gla_out,
                  g_mla_q=g_mla_q, g_mla_kv=g_mla_kv, w_mla_uq=w_mla_uq, w_mla_ukv=w_mla_ukv,
                  w_o_gqa=w_o_gqa, w_o_gla=w_o_gla, w_o_mla=w_o_mla, w_out=w_out)
    depth, D = w_in.shape[0], w_in.shape[1]
    B, S = x_prompt.shape[:2]
    Bd, Nd = x_sample.shape[:2]

    conds = jnp.concatenate([c_ctx[None, :], c, jnp.zeros((SUBLANES - 1 - Bd, D), F32)], axis=0)
    mods = _modulation(conds, w_mod, b_mod)
    rope = _rope_tables(Nd)
    w_t = jnp.swapaxes(w_in, 1, 2).astype(BF16)
    zrow = lambda n: jnp.zeros((depth, n, D), BF16)
    kr_rows = w_t[:, _R_KR:_R_GC]
    w_small = jnp.concatenate(
        [w_t[:, _R_R:_R_LAT], zrow(LANES - 2 * GLA_RANK),
         kr_rows, zrow(MLA_ROPE_LANE - MLA_ROPE_DIM), kr_rows, zrow(LANES - MLA_ROPE_LANE - MLA_ROPE_DIM)], axis=1)
    w_packed = (w_t, w_small, w_t[:, _R_M:])
    layers = [_pack_layer(l, params) for l in range(depth)]

    xp = x_prompt
    ctx_out = []
    for l in range(depth):
        xp, ctx = _sub_layer(xp, mods[l, 0:1][:, None, :], w_packed, l, layers[l], None, None)
        ctx_out.append(ctx)

    xs = x_sample
    cache = {"gqa_k": cache_gqa_k, "gqa_v": cache_gqa_v, "mla_ckv": cache_mla_ckv, "mla_krope": cache_mla_krope,
             "gla_fwd": state_gla_fwd, "gla_bwd": state_gla_bwd}
    for l in range(depth):
        xs, _ = _sub_layer(xs, mods[l, 1:1 + Bd][:, None, :], w_packed, l, layers[l], rope, cache)

    stack = lambda j: jnp.stack([ctx_out[l][j] for l in range(depth)], axis=1)
    new_k = stack(0).reshape(B, depth, S, GQA_KV_HEADS, GQA_HEAD_DIM)
    new_v = stack(1).reshape(B, depth, S, GQA_KV_HEADS, GQA_HEAD_DIM)
    return (xp, xs, new_k, new_v, stack(2), stack(3), stack(4), stack(5))
```

```python
import functools

import numpy as np
import jax
import jax.numpy as jnp
from jax import lax
from jax.experimental import pallas as pl
from jax.experimental.pallas import tpu as pltpu

F32 = jnp.float32
BF16 = jnp.bfloat16

EPS = 1e-6
LOG2_E = 1.4426950408889634
ROPE_THETA = 10000.0
GRID_W = 64

GQA_HEADS, GQA_KV_HEADS, GQA_HEAD_DIM = 8, 2, 64
GQA_REP = GQA_HEADS // GQA_KV_HEADS
GQA_WIDTH = GQA_HEADS * GQA_HEAD_DIM
GQA_KV_WIDTH = GQA_KV_HEADS * GQA_HEAD_DIM
GLA_HEADS, GLA_DK, GLA_DV = 4, 64, 128
GLA_WIDTH = GLA_HEADS * GLA_DV
GLA_K_WIDTH = GLA_HEADS * GLA_DK
GLA_RANK = 16
GLA_NORMALIZER = 16.0
GLA_CHUNK = 64
MLA_HEADS, MLA_Q_LORA, MLA_KV_LORA = 4, 256, 256
MLA_NOPE_DIM, MLA_ROPE_DIM, MLA_V_DIM = 64, 32, 128
MLA_WIDTH = MLA_HEADS * MLA_V_DIM
MLA_QK_PAD = 128
MLA_QK_WIDTH = MLA_HEADS * MLA_QK_PAD
MLA_ROPE_LANE = MLA_NOPE_DIM

LANES = 128
SUBLANES = 8
SCORE_SLOTS = 3
VMEM_LIMIT = 56 * 1024 * 1024

assert GQA_KV_WIDTH == LANES and 2 * GQA_HEAD_DIM == LANES and MLA_V_DIM == LANES


def _sigmoid(x):
    return 1.0 / (1.0 + jnp.exp(-x))


def _silu(x):
    return x * _sigmoid(x)


def _log_sigmoid(x):
    return jnp.minimum(x, 0.0) - jnp.log(1.0 + jnp.exp(-jnp.abs(x)))


def _dot(a, b):
    return jnp.dot(a, b, preferred_element_type=F32)


def _dot_nt(a, b):
    return lax.dot_general(a, b, (((1,), (1,)), ((), ())), preferred_element_type=F32)


def _dot_tn(a, b):
    return lax.dot_general(a, b, (((0,), (0,)), ((), ())), preferred_element_type=F32)


def _split3(x):
    hi = x.astype(BF16)
    r1 = x - hi.astype(F32)
    mid = r1.astype(BF16)
    lo = (r1 - mid.astype(F32)).astype(BF16)
    return hi, mid, lo


def _segment_mean_sq(x, ones_bf16, width):
    sq = x * x
    hi = sq.astype(BF16)
    lo = (sq - hi.astype(F32)).astype(BF16)
    return (_dot(hi, ones_bf16) + _dot(lo, ones_bf16)) * (1.0 / width)


def _swap_halves(x, half):
    n = x.shape[-1]
    lane = lax.broadcasted_iota(jnp.int32, x.shape, x.ndim - 1)
    first = (lane % (2 * half)) < half
    return jnp.where(first, pltpu.roll(x, n - half, x.ndim - 1), pltpu.roll(x, half, x.ndim - 1))


def _rope(x, cos, sin_signed, half):
    return x * cos + _swap_halves(x, half) * sin_signed


def _tile_lanes(t, reps):
    return t if reps == 1 else jnp.concatenate([t] * reps, axis=-1)


def _rms(x, gain):
    return x * lax.rsqrt(jnp.mean(x * x, axis=-1, keepdims=True) + EPS) * gain


def _mod_kernel(c_ref, w_ref, b_ref, o_ref):
    c = c_ref[...]
    o_ref[0] = _dot(_silu(c).astype(BF16), w_ref[0].astype(BF16)) + b_ref[0]


def _modulation(conds, w_mod, b_mod):
    L, D, D3 = w_mod.shape
    nj = D3 // D
    return pl.pallas_call(
        _mod_kernel,
        grid=(L, nj),
        in_specs=[pl.BlockSpec((SUBLANES, D), lambda l, j: (0, 0)),
                  pl.BlockSpec((1, D, D), lambda l, j: (l, 0, j)),
                  pl.BlockSpec((1, 1, D), lambda l, j: (l, 0, j))],
        out_specs=pl.BlockSpec((1, SUBLANES, D), lambda l, j: (l, 0, j)),
        out_shape=jax.ShapeDtypeStruct((L, SUBLANES, D3), F32),
        compiler_params=pltpu.CompilerParams(dimension_semantics=("parallel", "parallel")),
        name="modulation",
    )(conds, w_mod, b_mod.reshape(L, 1, D3))


_C_QA = 0
_C_KV = _C_QA + GQA_WIDTH
_C_GA = _C_KV + 2 * GQA_KV_WIDTH
_C_QKG = _C_GA + GQA_WIDTH
_C_VG = _C_QKG + 2 * GLA_K_WIDTH
_C_GG = _C_VG + GLA_WIDTH
_R_R = _C_GG + GLA_WIDTH
_R_LAT = _R_R + 2 * GLA_RANK
_R_KR = _R_LAT + MLA_Q_LORA + MLA_KV_LORA
_R_GC = _R_KR + MLA_ROPE_DIM
_R_M = _R_GC + MLA_WIDTH
assert _R_R % LANES == 0 and _R_LAT % 16 == 0 and _R_GC % 16 == 0


def _modulated_input(x, mod, g_pre, d_model):
    shift, scale = mod[:, :d_model], mod[:, d_model:2 * d_model]
    return (_rms(x, g_pre) * (1.0 + scale) + shift).astype(BF16)


def _in_proj_kernel(*refs, is_ctx, d_model):
    it = iter(refs)
    x_ref, mod_ref, gpre_ref, wt_ref, ws_ref, gq_ref, gk_ref, ones_ref, wdec_ref, bdec_ref = (next(it) for _ in range(10))
    gmq_ref, gmkv_ref, wuq_ref, wk_ref, wv_ref = (next(it) for _ in range(5))
    if not is_ctx:
        ca_ref, sa_ref, cc_ref, sc_ref = (next(it) for _ in range(4))
    (qa_o, ka_o, vat_o, ga_o, qkg_o, vg_o, gg_o, la_o,
     qc_o, kc_o, vct_o, gc_o) = (next(it) for _ in range(12))
    if is_ctx:
        ka32_o, va32_o, ckv32_o, kr32_o = (next(it) for _ in range(4))

    D = d_model
    hb = _modulated_input(x_ref[0], mod_ref[0], gpre_ref[...], D)

    z_head = _dot_nt(hb, wt_ref[0, :_R_R, :])
    z_lat = _dot_nt(hb, wt_ref[0, _R_LAT:_R_KR, :])
    z_gc = _dot_nt(hb, wt_ref[0, _R_GC:_R_M, :])
    z_small = _dot_nt(hb, ws_ref[0])

    def head(c0, width):
        return z_head[:, c0:c0 + width]

    ones = ones_ref[...]

    qa = head(_C_QA, GQA_WIDTH)
    qa = qa * lax.rsqrt(_segment_mean_sq(qa, ones, GQA_HEAD_DIM) + EPS) * gq_ref[...]
    kv = head(_C_KV, 2 * GQA_KV_WIDTH)
    ka, va = kv[:, :GQA_KV_WIDTH], kv[:, GQA_KV_WIDTH:]
    ka = ka * lax.rsqrt(_segment_mean_sq(ka, ones[:GQA_KV_WIDTH, :GQA_KV_WIDTH], GQA_HEAD_DIM) + EPS) * gk_ref[...]
    if is_ctx:
        ka32_o[0] = ka
        va32_o[0] = va
    else:
        ca, sa = ca_ref[...], sa_ref[...]
        qa = _rope(qa, _tile_lanes(ca, GQA_WIDTH // LANES), _tile_lanes(sa, GQA_WIDTH // LANES), GQA_HEAD_DIM // 4)
        ka = _rope(ka, ca, sa, GQA_HEAD_DIM // 4)
    qa_o[0] = (qa * (GQA_HEAD_DIM ** -0.5 * LOG2_E)).astype(BF16)
    ka_o[0] = ka.astype(BF16)
    vat_o[0] = va.T.astype(BF16)
    ga_o[0] = _silu(head(_C_GA, GQA_WIDTH)).astype(BF16)

    qkg = head(_C_QKG, 2 * GLA_K_WIDTH)
    lane = lax.broadcasted_iota(jnp.int32, qkg.shape, 1)
    qkg_o[0] = jnp.where(lane < GLA_K_WIDTH, qkg * GLA_DK ** -0.5, qkg)
    vg_o[0] = head(_C_VG, GLA_WIDTH)
    r = z_small[:, :LANES].astype(BF16)
    la_o[0] = _log_sigmoid(_dot(r, wdec_ref[...]) + bdec_ref[...]) * (1.0 / GLA_NORMALIZER)
    gg_o[0] = _silu(head(_C_GG, GLA_WIDTH)).astype(BF16)

    lat = z_lat
    ql = _rms(lat[:, :MLA_Q_LORA], gmq_ref[...])
    ckv = _rms(lat[:, MLA_Q_LORA:], gmkv_ref[...])
    qc = _dot(ql.astype(BF16), wuq_ref[...])
    krb = z_small[:, LANES:]
    if is_ctx:
        ckv32_o[0] = ckv
        kr32_o[0] = krb[:, :MLA_ROPE_DIM]
    else:
        cc, sc = cc_ref[...], sc_ref[...]
        qc = _rope(qc, _tile_lanes(cc, MLA_HEADS), _tile_lanes(sc, MLA_HEADS), MLA_ROPE_DIM // 4)
        krb = _rope(krb, cc, sc, MLA_ROPE_DIM // 4)
    lane = lax.broadcasted_iota(jnp.int32, krb.shape, 1)
    kr_part = jnp.where((lane >= MLA_ROPE_LANE) & (lane < MLA_ROPE_LANE + MLA_ROPE_DIM), krb, 0.0)
    qc_o[0] = (qc * ((MLA_NOPE_DIM + MLA_ROPE_DIM) ** -0.5 * LOG2_E)).astype(BF16)
    ckvb = ckv.astype(BF16)
    kc_o[0] = (_dot(ckvb, wk_ref[...]) + _tile_lanes(kr_part, MLA_HEADS)).astype(BF16)
    vct_o[0] = _dot(ckvb, wv_ref[...]).T.astype(BF16)
    gc_o[0] = _silu(z_gc).astype(BF16)


def _const_spec(shape):
    nd = len(shape)
    return pl.BlockSpec(shape, lambda b, i, _nd=nd: (0,) * _nd, pipeline_mode=pl.Buffered(1))


def _in_proj(x, mod, w_packed, w_small, layer, lw, rope, *, is_ctx, tm):
    B, N, D = x.shape
    assert w_packed.shape[1] == _R_M + 3 * D
    per_batch_mod = mod.shape[0] > 1
    row3 = lambda w: pl.BlockSpec((1, tm, w), lambda b, i: (b, i, 0))
    col3 = lambda h: pl.BlockSpec((1, h, tm), lambda b, i: (b, 0, i))
    consts = [lw["g_q"], lw["g_k"], lw["ones"], lw["w_dec"], lw["b_dec"],
              lw["g_mla_q"], lw["g_mla_kv"], lw["w_uq"], lw["w_uk"], lw["w_uv"]]
    in_specs = [row3(D),
                pl.BlockSpec((1, 1, 3 * D), (lambda b, i: (b, 0, 0)) if per_batch_mod else (lambda b, i: (0, 0, 0))),
                _const_spec(lw["g_pre"].shape),
                pl.BlockSpec((1, _R_M, D), lambda b, i: (layer, 0, 0), pipeline_mode=pl.Buffered(1)),
                pl.BlockSpec((1, 2 * LANES, D), lambda b, i: (layer, 0, 0), pipeline_mode=pl.Buffered(1))]
    in_specs += [_const_spec(c.shape) for c in consts]
    args = [x, mod, lw["g_pre"], w_packed, w_small] + consts
    if not is_ctx:
        in_specs += [pl.BlockSpec((tm, LANES), lambda b, i: (i, 0))] * 4
        args += list(rope)
    sds = jax.ShapeDtypeStruct
    out_shape = [
        sds((B, N, GQA_WIDTH), BF16), sds((B, N, GQA_KV_WIDTH), BF16), sds((B, GQA_KV_WIDTH, N), BF16),
        sds((B, N, GQA_WIDTH), BF16),
        sds((B, N, 2 * GLA_K_WIDTH), F32), sds((B, N, GLA_WIDTH), F32), sds((B, N, GLA_WIDTH), BF16),
        sds((B, N, 2 * GLA_K_WIDTH), F32),
        sds((B, N, MLA_QK_WIDTH), BF16), sds((B, N, MLA_QK_WIDTH), BF16), sds((B, MLA_WIDTH, N), BF16),
        sds((B, N, MLA_WIDTH), BF16)]
    out_specs = [
        row3(GQA_WIDTH), row3(GQA_KV_WIDTH), col3(GQA_KV_WIDTH), row3(GQA_WIDTH),
        row3(2 * GLA_K_WIDTH), row3(GLA_WIDTH), row3(GLA_WIDTH), row3(2 * GLA_K_WIDTH),
        row3(MLA_QK_WIDTH), row3(MLA_QK_WIDTH), col3(MLA_WIDTH), row3(MLA_WIDTH)]
    if is_ctx:
        out_shape += [sds((B, N, GQA_KV_WIDTH), F32), sds((B, N, GQA_KV_WIDTH), F32),
                      sds((B, N, MLA_KV_LORA), F32), sds((B, N, MLA_ROPE_DIM), F32)]
        out_specs += [row3(GQA_KV_WIDTH), row3(GQA_KV_WIDTH), row3(MLA_KV_LORA), row3(MLA_ROPE_DIM)]
    return pl.pallas_call(
        functools.partial(_in_proj_kernel, is_ctx=is_ctx, d_model=D),
        grid=(B, N // tm),
        in_specs=in_specs, out_specs=out_specs, out_shape=out_shape,
        compiler_params=pltpu.CompilerParams(dimension_semantics=("parallel", "parallel"),
                                             vmem_limit_bytes=VMEM_LIMIT),
        name="in_proj_ctx" if is_ctx else "in_proj_lat",
    )(*args)


def _cache_prep_kernel(gk_ref, gv_ref, ckv_ref, kr_ref, wk_ref, wv_ref, place_ref, ka_o, vat_o, kc_o, vct_o):
    ka_o[0] = gk_ref[0, 0].astype(BF16)
    vat_o[0] = gv_ref[0, 0].T.astype(BF16)
    ckvb = ckv_ref[0, 0].astype(BF16)
    kc_o[0] = (_dot(ckvb, wk_ref[...]) + _dot(kr_ref[0, 0].astype(BF16), place_ref[...])).astype(BF16)
    vct_o[0] = _dot(ckvb, wv_ref[...]).T.astype(BF16)


def _cache_prep(cache_gqa_k, cache_gqa_v, cache_mla_ckv, cache_mla_krope, layer, lw):
    B, L, P = cache_gqa_k.shape[:3]
    gk = cache_gqa_k.reshape(B, L, P, GQA_KV_WIDTH)
    gv = cache_gqa_v.reshape(B, L, P, GQA_KV_WIDTH)
    lsel = lambda w: pl.BlockSpec((1, 1, P, w), lambda b: (b, layer, 0, 0))
    full = lambda a: pl.BlockSpec(a.shape, lambda b: (0,) * a.ndim)
    sds = jax.ShapeDtypeStruct
    return pl.pallas_call(
        _cache_prep_kernel,
        grid=(B,),
        in_specs=[lsel(GQA_KV_WIDTH), lsel(GQA_KV_WIDTH), lsel(MLA_KV_LORA), lsel(MLA_ROPE_DIM),
                  full(lw["w_uk"]), full(lw["w_uv"]), full(lw["kr_place"])],
        out_specs=[pl.BlockSpec((1, P, GQA_KV_WIDTH), lambda b: (b, 0, 0)),
                   pl.BlockSpec((1, GQA_KV_WIDTH, P), lambda b: (b, 0, 0)),
                   pl.BlockSpec((1, P, MLA_QK_WIDTH), lambda b: (b, 0, 0)),
                   pl.BlockSpec((1, MLA_WIDTH, P), lambda b: (b, 0, 0))],
        out_shape=[sds((B, P, GQA_KV_WIDTH), BF16), sds((B, GQA_KV_WIDTH, P), BF16),
                   sds((B, P, MLA_QK_WIDTH), BF16), sds((B, MLA_WIDTH, P), BF16)],
        compiler_params=pltpu.CompilerParams(dimension_semantics=("parallel",)),
        name="cache_prep",
    )(gk, gv, cache_mla_ckv, cache_mla_krope, lw["w_uk"], lw["w_uv"], lw["kr_place"])


def _attention_kernel(*refs, n_src, tk, nsub, rep, shared_kv):
    q_ref, o_ref, s_scr = refs[0], refs[1 + 2 * n_src], refs[2 + 2 * n_src]
    srcs = [(refs[1 + 2 * j], refs[2 + 2 * j]) for j in range(n_src)]
    nb = q_ref.shape[0]
    tqs = q_ref.shape[1] // nsub
    g = pl.program_id(1)

    chunks = []
    off = 0
    for k_ref, vt_ref in srcs:
        n = k_ref.shape[1]
        step = min(tk, n)
        for c in range(n // step):
            chunks.append((k_ref, vt_ref, c * step, step, off))
            off += step

    def scores(j):
        b, rows = j // nsub, slice((j % nsub) * tqs, (j % nsub + 1) * tqs)
        if shared_kv:
            hd = LANES // 2
            x = q_ref[b, rows, :].astype(F32)
            lane = lax.broadcasted_iota(jnp.int32, (tqs, LANES), 1)
            pieces = []
            for r in range(rep):
                win = x[:, (r // 2) * LANES:(r // 2 + 1) * LANES]
                win = jnp.where(g == r % 2, win, pltpu.roll(win, hd, 1))
                pieces.append(jnp.where(lane // hd == g, win, 0.0).astype(BF16))
            qcat = jnp.concatenate(pieces, axis=0)
        else:
            qcat = q_ref[b, rows, :]
        W = qcat.shape[0]
        m_acc = jnp.full((SUBLANES, W), -jnp.inf, F32)
        for k_ref, _, c0, step, off in chunks:
            s = _dot_nt(k_ref[b, c0:c0 + step, :], qcat)
            s_scr[j % SCORE_SLOTS, off:off + step, :] = s
            m_acc = jnp.maximum(m_acc, jnp.max(s.reshape(step // SUBLANES, SUBLANES, W), axis=0))
        return jnp.max(m_acc, axis=0, keepdims=True)

    def outputs(j, m):
        W = m.shape[1]
        b, rows = j // nsub, slice((j % nsub) * tqs, (j % nsub + 1) * tqs)
        l_acc = jnp.zeros((SUBLANES, W), F32)
        acc = jnp.zeros((LANES, W), F32)
        for _, vt_ref, c0, step, off in chunks:
            p = jnp.exp2(s_scr[j % SCORE_SLOTS, off:off + step, :] - m)
            l_acc = l_acc + jnp.sum(p.reshape(step // SUBLANES, SUBLANES, W), axis=0)
            acc = acc + _dot(vt_ref[b, :, c0:c0 + step], p.astype(BF16))
        ot = acc / jnp.sum(l_acc, axis=0, keepdims=True)
        if shared_kv:
            ot = jnp.where(g == 0, ot[:LANES // 2], ot[LANES // 2:])
            for pair in range(rep // 2):
                two = jnp.concatenate([ot[:, (2 * pair) * tqs:(2 * pair + 1) * tqs],
                                       ot[:, (2 * pair + 1) * tqs:(2 * pair + 2) * tqs]], axis=0)
                o_ref[b, rows, pair * LANES:(pair + 1) * LANES] = two.T.astype(o_ref.dtype)
        else:
            o_ref[b, rows, :] = ot.T.astype(o_ref.dtype)

    m_prev = scores(0)
    for j in range(1, nb * nsub):
        m_next = scores(j)
        outputs(j - 1, m_prev)
        m_prev = m_next
    outputs(nb * nsub - 1, m_prev)


def _attention(q, sources, *, n_groups, shared_kv, tq, nsub, tk, nb):
    B, N, Wq = q.shape
    G = n_groups
    tqs = tq // nsub
    tq = min(tq, N)
    nsub = tq // tqs
    qw = Wq // G
    rep = qw // (LANES // 2) if shared_kv else 1
    kv_blk = (lambda b, g, i: (b, 0, 0)) if shared_kv else (lambda b, g, i: (b, 0, g))
    vt_blk = (lambda b, g, i: (b, 0, 0)) if shared_kv else (lambda b, g, i: (b, g, 0))
    in_specs = [pl.BlockSpec((nb, tq, qw), lambda b, g, i: (b, i, g))]
    args = [q]
    nk = 0
    for k, vt in sources:
        n = k.shape[1]
        nk += n
        in_specs += [pl.BlockSpec((nb, n, LANES), kv_blk), pl.BlockSpec((nb, LANES, n), vt_blk)]
        args += [k, vt]
    ow = qw if shared_kv else LANES
    return pl.pallas_call(
        functools.partial(_attention_kernel, n_src=len(sources), tk=tk, nsub=nsub, rep=rep, shared_kv=shared_kv),
        grid=(B // nb, G, N // tq),
        in_specs=in_specs,
        out_specs=pl.BlockSpec((nb, tq, ow), lambda b, g, i: (b, i, g)),
        out_shape=jax.ShapeDtypeStruct((B, N, G * ow), BF16),
        scratch_shapes=[pltpu.VMEM((min(SCORE_SLOTS, nb * nsub), nk, rep * tq // nsub), F32)],
        compiler_params=pltpu.CompilerParams(dimension_semantics=("parallel", "parallel", "parallel"),
                                             vmem_limit_bytes=VMEM_LIMIT),
        name="attention_gqa" if shared_kv else "attention_mla",
    )(*args)


def _gla_kernel(*refs, nchunk, nb, has_state):
    it = iter(refs)
    qkf_ref, vf_ref, gf_ref, qkb_ref, vb_ref, gb_ref = (next(it) for _ in range(6))
    if has_state:
        s0f_ref, s0b_ref = next(it), next(it)
    of_ref, ob_ref, sf_ref, sb_ref, st_scr, bd_scr = (next(it) for _ in range(6))
    C, H, DK, DV, KW, VW = GLA_CHUNK, GLA_HEADS, GLA_DK, GLA_DV, GLA_K_WIDTH, GLA_WIDTH
    i = pl.program_id(1)

    @pl.when(i == 0)
    def _():
        bd_scr[...] = jnp.zeros(bd_scr.shape, BF16)
        for d, s0_ref in enumerate((s0f_ref, s0b_ref) if has_state else (None, None)):
            for b in range(nb):
                for h in range(H):
                    s0 = s0_ref[b, 0, h] if has_state else jnp.zeros((DK, DV), F32)
                    st_scr[d, b, h] = s0
                    bd_scr[d, b, h * DK:(h + 1) * DK, h * DV:(h + 1) * DV] = s0.astype(BF16)

    row = lax.broadcasted_iota(jnp.int32, (C, H * C), 0)
    col = lax.broadcasted_iota(jnp.int32, (C, H * C), 1) % C
    keep = (row >= col, row <= col)
    tok = lax.broadcasted_iota(jnp.int32, (C, KW), 0)
    k_head = lax.broadcasted_iota(jnp.int32, (C, KW), 1) // DK
    v_head = lax.broadcasted_iota(jnp.int32, (C, VW), 1) // DV
    dirs = ((qkf_ref, vf_ref, gf_ref, of_ref, 0), (qkb_ref, vb_ref, gb_ref, ob_ref, KW))

    def chain(d, q, k, v, g, st, bd):
        cum = g
        shift = 1
        while shift < C:
            if d == 0:
                cum = cum + jnp.where(tok >= shift, pltpu.roll(cum, shift, 0), 0.0)
            else:
                cum = cum + jnp.where(tok < C - shift, pltpu.roll(cum, C - shift, 0), 0.0)
            shift *= 2
        last = cum[C - 1:C, :] if d == 0 else cum[0:1, :]
        qd = q * jnp.exp(cum)
        kd = k * jnp.exp(last - cum)
        kd_stack = jnp.concatenate([jnp.where(k_head == h, kd, 0.0).astype(BF16) for h in range(H)], axis=0)
        a = _dot_nt((qd * jnp.exp(-last)).astype(BF16), kd_stack)
        a = jnp.where(keep[d], a, 0.0).astype(BF16)
        v_stack = jnp.concatenate([jnp.where(v_head == h, v, 0.0).astype(BF16) for h in range(H)], axis=0)
        o = _dot(qd.astype(BF16), bd) + _dot(a, v_stack)
        kdt = kd.T.astype(BF16)
        decay = jnp.exp(jnp.broadcast_to(last, (LANES, KW)).T)
        vb = v.astype(BF16)
        s_new = [st[h] * decay[h * DK:(h + 1) * DK, :] + _dot(kdt[h * DK:(h + 1) * DK, :], vb[:, h * DV:(h + 1) * DV])
                 for h in range(H)]
        return o, s_new

    def block_diag(states):
        zero = jnp.zeros((DK, DV), BF16)
        return jnp.concatenate(
            [jnp.concatenate([states[h].astype(BF16) if j == h else zero for j in range(H)], axis=1)
             for h in range(H)], axis=0)

    def body(it, carry):
        work = []
        for d, (qk_ref, v_ref, g_ref, o_ref, g0) in enumerate(dirs):
            for b in range(nb):
                loads = []
                for u in range(per_iter):
                    c = it * per_iter + u
                    cc = c if d == 0 else nchunk - 1 - c
                    rows = pl.ds(pl.multiple_of(cc * C, C), C)
                    loads.append((rows, qk_ref[b, rows, :KW], qk_ref[b, rows, KW:], v_ref[b, rows, :],
                                  g_ref[b, rows, g0:g0 + KW]))
                work.append((d, b, o_ref, loads, [st_scr[d, b, h] for h in range(H)], bd_scr[d, b]))
        done = []
        for d, b, o_ref, loads, st, bd in work:
            outs = []
            for u, (rows, q, k, v, g) in enumerate(loads):
                o, st = chain(d, q, k, v, g, st, bd)
                outs.append((rows, o))
                if u + 1 < per_iter:
                    bd = block_diag(st)
            done.append((d, b, o_ref, outs, st))
        for d, b, o_ref, outs, st in done:
            for rows, o in outs:
                o_ref[b, rows, :] = o.astype(o_ref.dtype)
            for h in range(H):
                st_scr[d, b, h] = st[h]
                bd_scr[d, b, h * DK:(h + 1) * DK, h * DV:(h + 1) * DV] = st[h].astype(BF16)
        return carry

    per_iter = 4 if nchunk % 4 == 0 else 1
    lax.fori_loop(0, nchunk // per_iter, body, 0)

    @pl.when(i == pl.num_programs(1) - 1)
    def _():
        for d, s_ref in enumerate((sf_ref, sb_ref)):
            for b in range(nb):
                s_ref[b] = st_scr[d, b]


def _gla(qkg, vg, la, state_fwd, state_bwd, layer, *, tb, nb):
    B, N, _ = qkg.shape
    H, DK, DV = GLA_HEADS, GLA_DK, GLA_DV
    nblk = N // tb
    fwd = lambda w: pl.BlockSpec((nb, tb, w), lambda b, i: (b, i, 0))
    bwd = lambda w: pl.BlockSpec((nb, tb, w), lambda b, i: (b, nblk - 1 - i, 0))
    in_specs = [fwd(2 * GLA_K_WIDTH), fwd(GLA_WIDTH), fwd(2 * GLA_K_WIDTH),
                bwd(2 * GLA_K_WIDTH), bwd(GLA_WIDTH), bwd(2 * GLA_K_WIDTH)]
    args = [qkg, vg, la, qkg, vg, la]
    has_state = state_fwd is not None
    if has_state:
        s0 = pl.BlockSpec((nb, 1, H, DK, DV), lambda b, i: (b, layer, 0, 0, 0))
        in_specs += [s0, s0]
        args += [state_fwd, state_bwd]
    st = pl.BlockSpec((nb, H, DK, DV), lambda b, i: (b, 0, 0, 0))
    sds = jax.ShapeDtypeStruct
    return pl.pallas_call(
        functools.partial(_gla_kernel, nchunk=tb // GLA_CHUNK, nb=nb, has_state=has_state),
        grid=(B // nb, nblk),
        in_specs=in_specs,
        out_specs=[fwd(GLA_WIDTH), bwd(GLA_WIDTH), st, st],
        out_shape=[sds((B, N, GLA_WIDTH), BF16), sds((B, N, GLA_WIDTH), BF16),
                   sds((B, H, DK, DV), F32), sds((B, H, DK, DV), F32)],
        scratch_shapes=[pltpu.VMEM((2, nb, H, DK, DV), F32), pltpu.VMEM((2, nb, GLA_K_WIDTH, GLA_WIDTH), BF16)],
        compiler_params=pltpu.CompilerParams(dimension_semantics=("parallel", "arbitrary"),
                                             vmem_limit_bytes=VMEM_LIMIT),
        name="gla",
    )(*args)


def _out_proj_kernel(x_ref, mod_ref, ya_ref, ga_ref, of_ref, ob_ref, gg_ref, yc_ref, gc_ref,
                     gpre_ref, wm_ref, woa_ref, wog_ref, woc_ref, wout_ref, ggla_ref, gpost_ref,
                     o_ref, *, d_model):
    D = d_model
    zm = _dot_nt(_modulated_input(x_ref[0], mod_ref[0], gpre_ref[...], D), wm_ref[0])
    ya = _dot((ya_ref[0] * ga_ref[0]).astype(BF16), woa_ref[...])
    og = of_ref[0].astype(F32) + ob_ref[0].astype(F32)
    gg = gg_ref[0]
    heads = []
    for h in range(GLA_HEADS):
        sl = slice(h * GLA_DV, (h + 1) * GLA_DV)
        heads.append((_rms(og[:, sl], ggla_ref[...]) * gg[:, sl]).astype(BF16))
    yb = _dot(jnp.concatenate(heads, axis=-1), wog_ref[...])
    yc = _dot((yc_ref[0] * gc_ref[0]).astype(BF16), woc_ref[...])
    merged = (_sigmoid(zm[:, :D]) * ya + _sigmoid(zm[:, D:2 * D]) * yb + _sigmoid(zm[:, 2 * D:]) * yc)
    out = _rms(_dot(merged.astype(BF16), wout_ref[...]), gpost_ref[...])
    o_ref[0] = x_ref[0] + mod_ref[0][:, 2 * D:] * out


def _out_proj(x, mod, ya, ga, o_f, o_b, gg, yc, gc, w_merge, layer, lw, *, tm):
    B, N, D = x.shape
    per_batch_mod = mod.shape[0] > 1
    row3 = lambda w: pl.BlockSpec((1, tm, w), lambda b, i: (b, i, 0))
    consts = [lw["w_o_gqa"], lw["w_o_gla"], lw["w_o_mla"], lw["w_out"], lw["g_gla_out"], lw["g_post"]]
    in_specs = [row3(D),
                pl.BlockSpec((1, 1, 3 * D), (lambda b, i: (b, 0, 0)) if per_batch_mod else (lambda b, i: (0, 0, 0))),
                row3(GQA_WIDTH), row3(GQA_WIDTH), row3(GLA_WIDTH), row3(GLA_WIDTH), row3(GLA_WIDTH),
                row3(MLA_WIDTH), row3(MLA_WIDTH),
                _const_spec(lw["g_pre"].shape),
                pl.BlockSpec((1, 3 * D, D), lambda b, i: (layer, 0, 0), pipeline_mode=pl.Buffered(1))]
    in_specs += [_const_spec(c.shape) for c in consts]
    return pl.pallas_call(
        functools.partial(_out_proj_kernel, d_model=D),
        grid=(B, N // tm),
        in_specs=in_specs,
        out_specs=row3(D),
        out_shape=jax.ShapeDtypeStruct((B, N, D), F32),
        compiler_params=pltpu.CompilerParams(dimension_semantics=("parallel", "parallel"),
                                             vmem_limit_bytes=VMEM_LIMIT),
        name="out_proj",
    )(x, mod, ya, ga, o_f, o_b, gg, yc, gc, lw["g_pre"], w_merge, *consts)


def _rope_tables(n_tokens):
    t = np.arange(n_tokens)
    pos = np.stack([t // GRID_W, t % GRID_W], axis=0).astype(np.float64)

    def tables(lane_part, lane_in_part, half, active):
        freqs = ROPE_THETA ** (-(lane_in_part % half).astype(np.float64) / half)
        ang = pos[lane_part].T * freqs[None, :]
        cos = np.where(active[None, :], np.cos(ang), 1.0)
        sin = np.where(active[None, :], np.where(lane_in_part < half, -1.0, 1.0)[None, :] * np.sin(ang), 0.0)
        return jnp.asarray(cos, F32), jnp.asarray(sin, F32)

    lane = np.arange(LANES)
    ja = lane % GQA_HEAD_DIM
    ca, sa = tables(ja // (GQA_HEAD_DIM // 2), ja % (GQA_HEAD_DIM // 2), GQA_HEAD_DIM // 4, np.ones(LANES, bool))
    active = (lane >= MLA_ROPE_LANE) & (lane < MLA_ROPE_LANE + MLA_ROPE_DIM)
    jc = np.where(active, lane - MLA_ROPE_LANE, 0)
    cc, sc = tables(jc // (MLA_ROPE_DIM // 2), jc % (MLA_ROPE_DIM // 2), MLA_ROPE_DIM // 4, active)
    return ca, sa, cc, sc


def _pack_layer(l, p):
    w_dec = jnp.zeros((LANES, 2 * GLA_K_WIDTH), F32)
    w_dec = w_dec.at[:GLA_RANK, :GLA_K_WIDTH].set(p["w_gla_decay_fwd"][l])
    w_dec = w_dec.at[GLA_RANK:2 * GLA_RANK, GLA_K_WIDTH:].set(p["w_gla_decay_bwd"][l])
    b_dec = jnp.concatenate([p["b_gla_decay_fwd"][l], p["b_gla_decay_bwd"][l]])[None, :]
    pad_q = MLA_QK_PAD - MLA_NOPE_DIM - MLA_ROPE_DIM
    w_uq = p["w_mla_uq"][l].reshape(MLA_Q_LORA, MLA_HEADS, MLA_NOPE_DIM + MLA_ROPE_DIM)
    w_uq = jnp.pad(w_uq, ((0, 0), (0, 0), (0, pad_q))).reshape(MLA_Q_LORA, MLA_QK_WIDTH)
    w_ukv = p["w_mla_ukv"][l].reshape(MLA_KV_LORA, MLA_HEADS, MLA_NOPE_DIM + MLA_V_DIM)
    w_uk = jnp.pad(w_ukv[:, :, :MLA_NOPE_DIM], ((0, 0), (0, 0), (0, MLA_QK_PAD - MLA_NOPE_DIM)))
    w_uk = w_uk.reshape(MLA_KV_LORA, MLA_QK_WIDTH)
    w_uv = w_ukv[:, :, MLA_NOPE_DIM:].reshape(MLA_KV_LORA, MLA_WIDTH)
    seg = np.arange(GQA_WIDTH) // GQA_HEAD_DIM
    place = np.zeros((MLA_ROPE_DIM, MLA_QK_WIDTH), np.float32)
    for h in range(MLA_HEADS):
        place[np.arange(MLA_ROPE_DIM), h * MLA_QK_PAD + MLA_ROPE_LANE + np.arange(MLA_ROPE_DIM)] = 1.0
    return {
        "g_pre": p["g_pre"][l][None, :], "g_post": p["g_post"][l][None, :],
        "g_q": jnp.tile(p["g_q_norm"][l], GQA_HEADS)[None, :],
        "g_k": jnp.tile(p["g_k_norm"][l], GQA_KV_HEADS)[None, :],
        "ones": jnp.asarray(seg[:, None] == seg[None, :], BF16),
        "w_dec": w_dec.astype(BF16), "b_dec": b_dec,
        "g_mla_q": p["g_mla_q"][l][None, :], "g_mla_kv": p["g_mla_kv"][l][None, :],
        "w_uq": w_uq.astype(BF16), "w_uk": w_uk.astype(BF16), "w_uv": w_uv.astype(BF16),
        "kr_place": jnp.asarray(place, BF16),
        "w_o_gqa": p["w_o_gqa"][l].astype(BF16), "w_o_gla": p["w_o_gla"][l].astype(BF16),
        "w_o_mla": p["w_o_mla"][l].astype(BF16), "w_out": p["w_out"][l].astype(BF16),
        "g_gla_out": p["g_gla_out"][l][None, :],
    }


def _tiles(batch, seq):
    if seq <= 256:
        nb = 4 if batch % 4 == 0 else 1
        return dict(tm_in=seq, tm_out=seq, att_gqa=(seq, seq // 128, nb), att_mla=(seq, 1, nb), tk=512,
                    gla_tb=seq, gla_nb=2 if batch % 2 == 0 else 1)
    return dict(tm_in=512, tm_out=512, att_gqa=(1024, 16, 1), att_mla=(4096, 16, 1), tk=512,
                gla_tb=512, gla_nb=2 if batch % 2 == 0 else 1)


def _sub_layer(x, mod, w_packed, layer, lw, rope, cache):
    is_ctx = cache is None
    t = _tiles(x.shape[0], x.shape[1])
    outs = _in_proj(x, mod, w_packed[0], w_packed[1], layer, lw, rope, is_ctx=is_ctx, tm=t["tm_in"])
    (qa, ka, vat, ga, qkg, vg, gg, la, qc, kc, vct, gc) = outs[:12]
    src_a, src_c = [(ka, vat)], [(kc, vct)]
    s0f = s0b = None
    if not is_ctx:
        ka_p, vat_p, kc_p, vct_p = _cache_prep(cache["gqa_k"], cache["gqa_v"], cache["mla_ckv"],
                                               cache["mla_krope"], layer, lw)
        src_a, src_c = [(ka_p, vat_p)] + src_a, [(kc_p, vct_p)] + src_c
        s0f, s0b = cache["gla_fwd"], cache["gla_bwd"]
    (tq_a, nsub_a, nb_a), (tq_c, nsub_c, nb_c) = t["att_gqa"], t["att_mla"]
    ya = _attention(qa, src_a, n_groups=GQA_KV_HEADS, shared_kv=True, tq=tq_a, nsub=nsub_a, nb=nb_a, tk=t["tk"])
    yc = _attention(qc, src_c, n_groups=MLA_HEADS, shared_kv=False, tq=tq_c, nsub=nsub_c, nb=nb_c, tk=t["tk"])
    o_f, o_b, s_f, s_b = _gla(qkg, vg, la, s0f, s0b, layer, tb=t["gla_tb"], nb=t["gla_nb"])
    y = _out_proj(x, mod, ya, ga, o_f, o_b, gg, yc, gc, w_packed[2], layer, lw, tm=t["tm_out"])
    ctx = (outs[12], outs[13], outs[14], outs[15], s_f, s_b) if is_ctx else None
    return y, ctx


def kernel(x_prompt, x_sample, cache_gqa_k, cache_gqa_v, cache_mla_ckv, cache_mla_krope, state_gla_fwd, state_gla_bwd, c, c_ctx, w_mod, b_mod, g_pre, g_post, w_in, g_q_norm, g_k_norm, w_gla_decay_fwd, b_gla_decay_fwd, w_gla_decay_bwd, b_gla_decay_bwd, g_gla_out, g_mla_q, g_mla_kv, w_mla_uq, w_mla_ukv, w_o_gqa, w_o_gla, w_o_mla, w_out):
    params = dict(g_pre=g_pre, g_post=g_post, g_q_norm=g_q_norm, g_k_norm=g_k_norm,
                  w_gla_decay_fwd=w_gla_decay_fwd, b_gla_decay_fwd=b_gla_decay_fwd,
                  w_gla_decay_bwd=w_gla_decay_bwd, b_gla_decay_bwd=b_gla_decay_bwd, g_gla_out=g_gla_out,
                  g_mla_q=g_mla_q, g_mla_kv=g_mla_kv, w_mla_uq=w_mla_uq, w_mla_ukv=w_mla_ukv,
                  w_o_gqa=w_o_gqa, w_o_gla=w_o_gla, w_o_mla=w_o_mla, w_out=w_out)
    depth, D = w_in.shape[0], w_in.shape[1]
    B, S = x_prompt.shape[:2]
    Bd, Nd = x_sample.shape[:2]

    conds = jnp.concatenate([c_ctx[None, :], c, jnp.zeros((SUBLANES - 1 - Bd, D), F32)], axis=0)
    mods = _modulation(conds, w_mod, b_mod)
    rope = _rope_tables(Nd)
    w_t = jnp.swapaxes(w_in, 1, 2).astype(BF16)
    zrow = lambda n: jnp.zeros((depth, n, D), BF16)
    kr_rows = w_t[:, _R_KR:_R_GC]
    w_small = jnp.concatenate(
        [w_t[:, _R_R:_R_LAT], zrow(LANES - 2 * GLA_RANK),
         kr_rows, zrow(MLA_ROPE_LANE - MLA_ROPE_DIM), kr_rows, zrow(LANES - MLA_ROPE_LANE - MLA_ROPE_DIM)], axis=1)
    w_packed = (w_t, w_small, w_t[:, _R_M:])
    layers = [_pack_layer(l, params) for l in range(depth)]

    xp = x_prompt
    ctx_out = []
    for l in range(depth):
        xp, ctx = _sub_layer(xp, mods[l, 0:1][:, None, :], w_packed, l, layers[l], None, None)
        ctx_out.append(ctx)

    xs = x_sample
    cache = {"gqa_k": cache_gqa_k, "gqa_v": cache_gqa_v, "mla_ckv": cache_mla_ckv, "mla_krope": cache_mla_krope,
             "gla_fwd": state_gla_fwd, "gla_bwd": state_gla_bwd}
    for l in range(depth):
        xs, _ = _sub_layer(xs, mods[l, 1:1 + Bd][:, None, :], w_packed, l, layers[l], rope, cache)

    stack = lambda j: jnp.stack([ctx_out[l][j] for l in range(depth)], axis=1)
    new_k = stack(0).reshape(B, depth, S, GQA_KV_HEADS, GQA_HEAD_DIM)
    new_v = stack(1).reshape(B, depth, S, GQA_KV_HEADS, GQA_HEAD_DIM)
    return (xp, xs, new_k, new_v, stack(2), stack(3), stack(4), stack(5))
```

```python
import functools

import numpy as np
import jax
import jax.numpy as jnp
from jax import lax
from jax.experimental import pallas as pl
from jax.experimental.pallas import tpu as pltpu

F32 = jnp.float32
BF16 = jnp.bfloat16

EPS = 1e-6
LOG2_E = 1.4426950408889634
ROPE_THETA = 10000.0
GRID_W = 64

GQA_HEADS, GQA_KV_HEADS, GQA_HEAD_DIM = 8, 2, 64
GQA_REP = GQA_HEADS // GQA_KV_HEADS
GQA_WIDTH = GQA_HEADS * GQA_HEAD_DIM
GQA_KV_WIDTH = GQA_KV_HEADS * GQA_HEAD_DIM
GLA_HEADS, GLA_DK, GLA_DV = 4, 64, 128
GLA_WIDTH = GLA_HEADS * GLA_DV
GLA_K_WIDTH = GLA_HEADS * GLA_DK
GLA_RANK = 16
GLA_NORMALIZER = 16.0
GLA_CHUNK = 64
MLA_HEADS, MLA_Q_LORA, MLA_KV_LORA = 4, 256, 256
MLA_NOPE_DIM, MLA_ROPE_DIM, MLA_V_DIM = 64, 32, 128
MLA_WIDTH = MLA_HEADS * MLA_V_DIM
MLA_QK_PAD = 128
MLA_QK_WIDTH = MLA_HEADS * MLA_QK_PAD
MLA_ROPE_LANE = MLA_NOPE_DIM

LANES = 128
SUBLANES = 8
BF16_SUBLANES = 16
SCORE_SLOTS = 3
VMEM_LIMIT = 56 * 1024 * 1024

assert GQA_KV_WIDTH == LANES and 2 * GQA_HEAD_DIM == LANES and MLA_V_DIM == LANES


def _sigmoid(x):
    return 1.0 / (1.0 + jnp.exp(-x))


def _silu(x):
    return x * _sigmoid(x)


def _log_sigmoid(x):
    return jnp.minimum(x, 0.0) - jnp.log(1.0 + jnp.exp(-jnp.abs(x)))


def _dot(a, b):
    return jnp.dot(a, b, preferred_element_type=F32)


def _dot_nt(a, b):
    return lax.dot_general(a, b, (((1,), (1,)), ((), ())), preferred_element_type=F32)


def _segment_mean_sq(x, ones_bf16, width):
    sq = x * x
    hi = sq.astype(BF16)
    lo = (sq - hi.astype(F32)).astype(BF16)
    return (_dot(hi, ones_bf16) + _dot(lo, ones_bf16)) * (1.0 / width)


def _swap_halves(x, half):
    n = x.shape[-1]
    lane = lax.broadcasted_iota(jnp.int32, x.shape, x.ndim - 1)
    first = (lane % (2 * half)) < half
    return jnp.where(first, pltpu.roll(x, n - half, x.ndim - 1), pltpu.roll(x, half, x.ndim - 1))


def _rope(x, cos, sin_signed, half):
    return x * cos + _swap_halves(x, half) * sin_signed


def _tile_lanes(t, reps):
    return t if reps == 1 else jnp.concatenate([t] * reps, axis=-1)


def _rms(x, gain):
    return x * lax.rsqrt(jnp.mean(x * x, axis=-1, keepdims=True) + EPS) * gain


def _mod_kernel(c_ref, w_ref, b_ref, o_ref):
    c = c_ref[...]
    o_ref[0] = _dot(_silu(c).astype(BF16), w_ref[0].astype(BF16)) + b_ref[0]


def _modulation(conds, w_mod, b_mod):
    L, D, D3 = w_mod.shape
    nj = D3 // D
    return pl.pallas_call(
        _mod_kernel,
        grid=(L, nj),
        in_specs=[pl.BlockSpec((SUBLANES, D), lambda l, j: (0, 0)),
                  pl.BlockSpec((1, D, D), lambda l, j: (l, 0, j)),
                  pl.BlockSpec((1, 1, D), lambda l, j: (l, 0, j))],
        out_specs=pl.BlockSpec((1, SUBLANES, D), lambda l, j: (l, 0, j)),
        out_shape=jax.ShapeDtypeStruct((L, SUBLANES, D3), F32),
        compiler_params=pltpu.CompilerParams(dimension_semantics=("parallel", "parallel")),
        name="modulation",
    )(conds, w_mod, b_mod.reshape(L, 1, D3))


_C_QA = 0
_C_KV = _C_QA + GQA_WIDTH
_C_GA = _C_KV + 2 * GQA_KV_WIDTH
_C_QKG = _C_GA + GQA_WIDTH
_C_VG = _C_QKG + 2 * GLA_K_WIDTH
_C_GG = _C_VG + GLA_WIDTH
_R_R = _C_GG + GLA_WIDTH
_R_LAT = _R_R + 2 * GLA_RANK
_R_KR = _R_LAT + MLA_Q_LORA + MLA_KV_LORA
_R_GC = _R_KR + MLA_ROPE_DIM
_R_M = _R_GC + MLA_WIDTH
assert _R_R % LANES == 0 and _R_LAT % BF16_SUBLANES == 0 and _R_GC % BF16_SUBLANES == 0


def _modulated_input(x, mod, g_pre, d_model):
    shift, scale = mod[:, :d_model], mod[:, d_model:2 * d_model]
    return (_rms(x, g_pre) * (1.0 + scale) + shift).astype(BF16)


def _in_proj_kernel(*refs, is_ctx, d_model):
    it = iter(refs)
    x_ref, mod_ref, gpre_ref, wt_ref, ws_ref, gq_ref, gk_ref, ones_ref, wdec_ref, bdec_ref = (next(it) for _ in range(10))
    gmq_ref, gmkv_ref, wuq_ref, wk_ref, wv_ref = (next(it) for _ in range(5))
    if not is_ctx:
        ca_ref, sa_ref, cc_ref, sc_ref = (next(it) for _ in range(4))
    (qa_o, ka_o, vat_o, ga_o, qkg_o, vg_o, gg_o, la_o,
     qc_o, kc_o, vct_o, gc_o) = (next(it) for _ in range(12))
    if is_ctx:
        ka32_o, va32_o, ckv32_o, kr32_o = (next(it) for _ in range(4))

    D = d_model
    hb = _modulated_input(x_ref[0], mod_ref[0], gpre_ref[...], D)

    z_head = _dot_nt(hb, wt_ref[0, :_R_R, :])
    z_lat = _dot_nt(hb, wt_ref[0, _R_LAT:_R_KR, :])
    z_gc = _dot_nt(hb, wt_ref[0, _R_GC:_R_M, :])
    z_small = _dot_nt(hb, ws_ref[0])

    def head(c0, width):
        return z_head[:, c0:c0 + width]

    ones = ones_ref[...]

    qa = head(_C_QA, GQA_WIDTH)
    qa = qa * lax.rsqrt(_segment_mean_sq(qa, ones, GQA_HEAD_DIM) + EPS) * gq_ref[...]
    kv = head(_C_KV, 2 * GQA_KV_WIDTH)
    ka, va = kv[:, :GQA_KV_WIDTH], kv[:, GQA_KV_WIDTH:]
    ka = ka * lax.rsqrt(_segment_mean_sq(ka, ones[:GQA_KV_WIDTH, :GQA_KV_WIDTH], GQA_HEAD_DIM) + EPS) * gk_ref[...]
    if is_ctx:
        ka32_o[0] = ka
        va32_o[0] = va
    else:
        ca, sa = ca_ref[...], sa_ref[...]
        qa = _rope(qa, _tile_lanes(ca, GQA_WIDTH // LANES), _tile_lanes(sa, GQA_WIDTH // LANES), GQA_HEAD_DIM // 4)
        ka = _rope(ka, ca, sa, GQA_HEAD_DIM // 4)
    qa_o[0] = (qa * (GQA_HEAD_DIM ** -0.5 * LOG2_E)).astype(BF16)
    ka_o[0] = ka.astype(BF16)
    vat_o[0] = va.T.astype(BF16)
    ga_o[0] = _silu(head(_C_GA, GQA_WIDTH)).astype(BF16)

    qkg = head(_C_QKG, 2 * GLA_K_WIDTH)
    lane = lax.broadcasted_iota(jnp.int32, qkg.shape, 1)
    qkg_o[0] = jnp.where(lane < GLA_K_WIDTH, qkg * GLA_DK ** -0.5, qkg)
    vg_o[0] = head(_C_VG, GLA_WIDTH)
    r = z_small[:, :LANES].astype(BF16)
    la_o[0] = _log_sigmoid(_dot(r, wdec_ref[...]) + bdec_ref[...]) * (1.0 / GLA_NORMALIZER)
    gg_o[0] = _silu(head(_C_GG, GLA_WIDTH)).astype(BF16)

    lat = z_lat
    ql = _rms(lat[:, :MLA_Q_LORA], gmq_ref[...])
    ckv = _rms(lat[:, MLA_Q_LORA:], gmkv_ref[...])
    qc = _dot(ql.astype(BF16), wuq_ref[...])
    krb = z_small[:, LANES:]
    if is_ctx:
        ckv32_o[0] = ckv
        kr32_o[0] = krb[:, :MLA_ROPE_DIM]
    else:
        cc, sc = cc_ref[...], sc_ref[...]
        qc = _rope(qc, _tile_lanes(cc, MLA_HEADS), _tile_lanes(sc, MLA_HEADS), MLA_ROPE_DIM // 4)
        krb = _rope(krb, cc, sc, MLA_ROPE_DIM // 4)
    lane = lax.broadcasted_iota(jnp.int32, krb.shape, 1)
    kr_part = jnp.where((lane >= MLA_ROPE_LANE) & (lane < MLA_ROPE_LANE + MLA_ROPE_DIM), krb, 0.0)
    qc_o[0] = (qc * ((MLA_NOPE_DIM + MLA_ROPE_DIM) ** -0.5 * LOG2_E)).astype(BF16)
    ckvb = ckv.astype(BF16)
    kc_o[0] = (_dot(ckvb, wk_ref[...]) + _tile_lanes(kr_part, MLA_HEADS)).astype(BF16)
    vct_o[0] = _dot(ckvb, wv_ref[...]).T.astype(BF16)
    gc_o[0] = _silu(z_gc).astype(BF16)


def _const_spec(shape):
    nd = len(shape)
    return pl.BlockSpec(shape, lambda b, i, _nd=nd: (0,) * _nd, pipeline_mode=pl.Buffered(1))


def _in_proj(x, mod, w_packed, w_small, layer, lw, rope, *, is_ctx, tm):
    B, N, D = x.shape
    assert w_packed.shape[1] == _R_M + 3 * D
    per_batch_mod = mod.shape[0] > 1
    row3 = lambda w: pl.BlockSpec((1, tm, w), lambda b, i: (b, i, 0))
    col3 = lambda h: pl.BlockSpec((1, h, tm), lambda b, i: (b, 0, i))
    consts = [lw["g_q"], lw["g_k"], lw["ones"], lw["w_dec"], lw["b_dec"],
              lw["g_mla_q"], lw["g_mla_kv"], lw["w_uq"], lw["w_uk"], lw["w_uv"]]
    in_specs = [row3(D),
                pl.BlockSpec((1, 1, 3 * D), (lambda b, i: (b, 0, 0)) if per_batch_mod else (lambda b, i: (0, 0, 0))),
                _const_spec(lw["g_pre"].shape),
                pl.BlockSpec((1, _R_M, D), lambda b, i: (layer, 0, 0), pipeline_mode=pl.Buffered(1)),
                pl.BlockSpec((1, 2 * LANES, D), lambda b, i: (layer, 0, 0), pipeline_mode=pl.Buffered(1))]
    in_specs += [_const_spec(c.shape) for c in consts]
    args = [x, mod, lw["g_pre"], w_packed, w_small] + consts
    if not is_ctx:
        in_specs += [pl.BlockSpec((tm, LANES), lambda b, i: (i, 0))] * 4
        args += list(rope)
    sds = jax.ShapeDtypeStruct
    out_shape = [
        sds((B, N, GQA_WIDTH), BF16), sds((B, N, GQA_KV_WIDTH), BF16), sds((B, GQA_KV_WIDTH, N), BF16),
        sds((B, N, GQA_WIDTH), BF16),
        sds((B, N, 2 * GLA_K_WIDTH), F32), sds((B, N, GLA_WIDTH), F32), sds((B, N, GLA_WIDTH), BF16),
        sds((B, N, 2 * GLA_K_WIDTH), F32),
        sds((B, N, MLA_QK_WIDTH), BF16), sds((B, N, MLA_QK_WIDTH), BF16), sds((B, MLA_WIDTH, N), BF16),
        sds((B, N, MLA_WIDTH), BF16)]
    out_specs = [
        row3(GQA_WIDTH), row3(GQA_KV_WIDTH), col3(GQA_KV_WIDTH), row3(GQA_WIDTH),
        row3(2 * GLA_K_WIDTH), row3(GLA_WIDTH), row3(GLA_WIDTH), row3(2 * GLA_K_WIDTH),
        row3(MLA_QK_WIDTH), row3(MLA_QK_WIDTH), col3(MLA_WIDTH), row3(MLA_WIDTH)]
    if is_ctx:
        out_shape += [sds((B, N, GQA_KV_WIDTH), F32), sds((B, N, GQA_KV_WIDTH), F32),
                      sds((B, N, MLA_KV_LORA), F32), sds((B, N, MLA_ROPE_DIM), F32)]
        out_specs += [row3(GQA_KV_WIDTH), row3(GQA_KV_WIDTH), row3(MLA_KV_LORA), row3(MLA_ROPE_DIM)]
    return pl.pallas_call(
        functools.partial(_in_proj_kernel, is_ctx=is_ctx, d_model=D),
        grid=(B, N // tm),
        in_specs=in_specs, out_specs=out_specs, out_shape=out_shape,
        compiler_params=pltpu.CompilerParams(dimension_semantics=("parallel", "parallel"),
                                             vmem_limit_bytes=VMEM_LIMIT),
        name="in_proj_ctx" if is_ctx else "in_proj_lat",
    )(*args)


def _cache_prep_kernel(gk_ref, gv_ref, ckv_ref, kr_ref, wk_ref, wv_ref, place_ref, ka_o, vat_o, kc_o, vct_o):
    ka_o[0] = gk_ref[0, 0].astype(BF16)
    vat_o[0] = gv_ref[0, 0].T.astype(BF16)
    ckvb = ckv_ref[0, 0].astype(BF16)
    kc_o[0] = (_dot(ckvb, wk_ref[...]) + _dot(kr_ref[0, 0].astype(BF16), place_ref[...])).astype(BF16)
    vct_o[0] = _dot(ckvb, wv_ref[...]).T.astype(BF16)


def _cache_prep(cache_gqa_k, cache_gqa_v, cache_mla_ckv, cache_mla_krope, layer, lw):
    B, L, P = cache_gqa_k.shape[:3]
    gk = cache_gqa_k.reshape(B, L, P, GQA_KV_WIDTH)
    gv = cache_gqa_v.reshape(B, L, P, GQA_KV_WIDTH)
    lsel = lambda w: pl.BlockSpec((1, 1, P, w), lambda b: (b, layer, 0, 0))
    full = lambda a: pl.BlockSpec(a.shape, lambda b: (0,) * a.ndim)
    sds = jax.ShapeDtypeStruct
    return pl.pallas_call(
        _cache_prep_kernel,
        grid=(B,),
        in_specs=[lsel(GQA_KV_WIDTH), lsel(GQA_KV_WIDTH), lsel(MLA_KV_LORA), lsel(MLA_ROPE_DIM),
                  full(lw["w_uk"]), full(lw["w_uv"]), full(lw["kr_place"])],
        out_specs=[pl.BlockSpec((1, P, GQA_KV_WIDTH), lambda b: (b, 0, 0)),
                   pl.BlockSpec((1, GQA_KV_WIDTH, P), lambda b: (b, 0, 0)),
                   pl.BlockSpec((1, P, MLA_QK_WIDTH), lambda b: (b, 0, 0)),
                   pl.BlockSpec((1, MLA_WIDTH, P), lambda b: (b, 0, 0))],
        out_shape=[sds((B, P, GQA_KV_WIDTH), BF16), sds((B, GQA_KV_WIDTH, P), BF16),
                   sds((B, P, MLA_QK_WIDTH), BF16), sds((B, MLA_WIDTH, P), BF16)],
        compiler_params=pltpu.CompilerParams(dimension_semantics=("parallel",)),
        name="cache_prep",
    )(gk, gv, cache_mla_ckv, cache_mla_krope, lw["w_uk"], lw["w_uv"], lw["kr_place"])


def _attention_kernel(*refs, n_src, tk, nsub, rep, shared_kv):
    q_ref, o_ref, s_scr = refs[0], refs[1 + 2 * n_src], refs[2 + 2 * n_src]
    srcs = [(refs[1 + 2 * j], refs[2 + 2 * j]) for j in range(n_src)]
    nb = q_ref.shape[0]
    tqs = q_ref.shape[1] // nsub
    g = pl.program_id(1)

    chunks = []
    off = 0
    for k_ref, vt_ref in srcs:
        n = k_ref.shape[1]
        step = min(tk, n)
        for c in range(n // step):
            chunks.append((k_ref, vt_ref, c * step, step, off))
            off += step

    def scores(j):
        b, rows = j // nsub, slice((j % nsub) * tqs, (j % nsub + 1) * tqs)
        if shared_kv:
            hd = LANES // 2
            x = q_ref[b, rows, :].astype(F32)
            lane = lax.broadcasted_iota(jnp.int32, (tqs, LANES), 1)
            pieces = []
            for r in range(rep):
                win = x[:, (r // 2) * LANES:(r // 2 + 1) * LANES]
                win = jnp.where(g == r % 2, win, pltpu.roll(win, hd, 1))
                pieces.append(jnp.where(lane // hd == g, win, 0.0).astype(BF16))
            qcat = jnp.concatenate(pieces, axis=0)
        else:
            qcat = q_ref[b, rows, :]
        W = qcat.shape[0]
        m_acc = jnp.full((SUBLANES, W), -jnp.inf, F32)
        for k_ref, _, c0, step, off in chunks:
            s = _dot_nt(k_ref[b, c0:c0 + step, :], qcat)
            s_scr[j % SCORE_SLOTS, off:off + step, :] = s
            m_acc = jnp.maximum(m_acc, jnp.max(s.reshape(step // SUBLANES, SUBLANES, W), axis=0))
        return jnp.max(m_acc, axis=0, keepdims=True)

    def outputs(j, m):
        W = m.shape[1]
        b, rows = j // nsub, slice((j % nsub) * tqs, (j % nsub + 1) * tqs)
        l_acc = jnp.zeros((SUBLANES, W), F32)
        acc = jnp.zeros((LANES, W), F32)
        for _, vt_ref, c0, step, off in chunks:
            p = jnp.exp2(s_scr[j % SCORE_SLOTS, off:off + step, :] - m)
            l_acc = l_acc + jnp.sum(p.reshape(step // SUBLANES, SUBLANES, W), axis=0)
            acc = acc + _dot(vt_ref[b, :, c0:c0 + step], p.astype(BF16))
        ot = acc / jnp.sum(l_acc, axis=0, keepdims=True)
        if shared_kv:
            ot = jnp.where(g == 0, ot[:LANES // 2], ot[LANES // 2:])
            for pair in range(rep // 2):
                two = jnp.concatenate([ot[:, (2 * pair) * tqs:(2 * pair + 1) * tqs],
                                       ot[:, (2 * pair + 1) * tqs:(2 * pair + 2) * tqs]], axis=0)
                o_ref[b, rows, pair * LANES:(pair + 1) * LANES] = two.T.astype(o_ref.dtype)
        else:
            o_ref[b, rows, :] = ot.T.astype(o_ref.dtype)

    m_prev = scores(0)
    for j in range(1, nb * nsub):
        m_next = scores(j)
        outputs(j - 1, m_prev)
        m_prev = m_next
    outputs(nb * nsub - 1, m_prev)


def _attention(q, sources, *, n_groups, shared_kv, tq, nsub, tk, nb):
    B, N, Wq = q.shape
    G = n_groups
    tqs = tq // nsub
    tq = min(tq, N)
    nsub = tq // tqs
    qw = Wq // G
    rep = qw // (LANES // 2) if shared_kv else 1
    kv_blk = (lambda b, g, i: (b, 0, 0)) if shared_kv else (lambda b, g, i: (b, 0, g))
    vt_blk = (lambda b, g, i: (b, 0, 0)) if shared_kv else (lambda b, g, i: (b, g, 0))
    in_specs = [pl.BlockSpec((nb, tq, qw), lambda b, g, i: (b, i, g))]
    args = [q]
    nk = 0
    for k, vt in sources:
        n = k.shape[1]
        nk += n
        in_specs += [pl.BlockSpec((nb, n, LANES), kv_blk), pl.BlockSpec((nb, LANES, n), vt_blk)]
        args += [k, vt]
    ow = qw if shared_kv else LANES
    return pl.pallas_call(
        functools.partial(_attention_kernel, n_src=len(sources), tk=tk, nsub=nsub, rep=rep, shared_kv=shared_kv),
        grid=(B // nb, G, N // tq),
        in_specs=in_specs,
        out_specs=pl.BlockSpec((nb, tq, ow), lambda b, g, i: (b, i, g)),
        out_shape=jax.ShapeDtypeStruct((B, N, G * ow), BF16),
        scratch_shapes=[pltpu.VMEM((min(SCORE_SLOTS, nb * nsub), nk, rep * tq // nsub), F32)],
        compiler_params=pltpu.CompilerParams(dimension_semantics=("parallel", "parallel", "parallel"),
                                             vmem_limit_bytes=VMEM_LIMIT),
        name="attention_gqa" if shared_kv else "attention_mla",
    )(*args)


def _gla_kernel(*refs, nchunk, nb, has_state):
    it = iter(refs)
    qkf_ref, vf_ref, gf_ref, qkb_ref, vb_ref, gb_ref = (next(it) for _ in range(6))
    if has_state:
        s0f_ref, s0b_ref = next(it), next(it)
    of_ref, ob_ref, sf_ref, sb_ref, st_scr, bd_scr = (next(it) for _ in range(6))
    C, H, DK, DV, KW, VW = GLA_CHUNK, GLA_HEADS, GLA_DK, GLA_DV, GLA_K_WIDTH, GLA_WIDTH
    i = pl.program_id(1)

    @pl.when(i == 0)
    def _():
        bd_scr[...] = jnp.zeros(bd_scr.shape, BF16)
        for d, s0_ref in enumerate((s0f_ref, s0b_ref) if has_state else (None, None)):
            for b in range(nb):
                for h in range(H):
                    s0 = s0_ref[b, 0, h] if has_state else jnp.zeros((DK, DV), F32)
                    st_scr[d, b, h] = s0
                    bd_scr[d, b, h * DK:(h + 1) * DK, h * DV:(h + 1) * DV] = s0.astype(BF16)

    row = lax.broadcasted_iota(jnp.int32, (C, H * C), 0)
    col = lax.broadcasted_iota(jnp.int32, (C, H * C), 1) % C
    keep = (row >= col, row <= col)
    tok = lax.broadcasted_iota(jnp.int32, (C, KW), 0)
    k_head = lax.broadcasted_iota(jnp.int32, (C, KW), 1) // DK
    v_head = lax.broadcasted_iota(jnp.int32, (C, VW), 1) // DV
    dirs = ((qkf_ref, vf_ref, gf_ref, of_ref, 0), (qkb_ref, vb_ref, gb_ref, ob_ref, KW))

    def chain(d, q, k, v, g, st, bd):
        cum = g
        shift = 1
        while shift < C:
            if d == 0:
                cum = cum + jnp.where(tok >= shift, pltpu.roll(cum, shift, 0), 0.0)
            else:
                cum = cum + jnp.where(tok < C - shift, pltpu.roll(cum, C - shift, 0), 0.0)
            shift *= 2
        last = cum[C - 1:C, :] if d == 0 else cum[0:1, :]
        qd = q * jnp.exp(cum)
        kd = k * jnp.exp(last - cum)
        kd_stack = jnp.concatenate([jnp.where(k_head == h, kd, 0.0).astype(BF16) for h in range(H)], axis=0)
        a = _dot_nt((qd * jnp.exp(-last)).astype(BF16), kd_stack)
        a = jnp.where(keep[d], a, 0.0).astype(BF16)
        v_stack = jnp.concatenate([jnp.where(v_head == h, v, 0.0).astype(BF16) for h in range(H)], axis=0)
        o = _dot(qd.astype(BF16), bd) + _dot(a, v_stack)
        kdt = kd.T.astype(BF16)
        decay = jnp.exp(jnp.broadcast_to(last, (LANES, KW)).T)
        vb = v.astype(BF16)
        s_new = [st[h] * decay[h * DK:(h + 1) * DK, :] + _dot(kdt[h * DK:(h + 1) * DK, :], vb[:, h * DV:(h + 1) * DV])
                 for h in range(H)]
        return o, s_new

    def block_diag(states):
        zero = jnp.zeros((DK, DV), BF16)
        return jnp.concatenate(
            [jnp.concatenate([states[h].astype(BF16) if j == h else zero for j in range(H)], axis=1)
             for h in range(H)], axis=0)

    def body(it, carry):
        work = []
        for d, (qk_ref, v_ref, g_ref, o_ref, g0) in enumerate(dirs):
            for b in range(nb):
                loads = []
                for u in range(per_iter):
                    c = it * per_iter + u
                    cc = c if d == 0 else nchunk - 1 - c
                    rows = pl.ds(pl.multiple_of(cc * C, C), C)
                    loads.append((rows, qk_ref[b, rows, :KW], qk_ref[b, rows, KW:], v_ref[b, rows, :],
                                  g_ref[b, rows, g0:g0 + KW]))
                work.append((d, b, o_ref, loads, [st_scr[d, b, h] for h in range(H)], bd_scr[d, b]))
        done = []
        for d, b, o_ref, loads, st, bd in work:
            outs = []
            for u, (rows, q, k, v, g) in enumerate(loads):
                o, st = chain(d, q, k, v, g, st, bd)
                outs.append((rows, o))
                if u + 1 < per_iter:
                    bd = block_diag(st)
            done.append((d, b, o_ref, outs, st))
        for d, b, o_ref, outs, st in done:
            for rows, o in outs:
                o_ref[b, rows, :] = o.astype(o_ref.dtype)
            for h in range(H):
                st_scr[d, b, h] = st[h]
                bd_scr[d, b, h * DK:(h + 1) * DK, h * DV:(h + 1) * DV] = st[h].astype(BF16)
        return carry

    per_iter = 4 if nchunk % 4 == 0 else 1
    lax.fori_loop(0, nchunk // per_iter, body, 0)

    @pl.when(i == pl.num_programs(1) - 1)
    def _():
        for d, s_ref in enumerate((sf_ref, sb_ref)):
            for b in range(nb):
                s_ref[b] = st_scr[d, b]


def _gla(qkg, vg, la, state_fwd, state_bwd, layer, *, tb, nb):
    B, N, _ = qkg.shape
    H, DK, DV = GLA_HEADS, GLA_DK, GLA_DV
    nblk = N // tb
    fwd = lambda w: pl.BlockSpec((nb, tb, w), lambda b, i: (b, i, 0))
    bwd = lambda w: pl.BlockSpec((nb, tb, w), lambda b, i: (b, nblk - 1 - i, 0))
    in_specs = [fwd(2 * GLA_K_WIDTH), fwd(GLA_WIDTH), fwd(2 * GLA_K_WIDTH),
                bwd(2 * GLA_K_WIDTH), bwd(GLA_WIDTH), bwd(2 * GLA_K_WIDTH)]
    args = [qkg, vg, la, qkg, vg, la]
    has_state = state_fwd is not None
    if has_state:
        s0 = pl.BlockSpec((nb, 1, H, DK, DV), lambda b, i: (b, layer, 0, 0, 0))
        in_specs += [s0, s0]
        args += [state_fwd, state_bwd]
    st = pl.BlockSpec((nb, H, DK, DV), lambda b, i: (b, 0, 0, 0))
    sds = jax.ShapeDtypeStruct
    return pl.pallas_call(
        functools.partial(_gla_kernel, nchunk=tb // GLA_CHUNK, nb=nb, has_state=has_state),
        grid=(B // nb, nblk),
        in_specs=in_specs,
        out_specs=[fwd(GLA_WIDTH), bwd(GLA_WIDTH), st, st],
        out_shape=[sds((B, N, GLA_WIDTH), BF16), sds((B, N, GLA_WIDTH), BF16),
                   sds((B, H, DK, DV), F32), sds((B, H, DK, DV), F32)],
        scratch_shapes=[pltpu.VMEM((2, nb, H, DK, DV), F32), pltpu.VMEM((2, nb, GLA_K_WIDTH, GLA_WIDTH), BF16)],
        compiler_params=pltpu.CompilerParams(dimension_semantics=("parallel", "arbitrary"),
                                             vmem_limit_bytes=VMEM_LIMIT),
        name="gla",
    )(*args)


def _out_proj_kernel(x_ref, mod_ref, ya_ref, ga_ref, of_ref, ob_ref, gg_ref, yc_ref, gc_ref,
                     gpre_ref, wm_ref, woa_ref, wog_ref, woc_ref, wout_ref, ggla_ref, gpost_ref,
                     o_ref, *, d_model):
    D = d_model
    zm = _dot_nt(_modulated_input(x_ref[0], mod_ref[0], gpre_ref[...], D), wm_ref[0])
    ya = _dot(ya_ref[0] * ga_ref[0], woa_ref[...])
    og = of_ref[0].astype(F32) + ob_ref[0].astype(F32)
    gg = gg_ref[0]
    heads = []
    for h in range(GLA_HEADS):
        sl = slice(h * GLA_DV, (h + 1) * GLA_DV)
        heads.append((_rms(og[:, sl], ggla_ref[...]) * gg[:, sl]).astype(BF16))
    yb = _dot(jnp.concatenate(heads, axis=-1), wog_ref[...])
    yc = _dot(yc_ref[0] * gc_ref[0], woc_ref[...])
    merged = (_sigmoid(zm[:, :D]) * ya + _sigmoid(zm[:, D:2 * D]) * yb + _sigmoid(zm[:, 2 * D:]) * yc)
    out = _rms(_dot(merged.astype(BF16), wout_ref[...]), gpost_ref[...])
    o_ref[0] = x_ref[0] + mod_ref[0][:, 2 * D:] * out


def _out_proj(x, mod, ya, ga, o_f, o_b, gg, yc, gc, w_merge, layer, lw, *, tm):
    B, N, D = x.shape
    per_batch_mod = mod.shape[0] > 1
    row3 = lambda w: pl.BlockSpec((1, tm, w), lambda b, i: (b, i, 0))
    consts = [lw["w_o_gqa"], lw["w_o_gla"], lw["w_o_mla"], lw["w_out"], lw["g_gla_out"], lw["g_post"]]
    in_specs = [row3(D),
                pl.BlockSpec((1, 1, 3 * D), (lambda b, i: (b, 0, 0)) if per_batch_mod else (lambda b, i: (0, 0, 0))),
                row3(GQA_WIDTH), row3(GQA_WIDTH), row3(GLA_WIDTH), row3(GLA_WIDTH), row3(GLA_WIDTH),
                row3(MLA_WIDTH), row3(MLA_WIDTH),
                _const_spec(lw["g_pre"].shape),
                pl.BlockSpec((1, 3 * D, D), lambda b, i: (layer, 0, 0), pipeline_mode=pl.Buffered(1))]
    in_specs += [_const_spec(c.shape) for c in consts]
    return pl.pallas_call(
        functools.partial(_out_proj_kernel, d_model=D),
        grid=(B, N // tm),
        in_specs=in_specs,
        out_specs=row3(D),
        out_shape=jax.ShapeDtypeStruct((B, N, D), F32),
        compiler_params=pltpu.CompilerParams(dimension_semantics=("parallel", "parallel"),
                                             vmem_limit_bytes=VMEM_LIMIT),
        name="out_proj",
    )(x, mod, ya, ga, o_f, o_b, gg, yc, gc, lw["g_pre"], w_merge, *consts)


def _rope_tables(n_tokens):
    t = np.arange(n_tokens)
    pos = np.stack([t // GRID_W, t % GRID_W], axis=0).astype(np.float64)

    def tables(lane_part, lane_in_part, half, active):
        freqs = ROPE_THETA ** (-(lane_in_part % half).astype(np.float64) / half)
        ang = pos[lane_part].T * freqs[None, :]
        cos = np.where(active[None, :], np.cos(ang), 1.0)
        sin = np.where(active[None, :], np.where(lane_in_part < half, -1.0, 1.0)[None, :] * np.sin(ang), 0.0)
        return jnp.asarray(cos, F32), jnp.asarray(sin, F32)

    lane = np.arange(LANES)
    ja = lane % GQA_HEAD_DIM
    ca, sa = tables(ja // (GQA_HEAD_DIM // 2), ja % (GQA_HEAD_DIM // 2), GQA_HEAD_DIM // 4, np.ones(LANES, bool))
    active = (lane >= MLA_ROPE_LANE) & (lane < MLA_ROPE_LANE + MLA_ROPE_DIM)
    jc = np.where(active, lane - MLA_ROPE_LANE, 0)
    cc, sc = tables(jc // (MLA_ROPE_DIM // 2), jc % (MLA_ROPE_DIM // 2), MLA_ROPE_DIM // 4, active)
    return ca, sa, cc, sc


def _pack_layer(l, p):
    w_dec = jnp.zeros((LANES, 2 * GLA_K_WIDTH), F32)
    w_dec = w_dec.at[:GLA_RANK, :GLA_K_WIDTH].set(p["w_gla_decay_fwd"][l])
    w_dec = w_dec.at[GLA_RANK:2 * GLA_RANK, GLA_K_WIDTH:].set(p["w_gla_decay_bwd"][l])
    b_dec = jnp.concatenate([p["b_gla_decay_fwd"][l], p["b_gla_decay_bwd"][l]])[None, :]
    pad_q = MLA_QK_PAD - MLA_NOPE_DIM - MLA_ROPE_DIM
    w_uq = p["w_mla_uq"][l].reshape(MLA_Q_LORA, MLA_HEADS, MLA_NOPE_DIM + MLA_ROPE_DIM)
    w_uq = jnp.pad(w_uq, ((0, 0), (0, 0), (0, pad_q))).reshape(MLA_Q_LORA, MLA_QK_WIDTH)
    w_ukv = p["w_mla_ukv"][l].reshape(MLA_KV_LORA, MLA_HEADS, MLA_NOPE_DIM + MLA_V_DIM)
    w_uk = jnp.pad(w_ukv[:, :, :MLA_NOPE_DIM], ((0, 0), (0, 0), (0, MLA_QK_PAD - MLA_NOPE_DIM)))
    w_uk = w_uk.reshape(MLA_KV_LORA, MLA_QK_WIDTH)
    w_uv = w_ukv[:, :, MLA_NOPE_DIM:].reshape(MLA_KV_LORA, MLA_WIDTH)
    seg = np.arange(GQA_WIDTH) // GQA_HEAD_DIM
    place = np.zeros((MLA_ROPE_DIM, MLA_QK_WIDTH), np.float32)
    for h in range(MLA_HEADS):
        place[np.arange(MLA_ROPE_DIM), h * MLA_QK_PAD + MLA_ROPE_LANE + np.arange(MLA_ROPE_DIM)] = 1.0
    return {
        "g_pre": p["g_pre"][l][None, :], "g_post": p["g_post"][l][None, :],
        "g_q": jnp.tile(p["g_q_norm"][l], GQA_HEADS)[None, :],
        "g_k": jnp.tile(p["g_k_norm"][l], GQA_KV_HEADS)[None, :],
        "ones": jnp.asarray(seg[:, None] == seg[None, :], BF16),
        "w_dec": w_dec.astype(BF16), "b_dec": b_dec,
        "g_mla_q": p["g_mla_q"][l][None, :], "g_mla_kv": p["g_mla_kv"][l][None, :],
        "w_uq": w_uq.astype(BF16), "w_uk": w_uk.astype(BF16), "w_uv": w_uv.astype(BF16),
        "kr_place": jnp.asarray(place, BF16),
        "w_o_gqa": p["w_o_gqa"][l].astype(BF16), "w_o_gla": p["w_o_gla"][l].astype(BF16),
        "w_o_mla": p["w_o_mla"][l].astype(BF16), "w_out": p["w_out"][l].astype(BF16),
        "g_gla_out": p["g_gla_out"][l][None, :],
    }


def _tiles(batch, seq):
    if seq <= 256:
        nb = 4 if batch % 4 == 0 else 1
        return dict(tm_in=seq, tm_out=seq, att_gqa=(seq, seq // 128, nb), att_mla=(seq, 1, nb), tk=512,
                    gla_tb=seq, gla_nb=2 if batch % 2 == 0 else 1)
    return dict(tm_in=512, tm_out=512, att_gqa=(1024, 16, 1), att_mla=(4096, 16, 1), tk=512,
                gla_tb=512, gla_nb=2 if batch % 2 == 0 else 1)


def _sub_layer(x, mod, w_packed, layer, lw, rope, cache):
    is_ctx = cache is None
    t = _tiles(x.shape[0], x.shape[1])
    outs = _in_proj(x, mod, w_packed[0], w_packed[1], layer, lw, rope, is_ctx=is_ctx, tm=t["tm_in"])
    (qa, ka, vat, ga, qkg, vg, gg, la, qc, kc, vct, gc) = outs[:12]
    src_a, src_c = [(ka, vat)], [(kc, vct)]
    s0f = s0b = None
    if not is_ctx:
        ka_p, vat_p, kc_p, vct_p = _cache_prep(cache["gqa_k"], cache["gqa_v"], cache["mla_ckv"],
                                               cache["mla_krope"], layer, lw)
        src_a, src_c = [(ka_p, vat_p)] + src_a, [(kc_p, vct_p)] + src_c
        s0f, s0b = cache["gla_fwd"], cache["gla_bwd"]
    (tq_a, nsub_a, nb_a), (tq_c, nsub_c, nb_c) = t["att_gqa"], t["att_mla"]
    ya = _attention(qa, src_a, n_groups=GQA_KV_HEADS, shared_kv=True, tq=tq_a, nsub=nsub_a, nb=nb_a, tk=t["tk"])
    yc = _attention(qc, src_c, n_groups=MLA_HEADS, shared_kv=False, tq=tq_c, nsub=nsub_c, nb=nb_c, tk=t["tk"])
    o_f, o_b, s_f, s_b = _gla(qkg, vg, la, s0f, s0b, layer, tb=t["gla_tb"], nb=t["gla_nb"])
    y = _out_proj(x, mod, ya, ga, o_f, o_b, gg, yc, gc, w_packed[2], layer, lw, tm=t["tm_out"])
    ctx = (outs[12], outs[13], outs[14], outs[15], s_f, s_b) if is_ctx else None
    return y, ctx


def kernel(x_prompt, x_sample, cache_gqa_k, cache_gqa_v, cache_mla_ckv, cache_mla_krope, state_gla_fwd, state_gla_bwd, c, c_ctx, w_mod, b_mod, g_pre, g_post, w_in, g_q_norm, g_k_norm, w_gla_decay_fwd, b_gla_decay_fwd, w_gla_decay_bwd, b_gla_decay_bwd, g_gla_out, g_mla_q, g_mla_kv, w_mla_uq, w_mla_ukv, w_o_gqa, w_o_gla, w_o_mla, w_out):
    params = dict(g_pre=g_pre, g_post=g_post, g_q_norm=g_q_norm, g_k_norm=g_k_norm,
                  w_gla_decay_fwd=w_gla_decay_fwd, b_gla_decay_fwd=b_gla_decay_fwd,
                  w_gla_decay_bwd=w_gla_decay_bwd, b_gla_decay_bwd=b_gla_decay_bwd, g_gla_out=g_gla_out,
                  g_mla_q=g_mla_q, g_mla_kv=g_mla_kv, w_mla_uq=w_mla_uq, w_mla_ukv=w_mla_ukv,
                  w_o_gqa=w_o_gqa, w_o_gla=w_o_gla, w_o_mla=w_o_mla, w_out=w_out)
    depth, D = w_in.shape[0], w_in.shape[1]
    B, S = x_prompt.shape[:2]
    Bd, Nd = x_sample.shape[:2]

    conds = jnp.concatenate([c_ctx[None, :], c, jnp.zeros((SUBLANES - 1 - Bd, D), F32)], axis=0)
    mods = _modulation(conds, w_mod, b_mod)
    rope = _rope_tables(Nd)
    w_t = jnp.swapaxes(w_in, 1, 2).astype(BF16)
    zrow = lambda n: jnp.zeros((depth, n, D), BF16)
    kr_rows = w_t[:, _R_KR:_R_GC]
    w_small = jnp.concatenate(
        [w_t[:, _R_R:_R_LAT], zrow(LANES - 2 * GLA_RANK),
         kr_rows, zrow(MLA_ROPE_LANE - MLA_ROPE_DIM), kr_rows, zrow(LANES - MLA_ROPE_LANE - MLA_ROPE_DIM)], axis=1)
    w_packed = (w_t, w_small, w_t[:, _R_M:])
    layers = [_pack_layer(l, params) for l in range(depth)]

    xp = x_prompt
    ctx_out = []
    for l in range(depth):
        xp, ctx = _sub_layer(xp, mods[l, 0:1][:, None, :], w_packed, l, layers[l], None, None)
        ctx_out.append(ctx)

    xs = x_sample
    cache = {"gqa_k": cache_gqa_k, "gqa_v": cache_gqa_v, "mla_ckv": cache_mla_ckv, "mla_krope": cache_mla_krope,
             "gla_fwd": state_gla_fwd, "gla_bwd": state_gla_bwd}
    for l in range(depth):
        xs, _ = _sub_layer(xs, mods[l, 1:1 + Bd][:, None, :], w_packed, l, layers[l], rope, cache)

    stack = lambda j: jnp.stack([ctx_out[l][j] for l in range(depth)], axis=1)
    new_k = stack(0).reshape(B, depth, S, GQA_KV_HEADS, GQA_HEAD_DIM)
    new_v = stack(1).reshape(B, depth, S, GQA_KV_HEADS, GQA_HEAD_DIM)
    return (xp, xs, new_k, new_v, stack(2), stack(3), stack(4), stack(5))
```

```python
import functools

import numpy as np
import jax
import jax.numpy as jnp
from jax import lax
from jax.experimental import pallas as pl
from jax.experimental.pallas import tpu as pltpu

F32 = jnp.float32
BF16 = jnp.bfloat16

EPS = 1e-6
LOG2_E = 1.4426950408889634
ROPE_THETA = 10000.0
GRID_W = 64

GQA_HEADS, GQA_KV_HEADS, GQA_HEAD_DIM = 8, 2, 64
GQA_REP = GQA_HEADS // GQA_KV_HEADS
GQA_WIDTH = GQA_HEADS * GQA_HEAD_DIM
GQA_KV_WIDTH = GQA_KV_HEADS * GQA_HEAD_DIM
GLA_HEADS, GLA_DK, GLA_DV = 4, 64, 128
GLA_WIDTH = GLA_HEADS * GLA_DV
GLA_K_WIDTH = GLA_HEADS * GLA_DK
GLA_RANK = 16
GLA_NORMALIZER = 16.0
GLA_CHUNK = 64
MLA_HEADS, MLA_Q_LORA, MLA_KV_LORA = 4, 256, 256
MLA_NOPE_DIM, MLA_ROPE_DIM, MLA_V_DIM = 64, 32, 128
MLA_WIDTH = MLA_HEADS * MLA_V_DIM
MLA_QK_PAD = 128
MLA_QK_WIDTH = MLA_HEADS * MLA_QK_PAD
MLA_ROPE_LANE = MLA_NOPE_DIM

LANES = 128
SUBLANES = 8
BF16_SUBLANES = 16
SCORE_SLOTS = 3
VMEM_LIMIT = 56 * 1024 * 1024

assert GQA_KV_WIDTH == LANES and 2 * GQA_HEAD_DIM == LANES and MLA_V_DIM == LANES


def _sigmoid(x):
    return 1.0 / (1.0 + jnp.exp(-x))


def _silu(x):
    return x * _sigmoid(x)


def _log_sigmoid(x):
    return jnp.minimum(x, 0.0) - jnp.log(1.0 + jnp.exp(-jnp.abs(x)))


def _dot(a, b):
    return jnp.dot(a, b, preferred_element_type=F32)


def _dot_nt(a, b):
    return lax.dot_general(a, b, (((1,), (1,)), ((), ())), preferred_element_type=F32)


def _segment_mean_sq(x, ones_bf16, width):
    sq = x * x
    hi = sq.astype(BF16)
    lo = (sq - hi.astype(F32)).astype(BF16)
    return (_dot(hi, ones_bf16) + _dot(lo, ones_bf16)) * (1.0 / width)


def _swap_halves(x, half):
    n = x.shape[-1]
    lane = lax.broadcasted_iota(jnp.int32, x.shape, x.ndim - 1)
    first = (lane % (2 * half)) < half
    return jnp.where(first, pltpu.roll(x, n - half, x.ndim - 1), pltpu.roll(x, half, x.ndim - 1))


def _rope(x, cos, sin_signed, half):
    return x * cos + _swap_halves(x, half) * sin_signed


def _tile_lanes(t, reps):
    return t if reps == 1 else jnp.concatenate([t] * reps, axis=-1)


def _rms(x, gain):
    return x * lax.rsqrt(jnp.mean(x * x, axis=-1, keepdims=True) + EPS) * gain


def _mod_kernel(c_ref, w_ref, b_ref, o_ref):
    c = c_ref[...]
    o_ref[0] = _dot(_silu(c).astype(BF16), w_ref[0].astype(BF16)) + b_ref[0]


def _modulation(conds, w_mod, b_mod):
    L, D, D3 = w_mod.shape
    nj = D3 // D
    return pl.pallas_call(
        _mod_kernel,
        grid=(L, nj),
        in_specs=[pl.BlockSpec((SUBLANES, D), lambda l, j: (0, 0)),
                  pl.BlockSpec((1, D, D), lambda l, j: (l, 0, j)),
                  pl.BlockSpec((1, 1, D), lambda l, j: (l, 0, j))],
        out_specs=pl.BlockSpec((1, SUBLANES, D), lambda l, j: (l, 0, j)),
        out_shape=jax.ShapeDtypeStruct((L, SUBLANES, D3), F32),
        compiler_params=pltpu.CompilerParams(dimension_semantics=("parallel", "parallel")),
        name="modulation",
    )(conds, w_mod, b_mod.reshape(L, 1, D3))


_C_QA = 0
_C_KV = _C_QA + GQA_WIDTH
_C_GA = _C_KV + 2 * GQA_KV_WIDTH
_C_QKG = _C_GA + GQA_WIDTH
_C_VG = _C_QKG + 2 * GLA_K_WIDTH
_C_GG = _C_VG + GLA_WIDTH
_R_R = _C_GG + GLA_WIDTH
_R_LAT = _R_R + 2 * GLA_RANK
_R_KR = _R_LAT + MLA_Q_LORA + MLA_KV_LORA
_R_GC = _R_KR + MLA_ROPE_DIM
_R_M = _R_GC + MLA_WIDTH
assert _R_R % LANES == 0 and _R_LAT % BF16_SUBLANES == 0 and _R_GC % BF16_SUBLANES == 0


def _modulated_input(x, mod, g_pre, d_model):
    shift, scale = mod[:, :d_model], mod[:, d_model:2 * d_model]
    return (_rms(x, g_pre) * (1.0 + scale) + shift).astype(BF16)


def _in_proj_kernel(*refs, is_ctx, d_model):
    it = iter(refs)
    x_ref, mod_ref, gpre_ref, wt_ref, ws_ref, gq_ref, gk_ref, ones_ref, wdec_ref, bdec_ref = (next(it) for _ in range(10))
    gmq_ref, gmkv_ref, wuq_ref, wk_ref, wv_ref = (next(it) for _ in range(5))
    if not is_ctx:
        ca_ref, sa_ref, cc_ref, sc_ref = (next(it) for _ in range(4))
    (qa_o, ka_o, vat_o, ga_o, qkg_o, vg_o, gg_o, la_o,
     qc_o, kc_o, vct_o, gc_o) = (next(it) for _ in range(12))
    if is_ctx:
        ka32_o, va32_o, ckv32_o, kr32_o = (next(it) for _ in range(4))

    D = d_model
    hb = _modulated_input(x_ref[0], mod_ref[0], gpre_ref[...], D)

    z_head = _dot_nt(hb, wt_ref[0, :_R_R, :])
    z_lat = _dot_nt(hb, wt_ref[0, _R_LAT:_R_KR, :])
    z_gc = _dot_nt(hb, wt_ref[0, _R_GC:_R_M, :])
    z_small = _dot_nt(hb, ws_ref[0])

    def head(c0, width):
        return z_head[:, c0:c0 + width]

    ones = ones_ref[...]

    qa = head(_C_QA, GQA_WIDTH)
    qa = qa * lax.rsqrt(_segment_mean_sq(qa, ones, GQA_HEAD_DIM) + EPS) * gq_ref[...]
    kv = head(_C_KV, 2 * GQA_KV_WIDTH)
    ka, va = kv[:, :GQA_KV_WIDTH], kv[:, GQA_KV_WIDTH:]
    ka = ka * lax.rsqrt(_segment_mean_sq(ka, ones[:GQA_KV_WIDTH, :GQA_KV_WIDTH], GQA_HEAD_DIM) + EPS) * gk_ref[...]
    if is_ctx:
        ka32_o[0] = ka
        va32_o[0] = va
    else:
        ca, sa = ca_ref[...], sa_ref[...]
        qa = _rope(qa, _tile_lanes(ca, GQA_WIDTH // LANES), _tile_lanes(sa, GQA_WIDTH // LANES), GQA_HEAD_DIM // 4)
        ka = _rope(ka, ca, sa, GQA_HEAD_DIM // 4)
    qa_o[0] = (qa * (GQA_HEAD_DIM ** -0.5 * LOG2_E)).astype(BF16)
    ka_o[0] = ka.astype(BF16)
    vat_o[0] = va.T.astype(BF16)
    ga_o[0] = _silu(head(_C_GA, GQA_WIDTH)).astype(BF16)

    qkg = head(_C_QKG, 2 * GLA_K_WIDTH)
    lane = lax.broadcasted_iota(jnp.int32, qkg.shape, 1)
    qkg_o[0] = jnp.where(lane < GLA_K_WIDTH, qkg * GLA_DK ** -0.5, qkg)
    vg_o[0] = head(_C_VG, GLA_WIDTH)
    r = z_small[:, :LANES].astype(BF16)
    la_o[0] = _log_sigmoid(_dot(r, wdec_ref[...]) + bdec_ref[...]) * (1.0 / GLA_NORMALIZER)
    gg_o[0] = _silu(head(_C_GG, GLA_WIDTH)).astype(BF16)

    lat = z_lat
    ql = _rms(lat[:, :MLA_Q_LORA], gmq_ref[...])
    ckv = _rms(lat[:, MLA_Q_LORA:], gmkv_ref[...])
    qc = _dot(ql.astype(BF16), wuq_ref[...])
    krb = z_small[:, LANES:]
    if is_ctx:
        ckv32_o[0] = ckv
        kr32_o[0] = krb[:, :MLA_ROPE_DIM]
    else:
        cc, sc = cc_ref[...], sc_ref[...]
        qc = _rope(qc, _tile_lanes(cc, MLA_HEADS), _tile_lanes(sc, MLA_HEADS), MLA_ROPE_DIM // 4)
        krb = _rope(krb, cc, sc, MLA_ROPE_DIM // 4)
    lane = lax.broadcasted_iota(jnp.int32, krb.shape, 1)
    kr_part = jnp.where((lane >= MLA_ROPE_LANE) & (lane < MLA_ROPE_LANE + MLA_ROPE_DIM), krb, 0.0)
    qc_o[0] = (qc * ((MLA_NOPE_DIM + MLA_ROPE_DIM) ** -0.5 * LOG2_E)).astype(BF16)
    ckvb = ckv.astype(BF16)
    kc_o[0] = (_dot(ckvb, wk_ref[...]) + _tile_lanes(kr_part, MLA_HEADS)).astype(BF16)
    vct_o[0] = _dot(ckvb, wv_ref[...]).T.astype(BF16)
    gc_o[0] = _silu(z_gc).astype(BF16)


def _const_spec(shape):
    nd = len(shape)
    return pl.BlockSpec(shape, lambda b, i, _nd=nd: (0,) * _nd, pipeline_mode=pl.Buffered(1))


def _in_proj(x, mod, w_packed, w_small, layer, lw, rope, *, is_ctx, tm):
    B, N, D = x.shape
    assert w_packed.shape[1] == _R_M + 3 * D
    per_batch_mod = mod.shape[0] > 1
    row3 = lambda w: pl.BlockSpec((1, tm, w), lambda b, i: (b, i, 0))
    col3 = lambda h: pl.BlockSpec((1, h, tm), lambda b, i: (b, 0, i))
    consts = [lw["g_q"], lw["g_k"], lw["ones"], lw["w_dec"], lw["b_dec"],
              lw["g_mla_q"], lw["g_mla_kv"], lw["w_uq"], lw["w_uk"], lw["w_uv"]]
    in_specs = [row3(D),
                pl.BlockSpec((1, 1, 3 * D), (lambda b, i: (b, 0, 0)) if per_batch_mod else (lambda b, i: (0, 0, 0))),
                _const_spec(lw["g_pre"].shape),
                pl.BlockSpec((1, _R_M, D), lambda b, i: (layer, 0, 0), pipeline_mode=pl.Buffered(1)),
                pl.BlockSpec((1, 2 * LANES, D), lambda b, i: (layer, 0, 0), pipeline_mode=pl.Buffered(1))]
    in_specs += [_const_spec(c.shape) for c in consts]
    args = [x, mod, lw["g_pre"], w_packed, w_small] + consts
    if not is_ctx:
        in_specs += [pl.BlockSpec((tm, LANES), lambda b, i: (i, 0))] * 4
        args += list(rope)
    sds = jax.ShapeDtypeStruct
    out_shape = [
        sds((B, N, GQA_WIDTH), BF16), sds((B, N, GQA_KV_WIDTH), BF16), sds((B, GQA_KV_WIDTH, N), BF16),
        sds((B, N, GQA_WIDTH), BF16),
        sds((B, N, 2 * GLA_K_WIDTH), F32), sds((B, N, GLA_WIDTH), F32), sds((B, N, GLA_WIDTH), BF16),
        sds((B, N, 2 * GLA_K_WIDTH), F32),
        sds((B, N, MLA_QK_WIDTH), BF16), sds((B, N, MLA_QK_WIDTH), BF16), sds((B, MLA_WIDTH, N), BF16),
        sds((B, N, MLA_WIDTH), BF16)]
    out_specs = [
        row3(GQA_WIDTH), row3(GQA_KV_WIDTH), col3(GQA_KV_WIDTH), row3(GQA_WIDTH),
        row3(2 * GLA_K_WIDTH), row3(GLA_WIDTH), row3(GLA_WIDTH), row3(2 * GLA_K_WIDTH),
        row3(MLA_QK_WIDTH), row3(MLA_QK_WIDTH), col3(MLA_WIDTH), row3(MLA_WIDTH)]
    if is_ctx:
        out_shape += [sds((B, N, GQA_KV_WIDTH), F32), sds((B, N, GQA_KV_WIDTH), F32),
                      sds((B, N, MLA_KV_LORA), F32), sds((B, N, MLA_ROPE_DIM), F32)]
        out_specs += [row3(GQA_KV_WIDTH), row3(GQA_KV_WIDTH), row3(MLA_KV_LORA), row3(MLA_ROPE_DIM)]
    return pl.pallas_call(
        functools.partial(_in_proj_kernel, is_ctx=is_ctx, d_model=D),
        grid=(B, N // tm),
        in_specs=in_specs, out_specs=out_specs, out_shape=out_shape,
        compiler_params=pltpu.CompilerParams(dimension_semantics=("parallel", "parallel"),
                                             vmem_limit_bytes=VMEM_LIMIT),
        name="in_proj_ctx" if is_ctx else "in_proj_lat",
    )(*args)


def _cache_prep_kernel(gk_ref, gv_ref, ckv_ref, kr_ref, wk_ref, wv_ref, place_ref, ka_o, vat_o, kc_o, vct_o):
    ka_o[0] = gk_ref[0, 0].astype(BF16)
    vat_o[0] = gv_ref[0, 0].T.astype(BF16)
    ckvb = ckv_ref[0, 0].astype(BF16)
    kc_o[0] = (_dot(ckvb, wk_ref[...]) + _dot(kr_ref[0, 0].astype(BF16), place_ref[...])).astype(BF16)
    vct_o[0] = _dot(ckvb, wv_ref[...]).T.astype(BF16)


def _cache_prep(cache_gqa_k, cache_gqa_v, cache_mla_ckv, cache_mla_krope, layer, lw):
    B, L, P = cache_gqa_k.shape[:3]
    gk = cache_gqa_k.reshape(B, L, P, GQA_KV_WIDTH)
    gv = cache_gqa_v.reshape(B, L, P, GQA_KV_WIDTH)
    lsel = lambda w: pl.BlockSpec((1, 1, P, w), lambda b: (b, layer, 0, 0))
    full = lambda a: pl.BlockSpec(a.shape, lambda b: (0,) * a.ndim)
    sds = jax.ShapeDtypeStruct
    return pl.pallas_call(
        _cache_prep_kernel,
        grid=(B,),
        in_specs=[lsel(GQA_KV_WIDTH), lsel(GQA_KV_WIDTH), lsel(MLA_KV_LORA), lsel(MLA_ROPE_DIM),
                  full(lw["w_uk"]), full(lw["w_uv"]), full(lw["kr_place"])],
        out_specs=[pl.BlockSpec((1, P, GQA_KV_WIDTH), lambda b: (b, 0, 0)),
                   pl.BlockSpec((1, GQA_KV_WIDTH, P), lambda b: (b, 0, 0)),
                   pl.BlockSpec((1, P, MLA_QK_WIDTH), lambda b: (b, 0, 0)),
                   pl.BlockSpec((1, MLA_WIDTH, P), lambda b: (b, 0, 0))],
        out_shape=[sds((B, P, GQA_KV_WIDTH), BF16), sds((B, GQA_KV_WIDTH, P), BF16),
                   sds((B, P, MLA_QK_WIDTH), BF16), sds((B, MLA_WIDTH, P), BF16)],
        compiler_params=pltpu.CompilerParams(dimension_semantics=("parallel",)),
        name="cache_prep",
    )(gk, gv, cache_mla_ckv, cache_mla_krope, lw["w_uk"], lw["w_uv"], lw["kr_place"])


def _attention_kernel(*refs, n_src, tk, nsub, rep, shared_kv):
    q_ref, o_ref, s_scr = refs[0], refs[1 + 2 * n_src], refs[2 + 2 * n_src]
    srcs = [(refs[1 + 2 * j], refs[2 + 2 * j]) for j in range(n_src)]
    nb = q_ref.shape[0]
    tqs = q_ref.shape[1] // nsub
    g = pl.program_id(1)

    chunks = []
    off = 0
    for k_ref, vt_ref in srcs:
        n = k_ref.shape[1]
        step = min(tk, n)
        for c in range(n // step):
            chunks.append((k_ref, vt_ref, c * step, step, off))
            off += step

    def scores(j):
        b, rows = j // nsub, slice((j % nsub) * tqs, (j % nsub + 1) * tqs)
        if shared_kv:
            hd = LANES // 2
            x = q_ref[b, rows, :].astype(F32)
            lane = lax.broadcasted_iota(jnp.int32, (tqs, LANES), 1)
            pieces = []
            for r in range(rep):
                win = x[:, (r // 2) * LANES:(r // 2 + 1) * LANES]
                win = jnp.where(g == r % 2, win, pltpu.roll(win, hd, 1))
                pieces.append(jnp.where(lane // hd == g, win, 0.0).astype(BF16))
            qcat = jnp.concatenate(pieces, axis=0)
        else:
            qcat = q_ref[b, rows, :]
        W = qcat.shape[0]
        m_acc = jnp.full((SUBLANES, W), -jnp.inf, F32)
        for k_ref, _, c0, step, off in chunks:
            s = _dot_nt(k_ref[b, c0:c0 + step, :], qcat)
            s_scr[j % SCORE_SLOTS, off:off + step, :] = s
            m_acc = jnp.maximum(m_acc, jnp.max(s.reshape(step // SUBLANES, SUBLANES, W), axis=0))
        return jnp.max(m_acc, axis=0, keepdims=True)

    def outputs(j, m):
        W = m.shape[1]
        b, rows = j // nsub, slice((j % nsub) * tqs, (j % nsub + 1) * tqs)
        l_acc = jnp.zeros((SUBLANES, W), F32)
        acc = jnp.zeros((LANES, W), F32)
        for _, vt_ref, c0, step, off in chunks:
            p = jnp.exp2(s_scr[j % SCORE_SLOTS, off:off + step, :] - m)
            l_acc = l_acc + jnp.sum(p.reshape(step // SUBLANES, SUBLANES, W), axis=0)
            acc = acc + _dot(vt_ref[b, :, c0:c0 + step], p.astype(BF16))
        ot = acc / jnp.sum(l_acc, axis=0, keepdims=True)
        if shared_kv:
            ot = jnp.where(g == 0, ot[:LANES // 2], ot[LANES // 2:])
            for pair in range(rep // 2):
                two = jnp.concatenate([ot[:, (2 * pair) * tqs:(2 * pair + 1) * tqs],
                                       ot[:, (2 * pair + 1) * tqs:(2 * pair + 2) * tqs]], axis=0)
                o_ref[b, rows, pair * LANES:(pair + 1) * LANES] = two.T.astype(o_ref.dtype)
        else:
            o_ref[b, rows, :] = ot.T.astype(o_ref.dtype)

    m_prev = scores(0)
    for j in range(1, nb * nsub):
        m_next = scores(j)
        outputs(j - 1, m_prev)
        m_prev = m_next
    outputs(nb * nsub - 1, m_prev)


def _attention(q, sources, *, n_groups, shared_kv, tq, nsub, tk, nb):
    B, N, Wq = q.shape
    G = n_groups
    tqs = tq // nsub
    tq = min(tq, N)
    nsub = tq // tqs
    qw = Wq // G
    rep = qw // (LANES // 2) if shared_kv else 1
    kv_blk = (lambda b, g, i: (b, 0, 0)) if shared_kv else (lambda b, g, i: (b, 0, g))
    vt_blk = (lambda b, g, i: (b, 0, 0)) if shared_kv else (lambda b, g, i: (b, g, 0))
    in_specs = [pl.BlockSpec((nb, tq, qw), lambda b, g, i: (b, i, g))]
    args = [q]
    nk = 0
    for k, vt in sources:
        n = k.shape[1]
        nk += n
        in_specs += [pl.BlockSpec((nb, n, LANES), kv_blk), pl.BlockSpec((nb, LANES, n), vt_blk)]
        args += [k, vt]
    ow = qw if shared_kv else LANES
    return pl.pallas_call(
        functools.partial(_attention_kernel, n_src=len(sources), tk=tk, nsub=nsub, rep=rep, shared_kv=shared_kv),
        grid=(B // nb, G, N // tq),
        in_specs=in_specs,
        out_specs=pl.BlockSpec((nb, tq, ow), lambda b, g, i: (b, i, g)),
        out_shape=jax.ShapeDtypeStruct((B, N, G * ow), BF16),
        scratch_shapes=[pltpu.VMEM((min(SCORE_SLOTS, nb * nsub), nk, rep * tq // nsub), F32)],
        compiler_params=pltpu.CompilerParams(dimension_semantics=("parallel", "parallel", "parallel"),
                                             vmem_limit_bytes=VMEM_LIMIT),
        name="attention_gqa" if shared_kv else "attention_mla",
    )(*args)


def _gla_kernel(*refs, nchunk, nb, has_state):
    it = iter(refs)
    qkf_ref, vf_ref, gf_ref, qkb_ref, vb_ref, gb_ref = (next(it) for _ in range(6))
    if has_state:
        s0f_ref, s0b_ref = next(it), next(it)
    of_ref, ob_ref, sf_ref, sb_ref, st_scr, bd_scr = (next(it) for _ in range(6))
    C, H, DK, DV, KW, VW = GLA_CHUNK, GLA_HEADS, GLA_DK, GLA_DV, GLA_K_WIDTH, GLA_WIDTH
    i = pl.program_id(1)

    @pl.when(i == 0)
    def _():
        bd_scr[...] = jnp.zeros(bd_scr.shape, BF16)
        for d, s0_ref in enumerate((s0f_ref, s0b_ref) if has_state else (None, None)):
            for b in range(nb):
                for h in range(H):
                    s0 = s0_ref[b, 0, h] if has_state else jnp.zeros((DK, DV), F32)
                    st_scr[d, b, h] = s0
                    bd_scr[d, b, h * DK:(h + 1) * DK, h * DV:(h + 1) * DV] = s0.astype(BF16)

    row = lax.broadcasted_iota(jnp.int32, (C, H * C), 0)
    col = lax.broadcasted_iota(jnp.int32, (C, H * C), 1) % C
    keep = (row >= col, row <= col)
    tok = lax.broadcasted_iota(jnp.int32, (C, KW), 0)
    k_head = lax.broadcasted_iota(jnp.int32, (C, KW), 1) // DK
    v_head = lax.broadcasted_iota(jnp.int32, (C, VW), 1) // DV
    dirs = ((qkf_ref, vf_ref, gf_ref, of_ref, 0), (qkb_ref, vb_ref, gb_ref, ob_ref, KW))

    def chain(d, q, k, v, g, st, bd):
        cum = g
        shift = 1
        while shift < C:
            if d == 0:
                cum = cum + jnp.where(tok >= shift, pltpu.roll(cum, shift, 0), 0.0)
            else:
                cum = cum + jnp.where(tok < C - shift, pltpu.roll(cum, C - shift, 0), 0.0)
            shift *= 2
        last = cum[C - 1:C, :] if d == 0 else cum[0:1, :]
        qd = q * jnp.exp(cum)
        kd = k * jnp.exp(last - cum)
        half = 0.5 * last
        qa = q * jnp.exp(cum - half)
        ka = k * jnp.exp(half - cum)
        ka_stack = jnp.concatenate([jnp.where(k_head == h, ka, 0.0).astype(BF16) for h in range(H)], axis=0)
        a = _dot_nt(qa.astype(BF16), ka_stack)
        a = jnp.where(keep[d], a, 0.0).astype(BF16)
        v_stack = jnp.concatenate([jnp.where(v_head == h, v, 0.0).astype(BF16) for h in range(H)], axis=0)
        o = _dot(qd.astype(BF16), bd) + _dot(a, v_stack)
        kdt = kd.T.astype(BF16)
        decay = jnp.exp(jnp.broadcast_to(last, (LANES, KW)).T)
        vb = v.astype(BF16)
        s_new = [st[h] * decay[h * DK:(h + 1) * DK, :] + _dot(kdt[h * DK:(h + 1) * DK, :], vb[:, h * DV:(h + 1) * DV])
                 for h in range(H)]
        return o, s_new

    def block_diag(states):
        zero = jnp.zeros((DK, DV), BF16)
        return jnp.concatenate(
            [jnp.concatenate([states[h].astype(BF16) if j == h else zero for j in range(H)], axis=1)
             for h in range(H)], axis=0)

    def body(it, carry):
        work = []
        for d, (qk_ref, v_ref, g_ref, o_ref, g0) in enumerate(dirs):
            for b in range(nb):
                loads = []
                for u in range(per_iter):
                    c = it * per_iter + u
                    cc = c if d == 0 else nchunk - 1 - c
                    rows = pl.ds(pl.multiple_of(cc * C, C), C)
                    loads.append((rows, qk_ref[b, rows, :KW], qk_ref[b, rows, KW:], v_ref[b, rows, :],
                                  g_ref[b, rows, g0:g0 + KW]))
                work.append((d, b, o_ref, loads, [st_scr[d, b, h] for h in range(H)], bd_scr[d, b]))
        done = []
        for d, b, o_ref, loads, st, bd in work:
            outs = []
            for u, (rows, q, k, v, g) in enumerate(loads):
                o, st = chain(d, q, k, v, g, st, bd)
                outs.append((rows, o))
                if u + 1 < per_iter:
                    bd = block_diag(st)
            done.append((d, b, o_ref, outs, st))
        for d, b, o_ref, outs, st in done:
            for rows, o in outs:
                o_ref[b, rows, :] = o.astype(o_ref.dtype)
            for h in range(H):
                st_scr[d, b, h] = st[h]
                bd_scr[d, b, h * DK:(h + 1) * DK, h * DV:(h + 1) * DV] = st[h].astype(BF16)
        return carry

    per_iter = 4 if nchunk % 4 == 0 else 1
    lax.fori_loop(0, nchunk // per_iter, body, 0)

    @pl.when(i == pl.num_programs(1) - 1)
    def _():
        for d, s_ref in enumerate((sf_ref, sb_ref)):
            for b in range(nb):
                s_ref[b] = st_scr[d, b]


def _gla(qkg, vg, la, state_fwd, state_bwd, layer, *, tb, nb):
    B, N, _ = qkg.shape
    H, DK, DV = GLA_HEADS, GLA_DK, GLA_DV
    nblk = N // tb
    fwd = lambda w: pl.BlockSpec((nb, tb, w), lambda b, i: (b, i, 0))
    bwd = lambda w: pl.BlockSpec((nb, tb, w), lambda b, i: (b, nblk - 1 - i, 0))
    in_specs = [fwd(2 * GLA_K_WIDTH), fwd(GLA_WIDTH), fwd(2 * GLA_K_WIDTH),
                bwd(2 * GLA_K_WIDTH), bwd(GLA_WIDTH), bwd(2 * GLA_K_WIDTH)]
    args = [qkg, vg, la, qkg, vg, la]
    has_state = state_fwd is not None
    if has_state:
        s0 = pl.BlockSpec((nb, 1, H, DK, DV), lambda b, i: (b, layer, 0, 0, 0))
        in_specs += [s0, s0]
        args += [state_fwd, state_bwd]
    st = pl.BlockSpec((nb, H, DK, DV), lambda b, i: (b, 0, 0, 0))
    sds = jax.ShapeDtypeStruct
    return pl.pallas_call(
        functools.partial(_gla_kernel, nchunk=tb // GLA_CHUNK, nb=nb, has_state=has_state),
        grid=(B // nb, nblk),
        in_specs=in_specs,
        out_specs=[fwd(GLA_WIDTH), bwd(GLA_WIDTH), st, st],
        out_shape=[sds((B, N, GLA_WIDTH), BF16), sds((B, N, GLA_WIDTH), BF16),
                   sds((B, H, DK, DV), F32), sds((B, H, DK, DV), F32)],
        scratch_shapes=[pltpu.VMEM((2, nb, H, DK, DV), F32), pltpu.VMEM((2, nb, GLA_K_WIDTH, GLA_WIDTH), BF16)],
        compiler_params=pltpu.CompilerParams(dimension_semantics=("parallel", "arbitrary"),
                                             vmem_limit_bytes=VMEM_LIMIT),
        name="gla",
    )(*args)


def _out_proj_kernel(x_ref, mod_ref, ya_ref, ga_ref, of_ref, ob_ref, gg_ref, yc_ref, gc_ref,
                     gpre_ref, wm_ref, woa_ref, wog_ref, woc_ref, wout_ref, ggla_ref, gpost_ref,
                     o_ref, *, d_model):
    D = d_model
    zm = _dot_nt(_modulated_input(x_ref[0], mod_ref[0], gpre_ref[...], D), wm_ref[0])
    ya = _dot(ya_ref[0] * ga_ref[0], woa_ref[...])
    og = of_ref[0].astype(F32) + ob_ref[0].astype(F32)
    gg = gg_ref[0]
    heads = []
    for h in range(GLA_HEADS):
        sl = slice(h * GLA_DV, (h + 1) * GLA_DV)
        heads.append((_rms(og[:, sl], ggla_ref[...]) * gg[:, sl]).astype(BF16))
    yb = _dot(jnp.concatenate(heads, axis=-1), wog_ref[...])
    yc = _dot(yc_ref[0] * gc_ref[0], woc_ref[...])
    merged = (_sigmoid(zm[:, :D]) * ya + _sigmoid(zm[:, D:2 * D]) * yb + _sigmoid(zm[:, 2 * D:]) * yc)
    out = _rms(_dot(merged.astype(BF16), wout_ref[...]), gpost_ref[...])
    o_ref[0] = x_ref[0] + mod_ref[0][:, 2 * D:] * out


def _out_proj(x, mod, ya, ga, o_f, o_b, gg, yc, gc, w_merge, layer, lw, *, tm):
    B, N, D = x.shape
    per_batch_mod = mod.shape[0] > 1
    row3 = lambda w: pl.BlockSpec((1, tm, w), lambda b, i: (b, i, 0))
    consts = [lw["w_o_gqa"], lw["w_o_gla"], lw["w_o_mla"], lw["w_out"], lw["g_gla_out"], lw["g_post"]]
    in_specs = [row3(D),
                pl.BlockSpec((1, 1, 3 * D), (lambda b, i: (b, 0, 0)) if per_batch_mod else (lambda b, i: (0, 0, 0))),
                row3(GQA_WIDTH), row3(GQA_WIDTH), row3(GLA_WIDTH), row3(GLA_WIDTH), row3(GLA_WIDTH),
                row3(MLA_WIDTH), row3(MLA_WIDTH),
                _const_spec(lw["g_pre"].shape),
                pl.BlockSpec((1, 3 * D, D), lambda b, i: (layer, 0, 0), pipeline_mode=pl.Buffered(1))]
    in_specs += [_const_spec(c.shape) for c in consts]
    return pl.pallas_call(
        functools.partial(_out_proj_kernel, d_model=D),
        grid=(B, N // tm),
        in_specs=in_specs,
        out_specs=row3(D),
        out_shape=jax.ShapeDtypeStruct((B, N, D), F32),
        compiler_params=pltpu.CompilerParams(dimension_semantics=("parallel", "parallel"),
                                             vmem_limit_bytes=VMEM_LIMIT),
        name="out_proj",
    )(x, mod, ya, ga, o_f, o_b, gg, yc, gc, lw["g_pre"], w_merge, *consts)


def _rope_tables(n_tokens):
    t = np.arange(n_tokens)
    pos = np.stack([t // GRID_W, t % GRID_W], axis=0).astype(np.float64)

    def tables(lane_part, lane_in_part, half, active):
        freqs = ROPE_THETA ** (-(lane_in_part % half).astype(np.float64) / half)
        ang = pos[lane_part].T * freqs[None, :]
        cos = np.where(active[None, :], np.cos(ang), 1.0)
        sin = np.where(active[None, :], np.where(lane_in_part < half, -1.0, 1.0)[None, :] * np.sin(ang), 0.0)
        return jnp.asarray(cos, F32), jnp.asarray(sin, F32)

    lane = np.arange(LANES)
    ja = lane % GQA_HEAD_DIM
    ca, sa = tables(ja // (GQA_HEAD_DIM // 2), ja % (GQA_HEAD_DIM // 2), GQA_HEAD_DIM // 4, np.ones(LANES, bool))
    active = (lane >= MLA_ROPE_LANE) & (lane < MLA_ROPE_LANE + MLA_ROPE_DIM)
    jc = np.where(active, lane - MLA_ROPE_LANE, 0)
    cc, sc = tables(jc // (MLA_ROPE_DIM // 2), jc % (MLA_ROPE_DIM // 2), MLA_ROPE_DIM // 4, active)
    return ca, sa, cc, sc


def _pack_layer(l, p):
    w_dec = jnp.zeros((LANES, 2 * GLA_K_WIDTH), F32)
    w_dec = w_dec.at[:GLA_RANK, :GLA_K_WIDTH].set(p["w_gla_decay_fwd"][l])
    w_dec = w_dec.at[GLA_RANK:2 * GLA_RANK, GLA_K_WIDTH:].set(p["w_gla_decay_bwd"][l])
    b_dec = jnp.concatenate([p["b_gla_decay_fwd"][l], p["b_gla_decay_bwd"][l]])[None, :]
    pad_q = MLA_QK_PAD - MLA_NOPE_DIM - MLA_ROPE_DIM
    w_uq = p["w_mla_uq"][l].reshape(MLA_Q_LORA, MLA_HEADS, MLA_NOPE_DIM + MLA_ROPE_DIM)
    w_uq = jnp.pad(w_uq, ((0, 0), (0, 0), (0, pad_q))).reshape(MLA_Q_LORA, MLA_QK_WIDTH)
    w_ukv = p["w_mla_ukv"][l].reshape(MLA_KV_LORA, MLA_HEADS, MLA_NOPE_DIM + MLA_V_DIM)
    w_uk = jnp.pad(w_ukv[:, :, :MLA_NOPE_DIM], ((0, 0), (0, 0), (0, MLA_QK_PAD - MLA_NOPE_DIM)))
    w_uk = w_uk.reshape(MLA_KV_LORA, MLA_QK_WIDTH)
    w_uv = w_ukv[:, :, MLA_NOPE_DIM:].reshape(MLA_KV_LORA, MLA_WIDTH)
    seg = np.arange(GQA_WIDTH) // GQA_HEAD_DIM
    place = np.zeros((MLA_ROPE_DIM, MLA_QK_WIDTH), np.float32)
    for h in range(MLA_HEADS):
        place[np.arange(MLA_ROPE_DIM), h * MLA_QK_PAD + MLA_ROPE_LANE + np.arange(MLA_ROPE_DIM)] = 1.0
    return {
        "g_pre": p["g_pre"][l][None, :], "g_post": p["g_post"][l][None, :],
        "g_q": jnp.tile(p["g_q_norm"][l], GQA_HEADS)[None, :],
        "g_k": jnp.tile(p["g_k_norm"][l], GQA_KV_HEADS)[None, :],
        "ones": jnp.asarray(seg[:, None] == seg[None, :], BF16),
        "w_dec": w_dec.astype(BF16), "b_dec": b_dec,
        "g_mla_q": p["g_mla_q"][l][None, :], "g_mla_kv": p["g_mla_kv"][l][None, :],
        "w_uq": w_uq.astype(BF16), "w_uk": w_uk.astype(BF16), "w_uv": w_uv.astype(BF16),
        "kr_place": jnp.asarray(place, BF16),
        "w_o_gqa": p["w_o_gqa"][l].astype(BF16), "w_o_gla": p["w_o_gla"][l].astype(BF16),
        "w_o_mla": p["w_o_mla"][l].astype(BF16), "w_out": p["w_out"][l].astype(BF16),
        "g_gla_out": p["g_gla_out"][l][None, :],
    }


def _tiles(batch, seq):
    if seq <= 256:
        nb = 4 if batch % 4 == 0 else 1
        return dict(tm_in=seq, tm_out=seq, att_gqa=(seq, seq // 128, nb), att_mla=(seq, 1, nb), tk=512,
                    gla_tb=seq, gla_nb=2 if batch % 2 == 0 else 1)
    return dict(tm_in=512, tm_out=512, att_gqa=(1024, 16, 1), att_mla=(4096, 16, 1), tk=512,
                gla_tb=512, gla_nb=2 if batch % 2 == 0 else 1)


def _sub_layer(x, mod, w_packed, layer, lw, rope, cache):
    is_ctx = cache is None
    t = _tiles(x.shape[0], x.shape[1])
    outs = _in_proj(x, mod, w_packed[0], w_packed[1], layer, lw, rope, is_ctx=is_ctx, tm=t["tm_in"])
    (qa, ka, vat, ga, qkg, vg, gg, la, qc, kc, vct, gc) = outs[:12]
    src_a, src_c = [(ka, vat)], [(kc, vct)]
    s0f = s0b = None
    if not is_ctx:
        ka_p, vat_p, kc_p, vct_p = _cache_prep(cache["gqa_k"], cache["gqa_v"], cache["mla_ckv"],
                                               cache["mla_krope"], layer, lw)
        src_a, src_c = [(ka_p, vat_p)] + src_a, [(kc_p, vct_p)] + src_c
        s0f, s0b = cache["gla_fwd"], cache["gla_bwd"]
    (tq_a, nsub_a, nb_a), (tq_c, nsub_c, nb_c) = t["att_gqa"], t["att_mla"]
    ya = _attention(qa, src_a, n_groups=GQA_KV_HEADS, shared_kv=True, tq=tq_a, nsub=nsub_a, nb=nb_a, tk=t["tk"])
    yc = _attention(qc, src_c, n_groups=MLA_HEADS, shared_kv=False, tq=tq_c, nsub=nsub_c, nb=nb_c, tk=t["tk"])
    o_f, o_b, s_f, s_b = _gla(qkg, vg, la, s0f, s0b, layer, tb=t["gla_tb"], nb=t["gla_nb"])
    y = _out_proj(x, mod, ya, ga, o_f, o_b, gg, yc, gc, w_packed[2], layer, lw, tm=t["tm_out"])
    ctx = (outs[12], outs[13], outs[14], outs[15], s_f, s_b) if is_ctx else None
    return y, ctx


def kernel(x_prompt, x_sample, cache_gqa_k, cache_gqa_v, cache_mla_ckv, cache_mla_krope, state_gla_fwd, state_gla_bwd, c, c_ctx, w_mod, b_mod, g_pre, g_post, w_in, g_q_norm, g_k_norm, w_gla_decay_fwd, b_gla_decay_fwd, w_gla_decay_bwd, b_gla_decay_bwd, g_gla_out, g_mla_q, g_mla_kv, w_mla_uq, w_mla_ukv, w_o_gqa, w_o_gla, w_o_mla, w_out):
    params = dict(g_pre=g_pre, g_post=g_post, g_q_norm=g_q_norm, g_k_norm=g_k_norm,
                  w_gla_decay_fwd=w_gla_decay_fwd, b_gla_decay_fwd=b_gla_decay_fwd,
                  w_gla_decay_bwd=w_gla_decay_bwd, b_gla_decay_bwd=b_gla_decay_bwd, g_gla_out=g_gla_out,
                  g_mla_q=g_mla_q, g_mla_kv=g_mla_kv, w_mla_uq=w_mla_uq, w_mla_ukv=w_mla_ukv,
                  w_o_gqa=w_o_gqa, w_o_gla=w_o_gla, w_o_mla=w_o_mla, w_out=w_out)
    depth, D = w_in.shape[0], w_in.shape[1]
    B, S = x_prompt.shape[:2]
    Bd, Nd = x_sample.shape[:2]

    conds = jnp.concatenate([c_ctx[None, :], c, jnp.zeros((SUBLANES - 1 - Bd, D), F32)], axis=0)
    mods = _modulation(conds, w_mod, b_mod)
    rope = _rope_tables(Nd)
    w_t = jnp.swapaxes(w_in, 1, 2).astype(BF16)
    zrow = lambda n: jnp.zeros((depth, n, D), BF16)
    kr_rows = w_t[:, _R_KR:_R_GC]
    w_small = jnp.concatenate(
        [w_t[:, _R_R:_R_LAT], zrow(LANES - 2 * GLA_RANK),
         kr_rows, zrow(MLA_ROPE_LANE - MLA_ROPE_DIM), kr_rows, zrow(LANES - MLA_ROPE_LANE - MLA_ROPE_DIM)], axis=1)
    w_packed = (w_t, w_small, w_t[:, _R_M:])
    layers = [_pack_layer(l, params) for l in range(depth)]

    xp = x_prompt
    ctx_out = []
    for l in range(depth):
        xp, ctx = _sub_layer(xp, mods[l, 0:1][:, None, :], w_packed, l, layers[l], None, None)
        ctx_out.append(ctx)

    xs = x_sample
    cache = {"gqa_k": cache_gqa_k, "gqa_v": cache_gqa_v, "mla_ckv": cache_mla_ckv, "mla_krope": cache_mla_krope,
             "gla_fwd": state_gla_fwd, "gla_bwd": state_gla_bwd}
    for l in range(depth):
        xs, _ = _sub_layer(xs, mods[l, 1:1 + Bd][:, None, :], w_packed, l, layers[l], rope, cache)

    stack = lambda j: jnp.stack([ctx_out[l][j] for l in range(depth)], axis=1)
    new_k = stack(0).reshape(B, depth, S, GQA_KV_HEADS, GQA_HEAD_DIM)
    new_v = stack(1).reshape(B, depth, S, GQA_KV_HEADS, GQA_HEAD_DIM)
    return (xp, xs, new_k, new_v, stack(2), stack(3), stack(4), stack(5))
```

```python
import functools

import numpy as np
import jax
import jax.numpy as jnp
from jax import lax
from jax.experimental import pallas as pl
from jax.experimental.pallas import tpu as pltpu

F32 = jnp.float32
BF16 = jnp.bfloat16

EPS = 1e-6
LOG2_E = 1.4426950408889634
ROPE_THETA = 10000.0
GRID_W = 64

GQA_HEADS, GQA_KV_HEADS, GQA_HEAD_DIM = 8, 2, 64
GQA_REP = GQA_HEADS // GQA_KV_HEADS
GQA_WIDTH = GQA_HEADS * GQA_HEAD_DIM
GQA_KV_WIDTH = GQA_KV_HEADS * GQA_HEAD_DIM
GLA_HEADS, GLA_DK, GLA_DV = 4, 64, 128
GLA_WIDTH = GLA_HEADS * GLA_DV
GLA_K_WIDTH = GLA_HEADS * GLA_DK
GLA_RANK = 16
GLA_NORMALIZER = 16.0
GLA_CHUNK = 64
MLA_HEADS, MLA_Q_LORA, MLA_KV_LORA = 4, 256, 256
MLA_NOPE_DIM, MLA_ROPE_DIM, MLA_V_DIM = 64, 32, 128
MLA_WIDTH = MLA_HEADS * MLA_V_DIM
MLA_QK_PAD = 128
MLA_QK_WIDTH = MLA_HEADS * MLA_QK_PAD
MLA_ROPE_LANE = MLA_NOPE_DIM

LANES = 128
SUBLANES = 8
BF16_SUBLANES = 16
SCORE_SLOTS = 3
VMEM_LIMIT = 56 * 1024 * 1024

assert GQA_KV_WIDTH == LANES and 2 * GQA_HEAD_DIM == LANES and MLA_V_DIM == LANES


def _sigmoid(x):
    return 1.0 / (1.0 + jnp.exp(-x))


def _silu(x):
    return x * _sigmoid(x)


def _log_sigmoid(x):
    return jnp.minimum(x, 0.0) - jnp.log(1.0 + jnp.exp(-jnp.abs(x)))


def _dot(a, b):
    return jnp.dot(a, b, preferred_element_type=F32)


def _dot_nt(a, b):
    return lax.dot_general(a, b, (((1,), (1,)), ((), ())), preferred_element_type=F32)


def _segment_mean_sq(x, ones_bf16, width):
    sq = x * x
    hi = sq.astype(BF16)
    lo = (sq - hi.astype(F32)).astype(BF16)
    return (_dot(hi, ones_bf16) + _dot(lo, ones_bf16)) * (1.0 / width)


def _swap_halves(x, half):
    n = x.shape[-1]
    lane = lax.broadcasted_iota(jnp.int32, x.shape, x.ndim - 1)
    first = (lane % (2 * half)) < half
    return jnp.where(first, pltpu.roll(x, n - half, x.ndim - 1), pltpu.roll(x, half, x.ndim - 1))


def _rope(x, cos, sin_signed, half):
    return x * cos + _swap_halves(x, half) * sin_signed


def _tile_lanes(t, reps):
    return t if reps == 1 else jnp.concatenate([t] * reps, axis=-1)


def _rms(x, gain):
    return x * lax.rsqrt(jnp.mean(x * x, axis=-1, keepdims=True) + EPS) * gain


def _mod_kernel(c_ref, w_ref, b_ref, o_ref):
    c = c_ref[...]
    o_ref[0] = _dot(_silu(c).astype(BF16), w_ref[0].astype(BF16)) + b_ref[0]


def _modulation(conds, w_mod, b_mod):
    L, D, D3 = w_mod.shape
    nj = D3 // D
    return pl.pallas_call(
        _mod_kernel,
        grid=(L, nj),
        in_specs=[pl.BlockSpec((SUBLANES, D), lambda l, j: (0, 0)),
                  pl.BlockSpec((1, D, D), lambda l, j: (l, 0, j)),
                  pl.BlockSpec((1, 1, D), lambda l, j: (l, 0, j))],
        out_specs=pl.BlockSpec((1, SUBLANES, D), lambda l, j: (l, 0, j)),
        out_shape=jax.ShapeDtypeStruct((L, SUBLANES, D3), F32),
        compiler_params=pltpu.CompilerParams(dimension_semantics=("parallel", "parallel")),
        name="modulation",
    )(conds, w_mod, b_mod.reshape(L, 1, D3))


_C_QA = 0
_C_KV = _C_QA + GQA_WIDTH
_C_GA = _C_KV + 2 * GQA_KV_WIDTH
_C_QKG = _C_GA + GQA_WIDTH
_C_VG = _C_QKG + 2 * GLA_K_WIDTH
_C_GG = _C_VG + GLA_WIDTH
_R_R = _C_GG + GLA_WIDTH
_R_LAT = _R_R + 2 * GLA_RANK
_R_KR = _R_LAT + MLA_Q_LORA + MLA_KV_LORA
_R_GC = _R_KR + MLA_ROPE_DIM
_R_M = _R_GC + MLA_WIDTH
assert _R_R % LANES == 0 and _R_LAT % BF16_SUBLANES == 0 and _R_GC % BF16_SUBLANES == 0


def _modulated_input(x, mod, g_pre, d_model):
    shift, scale = mod[:, :d_model], mod[:, d_model:2 * d_model]
    return (_rms(x, g_pre) * (1.0 + scale) + shift).astype(BF16)


def _in_proj_kernel(*refs, is_ctx, d_model):
    it = iter(refs)
    x_ref, mod_ref, gpre_ref, wt_ref, ws_ref, gq_ref, gk_ref, ones_ref, wdec_ref, bdec_ref = (next(it) for _ in range(10))
    gmq_ref, gmkv_ref, wuq_ref, wk_ref, wv_ref = (next(it) for _ in range(5))
    if not is_ctx:
        ca_ref, sa_ref, cc_ref, sc_ref = (next(it) for _ in range(4))
    (qa_o, ka_o, vat_o, ga_o, qkg_o, vg_o, gg_o, la_o,
     qc_o, kc_o, vct_o, gc_o) = (next(it) for _ in range(12))
    if is_ctx:
        ka32_o, va32_o, ckv32_o, kr32_o = (next(it) for _ in range(4))

    D = d_model
    hb = _modulated_input(x_ref[0], mod_ref[0], gpre_ref[...], D)

    z_head = _dot_nt(hb, wt_ref[0, :_R_R, :])
    z_lat = _dot_nt(hb, wt_ref[0, _R_LAT:_R_KR, :])
    z_gc = _dot_nt(hb, wt_ref[0, _R_GC:_R_M, :])
    z_small = _dot_nt(hb, ws_ref[0])

    def head(c0, width):
        return z_head[:, c0:c0 + width]

    ones = ones_ref[...]

    qa = head(_C_QA, GQA_WIDTH)
    qa = qa * lax.rsqrt(_segment_mean_sq(qa, ones, GQA_HEAD_DIM) + EPS) * gq_ref[...]
    kv = head(_C_KV, 2 * GQA_KV_WIDTH)
    ka, va = kv[:, :GQA_KV_WIDTH], kv[:, GQA_KV_WIDTH:]
    ka = ka * lax.rsqrt(_segment_mean_sq(ka, ones[:GQA_KV_WIDTH, :GQA_KV_WIDTH], GQA_HEAD_DIM) + EPS) * gk_ref[...]
    if is_ctx:
        ka32_o[0] = ka
        va32_o[0] = va
    else:
        ca, sa = ca_ref[...], sa_ref[...]
        qa = _rope(qa, _tile_lanes(ca, GQA_WIDTH // LANES), _tile_lanes(sa, GQA_WIDTH // LANES), GQA_HEAD_DIM // 4)
        ka = _rope(ka, ca, sa, GQA_HEAD_DIM // 4)
    qa_o[0] = (qa * (GQA_HEAD_DIM ** -0.5 * LOG2_E)).astype(BF16)
    ka_o[0] = ka.astype(BF16)
    vat_o[0] = va.T.astype(BF16)
    ga_o[0] = _silu(head(_C_GA, GQA_WIDTH)).astype(BF16)

    qkg = head(_C_QKG, 2 * GLA_K_WIDTH)
    lane = lax.broadcasted_iota(jnp.int32, qkg.shape, 1)
    qkg_o[0] = jnp.where(lane < GLA_K_WIDTH, qkg * GLA_DK ** -0.5, qkg)
    vg_o[0] = head(_C_VG, GLA_WIDTH)
    r = z_small[:, :LANES].astype(BF16)
    la_o[0] = _log_sigmoid(_dot(r, wdec_ref[...]) + bdec_ref[...]) * (1.0 / GLA_NORMALIZER)
    gg_o[0] = _silu(head(_C_GG, GLA_WIDTH)).astype(BF16)

    lat = z_lat
    ql = _rms(lat[:, :MLA_Q_LORA], gmq_ref[...])
    ckv = _rms(lat[:, MLA_Q_LORA:], gmkv_ref[...])
    qc = _dot(ql.astype(BF16), wuq_ref[...])
    krb = z_small[:, LANES:]
    if is_ctx:
        ckv32_o[0] = ckv
        kr32_o[0] = krb[:, :MLA_ROPE_DIM]
    else:
        cc, sc = cc_ref[...], sc_ref[...]
        qc = _rope(qc, _tile_lanes(cc, MLA_HEADS), _tile_lanes(sc, MLA_HEADS), MLA_ROPE_DIM // 4)
        krb = _rope(krb, cc, sc, MLA_ROPE_DIM // 4)
    lane = lax.broadcasted_iota(jnp.int32, krb.shape, 1)
    kr_part = jnp.where((lane >= MLA_ROPE_LANE) & (lane < MLA_ROPE_LANE + MLA_ROPE_DIM), krb, 0.0)
    qc_o[0] = (qc * ((MLA_NOPE_DIM + MLA_ROPE_DIM) ** -0.5 * LOG2_E)).astype(BF16)
    ckvb = ckv.astype(BF16)
    kc_o[0] = (_dot(ckvb, wk_ref[...]) + _tile_lanes(kr_part, MLA_HEADS)).astype(BF16)
    vct_o[0] = _dot(ckvb, wv_ref[...]).T.astype(BF16)
    gc_o[0] = _silu(z_gc).astype(BF16)


def _const_spec(shape):
    nd = len(shape)
    return pl.BlockSpec(shape, lambda b, i, _nd=nd: (0,) * _nd, pipeline_mode=pl.Buffered(1))


def _in_proj(x, mod, w_packed, w_small, layer, lw, rope, *, is_ctx, tm):
    B, N, D = x.shape
    assert w_packed.shape[1] == _R_M + 3 * D
    per_batch_mod = mod.shape[0] > 1
    row3 = lambda w: pl.BlockSpec((1, tm, w), lambda b, i: (b, i, 0))
    col3 = lambda h: pl.BlockSpec((1, h, tm), lambda b, i: (b, 0, i))
    consts = [lw["g_q"], lw["g_k"], lw["ones"], lw["w_dec"], lw["b_dec"],
              lw["g_mla_q"], lw["g_mla_kv"], lw["w_uq"], lw["w_uk"], lw["w_uv"]]
    in_specs = [row3(D),
                pl.BlockSpec((1, 1, 3 * D), (lambda b, i: (b, 0, 0)) if per_batch_mod else (lambda b, i: (0, 0, 0))),
                _const_spec(lw["g_pre"].shape),
                pl.BlockSpec((1, _R_M, D), lambda b, i: (layer, 0, 0), pipeline_mode=pl.Buffered(1)),
                pl.BlockSpec((1, 2 * LANES, D), lambda b, i: (layer, 0, 0), pipeline_mode=pl.Buffered(1))]
    in_specs += [_const_spec(c.shape) for c in consts]
    args = [x, mod, lw["g_pre"], w_packed, w_small] + consts
    if not is_ctx:
        in_specs += [pl.BlockSpec((tm, LANES), lambda b, i: (i, 0))] * 4
        args += list(rope)
    sds = jax.ShapeDtypeStruct
    out_shape = [
        sds((B, N, GQA_WIDTH), BF16), sds((B, N, GQA_KV_WIDTH), BF16), sds((B, GQA_KV_WIDTH, N), BF16),
        sds((B, N, GQA_WIDTH), BF16),
        sds((B, N, 2 * GLA_K_WIDTH), F32), sds((B, N, GLA_WIDTH), F32), sds((B, N, GLA_WIDTH), BF16),
        sds((B, N, 2 * GLA_K_WIDTH), F32),
        sds((B, N, MLA_QK_WIDTH), BF16), sds((B, N, MLA_QK_WIDTH), BF16), sds((B, MLA_WIDTH, N), BF16),
        sds((B, N, MLA_WIDTH), BF16)]
    out_specs = [
        row3(GQA_WIDTH), row3(GQA_KV_WIDTH), col3(GQA_KV_WIDTH), row3(GQA_WIDTH),
        row3(2 * GLA_K_WIDTH), row3(GLA_WIDTH), row3(GLA_WIDTH), row3(2 * GLA_K_WIDTH),
        row3(MLA_QK_WIDTH), row3(MLA_QK_WIDTH), col3(MLA_WIDTH), row3(MLA_WIDTH)]
    if is_ctx:
        out_shape += [sds((B, N, GQA_KV_WIDTH), F32), sds((B, N, GQA_KV_WIDTH), F32),
                      sds((B, N, MLA_KV_LORA), F32), sds((B, N, MLA_ROPE_DIM), F32)]
        out_specs += [row3(GQA_KV_WIDTH), row3(GQA_KV_WIDTH), row3(MLA_KV_LORA), row3(MLA_ROPE_DIM)]
    return pl.pallas_call(
        functools.partial(_in_proj_kernel, is_ctx=is_ctx, d_model=D),
        grid=(B, N // tm),
        in_specs=in_specs, out_specs=out_specs, out_shape=out_shape,
        compiler_params=pltpu.CompilerParams(dimension_semantics=("parallel", "parallel"),
                                             vmem_limit_bytes=VMEM_LIMIT),
        name="in_proj_ctx" if is_ctx else "in_proj_lat",
    )(*args)


def _cache_prep_kernel(gk_ref, gv_ref, ckv_ref, kr_ref, wk_ref, wv_ref, place_ref, ka_o, vat_o, kc_o, vct_o):
    ka_o[0] = gk_ref[0, 0].astype(BF16)
    vat_o[0] = gv_ref[0, 0].T.astype(BF16)
    ckvb = ckv_ref[0, 0].astype(BF16)
    kc_o[0] = (_dot(ckvb, wk_ref[...]) + _dot(kr_ref[0, 0].astype(BF16), place_ref[...])).astype(BF16)
    vct_o[0] = _dot(ckvb, wv_ref[...]).T.astype(BF16)


def _cache_prep(cache_gqa_k, cache_gqa_v, cache_mla_ckv, cache_mla_krope, layer, lw):
    B, L, P = cache_gqa_k.shape[:3]
    gk = cache_gqa_k.reshape(B, L, P, GQA_KV_WIDTH)
    gv = cache_gqa_v.reshape(B, L, P, GQA_KV_WIDTH)
    lsel = lambda w: pl.BlockSpec((1, 1, P, w), lambda b: (b, layer, 0, 0))
    full = lambda a: pl.BlockSpec(a.shape, lambda b: (0,) * a.ndim)
    sds = jax.ShapeDtypeStruct
    return pl.pallas_call(
        _cache_prep_kernel,
        grid=(B,),
        in_specs=[lsel(GQA_KV_WIDTH), lsel(GQA_KV_WIDTH), lsel(MLA_KV_LORA), lsel(MLA_ROPE_DIM),
                  full(lw["w_uk"]), full(lw["w_uv"]), full(lw["kr_place"])],
        out_specs=[pl.BlockSpec((1, P, GQA_KV_WIDTH), lambda b: (b, 0, 0)),
                   pl.BlockSpec((1, GQA_KV_WIDTH, P), lambda b: (b, 0, 0)),
                   pl.BlockSpec((1, P, MLA_QK_WIDTH), lambda b: (b, 0, 0)),
                   pl.BlockSpec((1, MLA_WIDTH, P), lambda b: (b, 0, 0))],
        out_shape=[sds((B, P, GQA_KV_WIDTH), BF16), sds((B, GQA_KV_WIDTH, P), BF16),
                   sds((B, P, MLA_QK_WIDTH), BF16), sds((B, MLA_WIDTH, P), BF16)],
        compiler_params=pltpu.CompilerParams(dimension_semantics=("parallel",)),
        name="cache_prep",
    )(gk, gv, cache_mla_ckv, cache_mla_krope, lw["w_uk"], lw["w_uv"], lw["kr_place"])


def _attention_kernel(*refs, n_src, tk, nsub, rep, shared_kv):
    q_ref, o_ref, s_scr = refs[0], refs[1 + 2 * n_src], refs[2 + 2 * n_src]
    srcs = [(refs[1 + 2 * j], refs[2 + 2 * j]) for j in range(n_src)]
    nb = q_ref.shape[0]
    tqs = q_ref.shape[1] // nsub
    g = pl.program_id(1)

    chunks = []
    off = 0
    for k_ref, vt_ref in srcs:
        n = k_ref.shape[1]
        step = min(tk, n)
        for c in range(n // step):
            chunks.append((k_ref, vt_ref, c * step, step, off))
            off += step

    def scores(j):
        b, rows = j // nsub, slice((j % nsub) * tqs, (j % nsub + 1) * tqs)
        if shared_kv:
            hd = LANES // 2
            x = q_ref[b, rows, :].astype(F32)
            lane = lax.broadcasted_iota(jnp.int32, (tqs, LANES), 1)
            pieces = []
            for r in range(rep):
                win = x[:, (r // 2) * LANES:(r // 2 + 1) * LANES]
                win = jnp.where(g == r % 2, win, pltpu.roll(win, hd, 1))
                pieces.append(jnp.where(lane // hd == g, win, 0.0).astype(BF16))
            qcat = jnp.concatenate(pieces, axis=0)
        else:
            qcat = q_ref[b, rows, :]
        W = qcat.shape[0]
        m_acc = jnp.full((SUBLANES, W), -jnp.inf, F32)
        for k_ref, _, c0, step, off in chunks:
            s = _dot_nt(k_ref[b, c0:c0 + step, :], qcat)
            s_scr[j % SCORE_SLOTS, off:off + step, :] = s
            m_acc = jnp.maximum(m_acc, jnp.max(s.reshape(step // SUBLANES, SUBLANES, W), axis=0))
        return jnp.max(m_acc, axis=0, keepdims=True)

    def outputs(j, m):
        W = m.shape[1]
        b, rows = j // nsub, slice((j % nsub) * tqs, (j % nsub + 1) * tqs)
        l_acc = jnp.zeros((SUBLANES, W), F32)
        acc = jnp.zeros((LANES, W), F32)
        for _, vt_ref, c0, step, off in chunks:
            p = jnp.exp2(s_scr[j % SCORE_SLOTS, off:off + step, :] - m)
            l_acc = l_acc + jnp.sum(p.reshape(step // SUBLANES, SUBLANES, W), axis=0)
            acc = acc + _dot(vt_ref[b, :, c0:c0 + step], p.astype(BF16))
        ot = acc / jnp.sum(l_acc, axis=0, keepdims=True)
        if shared_kv:
            ot = jnp.where(g == 0, ot[:LANES // 2], ot[LANES // 2:])
            for pair in range(rep // 2):
                two = jnp.concatenate([ot[:, (2 * pair) * tqs:(2 * pair + 1) * tqs],
                                       ot[:, (2 * pair + 1) * tqs:(2 * pair + 2) * tqs]], axis=0)
                o_ref[b, rows, pair * LANES:(pair + 1) * LANES] = two.T.astype(o_ref.dtype)
        else:
            o_ref[b, rows, :] = ot.T.astype(o_ref.dtype)

    m_prev = scores(0)
    for j in range(1, nb * nsub):
        m_next = scores(j)
        outputs(j - 1, m_prev)
        m_prev = m_next
    outputs(nb * nsub - 1, m_prev)


def _attention(q, sources, *, n_groups, shared_kv, tq, nsub, tk, nb):
    B, N, Wq = q.shape
    G = n_groups
    tqs = tq // nsub
    tq = min(tq, N)
    nsub = tq // tqs
    qw = Wq // G
    rep = qw // (LANES // 2) if shared_kv else 1
    kv_blk = (lambda b, g, i: (b, 0, 0)) if shared_kv else (lambda b, g, i: (b, 0, g))
    vt_blk = (lambda b, g, i: (b, 0, 0)) if shared_kv else (lambda b, g, i: (b, g, 0))
    in_specs = [pl.BlockSpec((nb, tq, qw), lambda b, g, i: (b, i, g))]
    args = [q]
    nk = 0
    for k, vt in sources:
        n = k.shape[1]
        nk += n
        in_specs += [pl.BlockSpec((nb, n, LANES), kv_blk), pl.BlockSpec((nb, LANES, n), vt_blk)]
        args += [k, vt]
    ow = qw if shared_kv else LANES
    return pl.pallas_call(
        functools.partial(_attention_kernel, n_src=len(sources), tk=tk, nsub=nsub, rep=rep, shared_kv=shared_kv),
        grid=(B // nb, G, N // tq),
        in_specs=in_specs,
        out_specs=pl.BlockSpec((nb, tq, ow), lambda b, g, i: (b, i, g)),
        out_shape=jax.ShapeDtypeStruct((B, N, G * ow), BF16),
        scratch_shapes=[pltpu.VMEM((min(SCORE_SLOTS, nb * nsub), nk, rep * tq // nsub), F32)],
        compiler_params=pltpu.CompilerParams(dimension_semantics=("parallel", "parallel", "parallel"),
                                             vmem_limit_bytes=VMEM_LIMIT),
        name="attention_gqa" if shared_kv else "attention_mla",
    )(*args)


def _gla_kernel(*refs, nchunk, nb, has_state):
    it = iter(refs)
    qkf_ref, vf_ref, gf_ref, qkb_ref, vb_ref, gb_ref = (next(it) for _ in range(6))
    if has_state:
        s0f_ref, s0b_ref = next(it), next(it)
    of_ref, ob_ref, sf_ref, sb_ref, st_scr, bd_scr = (next(it) for _ in range(6))
    C, H, DK, DV, KW, VW = GLA_CHUNK, GLA_HEADS, GLA_DK, GLA_DV, GLA_K_WIDTH, GLA_WIDTH
    i = pl.program_id(1)

    @pl.when(i == 0)
    def _():
        bd_scr[...] = jnp.zeros(bd_scr.shape, BF16)
        for d, s0_ref in enumerate((s0f_ref, s0b_ref) if has_state else (None, None)):
            for b in range(nb):
                for h in range(H):
                    s0 = s0_ref[b, 0, h] if has_state else jnp.zeros((DK, DV), F32)
                    st_scr[d, b, h] = s0
                    bd_scr[d, b, h * DK:(h + 1) * DK, h * DV:(h + 1) * DV] = s0.astype(BF16)

    row = lax.broadcasted_iota(jnp.int32, (C, H * C), 0)
    col = lax.broadcasted_iota(jnp.int32, (C, H * C), 1) % C
    keep = (row >= col, row <= col)
    tok = lax.broadcasted_iota(jnp.int32, (C, KW), 0)
    k_head = lax.broadcasted_iota(jnp.int32, (C, KW), 1) // DK
    v_head = lax.broadcasted_iota(jnp.int32, (C, VW), 1) // DV
    dirs = ((qkf_ref, vf_ref, gf_ref, of_ref, 0), (qkb_ref, vb_ref, gb_ref, ob_ref, KW))

    def chain(d, q, k, v, g, st, bd):
        cum = g
        shift = 1
        while shift < C:
            if d == 0:
                cum = cum + jnp.where(tok >= shift, pltpu.roll(cum, shift, 0), 0.0)
            else:
                cum = cum + jnp.where(tok < C - shift, pltpu.roll(cum, C - shift, 0), 0.0)
            shift *= 2
        last = cum[C - 1:C, :] if d == 0 else cum[0:1, :]
        qd = q * jnp.exp(cum)
        kd = k * jnp.exp(last - cum)
        half = 0.5 * last
        qa = q * jnp.exp(cum - half)
        ka = k * jnp.exp(half - cum)
        ka_stack = jnp.concatenate([jnp.where(k_head == h, ka, 0.0).astype(BF16) for h in range(H)], axis=0)
        a = _dot_nt(qa.astype(BF16), ka_stack)
        a = jnp.where(keep[d], a, 0.0).astype(BF16)
        v_stack = jnp.concatenate([jnp.where(v_head == h, v, 0.0).astype(BF16) for h in range(H)], axis=0)
        o = _dot(qd.astype(BF16), bd) + _dot(a, v_stack)
        kdt = kd.T.astype(BF16)
        decay = jnp.exp(jnp.broadcast_to(last, (LANES, KW)).T)
        vb = v.astype(BF16)
        s_new = [st[h] * decay[h * DK:(h + 1) * DK, :] + _dot(kdt[h * DK:(h + 1) * DK, :], vb[:, h * DV:(h + 1) * DV])
                 for h in range(H)]
        return o, s_new

    def block_diag(states):
        zero = jnp.zeros((DK, DV), BF16)
        return jnp.concatenate(
            [jnp.concatenate([states[h].astype(BF16) if j == h else zero for j in range(H)], axis=1)
             for h in range(H)], axis=0)

    def body(it, carry):
        work = []
        for d, (qk_ref, v_ref, g_ref, o_ref, g0) in enumerate(dirs):
            for b in range(nb):
                loads = []
                for u in range(per_iter):
                    c = it * per_iter + u
                    cc = c if d == 0 else nchunk - 1 - c
                    rows = pl.ds(pl.multiple_of(cc * C, C), C)
                    loads.append((rows, qk_ref[b, rows, :KW], qk_ref[b, rows, KW:], v_ref[b, rows, :],
                                  g_ref[b, rows, g0:g0 + KW]))
                work.append((d, b, o_ref, loads, [st_scr[d, b, h] for h in range(H)], bd_scr[d, b]))
        done = []
        for d, b, o_ref, loads, st, bd in work:
            outs = []
            for u, (rows, q, k, v, g) in enumerate(loads):
                o, st = chain(d, q, k, v, g, st, bd)
                outs.append((rows, o))
                if u + 1 < per_iter:
                    bd = block_diag(st)
            done.append((d, b, o_ref, outs, st))
        for d, b, o_ref, outs, st in done:
            for rows, o in outs:
                o_ref[b, rows, :] = o.astype(o_ref.dtype)
            for h in range(H):
                st_scr[d, b, h] = st[h]
                bd_scr[d, b, h * DK:(h + 1) * DK, h * DV:(h + 1) * DV] = st[h].astype(BF16)
        return carry

    per_iter = 4 if nchunk % 4 == 0 else 1
    lax.fori_loop(0, nchunk // per_iter, body, 0)

    @pl.when(i == pl.num_programs(1) - 1)
    def _():
        for d, s_ref in enumerate((sf_ref, sb_ref)):
            for b in range(nb):
                s_ref[b] = st_scr[d, b]


def _gla(qkg, vg, la, state_fwd, state_bwd, layer, *, tb, nb):
    B, N, _ = qkg.shape
    H, DK, DV = GLA_HEADS, GLA_DK, GLA_DV
    nblk = N // tb
    fwd = lambda w: pl.BlockSpec((nb, tb, w), lambda b, i: (b, i, 0))
    bwd = lambda w: pl.BlockSpec((nb, tb, w), lambda b, i: (b, nblk - 1 - i, 0))
    in_specs = [fwd(2 * GLA_K_WIDTH), fwd(GLA_WIDTH), fwd(2 * GLA_K_WIDTH),
                bwd(2 * GLA_K_WIDTH), bwd(GLA_WIDTH), bwd(2 * GLA_K_WIDTH)]
    args = [qkg, vg, la, qkg, vg, la]
    has_state = state_fwd is not None
    if has_state:
        s0 = pl.BlockSpec((nb, 1, H, DK, DV), lambda b, i: (b, layer, 0, 0, 0))
        in_specs += [s0, s0]
        args += [state_fwd, state_bwd]
    st = pl.BlockSpec((nb, H, DK, DV), lambda b, i: (b, 0, 0, 0))
    sds = jax.ShapeDtypeStruct
    return pl.pallas_call(
        functools.partial(_gla_kernel, nchunk=tb // GLA_CHUNK, nb=nb, has_state=has_state),
        grid=(B // nb, nblk),
        in_specs=in_specs,
        out_specs=[fwd(GLA_WIDTH), bwd(GLA_WIDTH), st, st],
        out_shape=[sds((B, N, GLA_WIDTH), BF16), sds((B, N, GLA_WIDTH), BF16),
                   sds((B, H, DK, DV), F32), sds((B, H, DK, DV), F32)],
        scratch_shapes=[pltpu.VMEM((2, nb, H, DK, DV), F32), pltpu.VMEM((2, nb, GLA_K_WIDTH, GLA_WIDTH), BF16)],
        compiler_params=pltpu.CompilerParams(dimension_semantics=("parallel", "arbitrary"),
                                             vmem_limit_bytes=VMEM_LIMIT),
        name="gla",
    )(*args)


def _out_proj_kernel(x_ref, mod_ref, ya_ref, ga_ref, of_ref, ob_ref, gg_ref, yc_ref, gc_ref,
                     gpre_ref, wm_ref, woa_ref, wog_ref, woc_ref, wout_ref, ggla_ref, gpost_ref,
                     o_ref, *, d_model):
    D = d_model
    zm = _dot_nt(_modulated_input(x_ref[0], mod_ref[0], gpre_ref[...], D), wm_ref[0])
    ya = _dot(ya_ref[0] * ga_ref[0], woa_ref[...])
    og = of_ref[0].astype(F32) + ob_ref[0].astype(F32)
    gg = gg_ref[0]
    heads = []
    for h in range(GLA_HEADS):
        sl = slice(h * GLA_DV, (h + 1) * GLA_DV)
        heads.append((_rms(og[:, sl], ggla_ref[...]) * gg[:, sl]).astype(BF16))
    yb = _dot(jnp.concatenate(heads, axis=-1), wog_ref[...])
    yc = _dot(yc_ref[0] * gc_ref[0], woc_ref[...])
    merged = (_sigmoid(zm[:, :D]) * ya + _sigmoid(zm[:, D:2 * D]) * yb + _sigmoid(zm[:, 2 * D:]) * yc)
    out = _rms(_dot(merged.astype(BF16), wout_ref[...]), gpost_ref[...])
    o_ref[0] = x_ref[0] + mod_ref[0][:, 2 * D:] * out


def _out_proj(x, mod, ya, ga, o_f, o_b, gg, yc, gc, w_merge, layer, lw, *, tm):
    B, N, D = x.shape
    per_batch_mod = mod.shape[0] > 1
    row3 = lambda w: pl.BlockSpec((1, tm, w), lambda b, i: (b, i, 0))
    consts = [lw["w_o_gqa"], lw["w_o_gla"], lw["w_o_mla"], lw["w_out"], lw["g_gla_out"], lw["g_post"]]
    in_specs = [row3(D),
                pl.BlockSpec((1, 1, 3 * D), (lambda b, i: (b, 0, 0)) if per_batch_mod else (lambda b, i: (0, 0, 0))),
                row3(GQA_WIDTH), row3(GQA_WIDTH), row3(GLA_WIDTH), row3(GLA_WIDTH), row3(GLA_WIDTH),
                row3(MLA_WIDTH), row3(MLA_WIDTH),
                _const_spec(lw["g_pre"].shape),
                pl.BlockSpec((1, 3 * D, D), lambda b, i: (layer, 0, 0), pipeline_mode=pl.Buffered(1))]
    in_specs += [_const_spec(c.shape) for c in consts]
    return pl.pallas_call(
        functools.partial(_out_proj_kernel, d_model=D),
        grid=(B, N // tm),
        in_specs=in_specs,
        out_specs=row3(D),
        out_shape=jax.ShapeDtypeStruct((B, N, D), F32),
        compiler_params=pltpu.CompilerParams(dimension_semantics=("parallel", "parallel"),
                                             vmem_limit_bytes=VMEM_LIMIT),
        name="out_proj",
    )(x, mod, ya, ga, o_f, o_b, gg, yc, gc, lw["g_pre"], w_merge, *consts)


def _rope_tables(n_tokens):
    t = np.arange(n_tokens)
    pos = np.stack([t // GRID_W, t % GRID_W], axis=0).astype(np.float64)

    def tables(lane_part, lane_in_part, half, active):
        freqs = ROPE_THETA ** (-(lane_in_part % half).astype(np.float64) / half)
        ang = pos[lane_part].T * freqs[None, :]
        cos = np.where(active[None, :], np.cos(ang), 1.0)
        sin = np.where(active[None, :], np.where(lane_in_part < half, -1.0, 1.0)[None, :] * np.sin(ang), 0.0)
        return jnp.asarray(cos, F32), jnp.asarray(sin, F32)

    lane = np.arange(LANES)
    ja = lane % GQA_HEAD_DIM
    ca, sa = tables(ja // (GQA_HEAD_DIM // 2), ja % (GQA_HEAD_DIM // 2), GQA_HEAD_DIM // 4, np.ones(LANES, bool))
    active = (lane >= MLA_ROPE_LANE) & (lane < MLA_ROPE_LANE + MLA_ROPE_DIM)
    jc = np.where(active, lane - MLA_ROPE_LANE, 0)
    cc, sc = tables(jc // (MLA_ROPE_DIM // 2), jc % (MLA_ROPE_DIM // 2), MLA_ROPE_DIM // 4, active)
    return ca, sa, cc, sc


def _pack_layer(l, p):
    w_dec = jnp.zeros((LANES, 2 * GLA_K_WIDTH), F32)
    w_dec = w_dec.at[:GLA_RANK, :GLA_K_WIDTH].set(p["w_gla_decay_fwd"][l])
    w_dec = w_dec.at[GLA_RANK:2 * GLA_RANK, GLA_K_WIDTH:].set(p["w_gla_decay_bwd"][l])
    b_dec = jnp.concatenate([p["b_gla_decay_fwd"][l], p["b_gla_decay_bwd"][l]])[None, :]
    pad_q = MLA_QK_PAD - MLA_NOPE_DIM - MLA_ROPE_DIM
    w_uq = p["w_mla_uq"][l].reshape(MLA_Q_LORA, MLA_HEADS, MLA_NOPE_DIM + MLA_ROPE_DIM)
    w_uq = jnp.pad(w_uq, ((0, 0), (0, 0), (0, pad_q))).reshape(MLA_Q_LORA, MLA_QK_WIDTH)
    w_ukv = p["w_mla_ukv"][l].reshape(MLA_KV_LORA, MLA_HEADS, MLA_NOPE_DIM + MLA_V_DIM)
    w_uk = jnp.pad(w_ukv[:, :, :MLA_NOPE_DIM], ((0, 0), (0, 0), (0, MLA_QK_PAD - MLA_NOPE_DIM)))
    w_uk = w_uk.reshape(MLA_KV_LORA, MLA_QK_WIDTH)
    w_uv = w_ukv[:, :, MLA_NOPE_DIM:].reshape(MLA_KV_LORA, MLA_WIDTH)
    seg = np.arange(GQA_WIDTH) // GQA_HEAD_DIM
    place = np.zeros((MLA_ROPE_DIM, MLA_QK_WIDTH), np.float32)
    for h in range(MLA_HEADS):
        place[np.arange(MLA_ROPE_DIM), h * MLA_QK_PAD + MLA_ROPE_LANE + np.arange(MLA_ROPE_DIM)] = 1.0
    return {
        "g_pre": p["g_pre"][l][None, :], "g_post": p["g_post"][l][None, :],
        "g_q": jnp.tile(p["g_q_norm"][l], GQA_HEADS)[None, :],
        "g_k": jnp.tile(p["g_k_norm"][l], GQA_KV_HEADS)[None, :],
        "ones": jnp.asarray(seg[:, None] == seg[None, :], BF16),
        "w_dec": w_dec.astype(BF16), "b_dec": b_dec,
        "g_mla_q": p["g_mla_q"][l][None, :], "g_mla_kv": p["g_mla_kv"][l][None, :],
        "w_uq": w_uq.astype(BF16), "w_uk": w_uk.astype(BF16), "w_uv": w_uv.astype(BF16),
        "kr_place": jnp.asarray(place, BF16),
        "w_o_gqa": p["w_o_gqa"][l].astype(BF16), "w_o_gla": p["w_o_gla"][l].astype(BF16),
        "w_o_mla": p["w_o_mla"][l].astype(BF16), "w_out": p["w_out"][l].astype(BF16),
        "g_gla_out": p["g_gla_out"][l][None, :],
    }


def _tiles(batch, seq):
    if seq <= 256:
        nb = 4 if batch % 4 == 0 else 1
        nb_mla = 8 if batch % 8 == 0 else nb
        return dict(tm_in=seq, tm_out=seq, att_gqa=(seq, max(seq // 64, 1), nb), att_mla=(seq, 1, nb_mla), tk=512,
                    gla_tb=seq, gla_nb=2 if batch % 2 == 0 else 1)
    return dict(tm_in=512, tm_out=512, att_gqa=(1024, 16, 1), att_mla=(4096, 16, 1), tk=512,
                gla_tb=512, gla_nb=2 if batch % 2 == 0 else 1)


def _sub_layer(x, mod, w_packed, layer, lw, rope, cache):
    is_ctx = cache is None
    t = _tiles(x.shape[0], x.shape[1])
    outs = _in_proj(x, mod, w_packed[0], w_packed[1], layer, lw, rope, is_ctx=is_ctx, tm=t["tm_in"])
    (qa, ka, vat, ga, qkg, vg, gg, la, qc, kc, vct, gc) = outs[:12]
    src_a, src_c = [(ka, vat)], [(kc, vct)]
    s0f = s0b = None
    if not is_ctx:
        ka_p, vat_p, kc_p, vct_p = _cache_prep(cache["gqa_k"], cache["gqa_v"], cache["mla_ckv"],
                                               cache["mla_krope"], layer, lw)
        src_a, src_c = [(ka_p, vat_p)] + src_a, [(kc_p, vct_p)] + src_c
        s0f, s0b = cache["gla_fwd"], cache["gla_bwd"]
    (tq_a, nsub_a, nb_a), (tq_c, nsub_c, nb_c) = t["att_gqa"], t["att_mla"]
    ya = _attention(qa, src_a, n_groups=GQA_KV_HEADS, shared_kv=True, tq=tq_a, nsub=nsub_a, nb=nb_a, tk=t["tk"])
    yc = _attention(qc, src_c, n_groups=MLA_HEADS, shared_kv=False, tq=tq_c, nsub=nsub_c, nb=nb_c, tk=t["tk"])
    o_f, o_b, s_f, s_b = _gla(qkg, vg, la, s0f, s0b, layer, tb=t["gla_tb"], nb=t["gla_nb"])
    y = _out_proj(x, mod, ya, ga, o_f, o_b, gg, yc, gc, w_packed[2], layer, lw, tm=t["tm_out"])
    ctx = (outs[12], outs[13], outs[14], outs[15], s_f, s_b) if is_ctx else None
    return y, ctx


def kernel(x_prompt, x_sample, cache_gqa_k, cache_gqa_v, cache_mla_ckv, cache_mla_krope, state_gla_fwd, state_gla_bwd, c, c_ctx, w_mod, b_mod, g_pre, g_post, w_in, g_q_norm, g_k_norm, w_gla_decay_fwd, b_gla_decay_fwd, w_gla_decay_bwd, b_gla_decay_bwd, g_gla_out, g_mla_q, g_mla_kv, w_mla_uq, w_mla_ukv, w_o_gqa, w_o_gla, w_o_mla, w_out):
    params = dict(g_pre=g_pre, g_post=g_post, g_q_norm=g_q_norm, g_k_norm=g_k_norm,
                  w_gla_decay_fwd=w_gla_decay_fwd, b_gla_decay_fwd=b_gla_decay_fwd,
                  w_gla_decay_bwd=w_gla_decay_bwd, b_gla_decay_bwd=b_gla_decay_bwd, g_gla_out=g_gla_out,
                  g_mla_q=g_mla_q, g_mla_kv=g_mla_kv, w_mla_uq=w_mla_uq, w_mla_ukv=w_mla_ukv,
                  w_o_gqa=w_o_gqa, w_o_gla=w_o_gla, w_o_mla=w_o_mla, w_out=w_out)
    depth, D = w_in.shape[0], w_in.shape[1]
    B, S = x_prompt.shape[:2]
    Bd, Nd = x_sample.shape[:2]

    conds = jnp.concatenate([c_ctx[None, :], c, jnp.zeros((SUBLANES - 1 - Bd, D), F32)], axis=0)
    mods = _modulation(conds, w_mod, b_mod)
    rope = _rope_tables(Nd)
    w_t = jnp.swapaxes(w_in, 1, 2).astype(BF16)
    zrow = lambda n: jnp.zeros((depth, n, D), BF16)
    kr_rows = w_t[:, _R_KR:_R_GC]
    w_small = jnp.concatenate(
        [w_t[:, _R_R:_R_LAT], zrow(LANES - 2 * GLA_RANK),
         kr_rows, zrow(MLA_ROPE_LANE - MLA_ROPE_DIM), kr_rows, zrow(LANES - MLA_ROPE_LANE - MLA_ROPE_DIM)], axis=1)
    w_packed = (w_t, w_small, w_t[:, _R_M:])
    layers = [_pack_layer(l, params) for l in range(depth)]

    xp = x_prompt
    ctx_out = []
    for l in range(depth):
        xp, ctx = _sub_layer(xp, mods[l, 0:1][:, None, :], w_packed, l, layers[l], None, None)
        ctx_out.append(ctx)

    xs = x_sample
    cache = {"gqa_k": cache_gqa_k, "gqa_v": cache_gqa_v, "mla_ckv": cache_mla_ckv, "mla_krope": cache_mla_krope,
             "gla_fwd": state_gla_fwd, "gla_bwd": state_gla_bwd}
    for l in range(depth):
        xs, _ = _sub_layer(xs, mods[l, 1:1 + Bd][:, None, :], w_packed, l, layers[l], rope, cache)

    stack = lambda j: jnp.stack([ctx_out[l][j] for l in range(depth)], axis=1)
    new_k = stack(0).reshape(B, depth, S, GQA_KV_HEADS, GQA_HEAD_DIM)
    new_v = stack(1).reshape(B, depth, S, GQA_KV_HEADS, GQA_HEAD_DIM)
    return (xp, xs, new_k, new_v, stack(2), stack(3), stack(4), stack(5))
```
